```python
import math
import jax, jax.numpy as jnp
from jax import lax
import numpy as np

D_MODEL = 1024
BATCH = 16
SEQ = 2048
DEPTH = 1

NSA_HEADS = 8
NSA_KV_GROUPS = 2
NSA_HEAD_DIM = 64
NSA_HPG = NSA_HEADS // NSA_KV_GROUPS
CMP_BLOCK = 32
CMP_STRIDE = 16
CMP_HIDDEN = 2 * NSA_HEAD_DIM
SEL_BLOCK = 64
SEL_TOP_N = 8
SEL_QBLK = 64
WINDOW = 512
WIN_QBLK = 128
ROPE_THETA = 500000.0
ROPE_DIM = NSA_HEAD_DIM // 4
GDN_HEADS = 4
GDN_HEAD_DIM = 128
GDN_CONV = 4
GDN_CHUNK = 64
NSA_WIDTH = NSA_HEADS * NSA_HEAD_DIM
NSA_KV_WIDTH = NSA_KV_GROUPS * NSA_HEAD_DIM
GDN_WIDTH = GDN_HEADS * GDN_HEAD_DIM
MIX_WIDTH = NSA_WIDTH + GDN_WIDTH
D_FF = ((8 * D_MODEL + 3 * 256 - 1) // (3 * 256)) * 256
OFF_NSA_KV = NSA_WIDTH
OFF_NSA_GATE = OFF_NSA_KV + 6 * NSA_KV_WIDTH
OFF_GDN_QKV = OFF_NSA_GATE + 3 * NSA_HEADS
OFF_GDN_Z = OFF_GDN_QKV + 3 * GDN_WIDTH
OFF_GDN_B = OFF_GDN_Z + GDN_WIDTH
OFF_GDN_A = OFF_GDN_B + GDN_HEADS
IN_WIDTH = OFF_GDN_A + GDN_HEADS
SPLIT_POINTS = (OFF_NSA_KV, OFF_NSA_GATE, OFF_GDN_QKV, OFF_GDN_Z, OFF_GDN_B, OFF_GDN_A)
NEG_INF = -1e30
FORCE_SCORE = 1e9
RMS_EPS = 1e-6

kernel_name = "nsa_gdn_parallel_hybrid_block"


def rmsnorm(x, g):
    xf = x.astype(jnp.float32)
    y = xf * lax.rsqrt(jnp.mean(xf * xf, axis=-1, keepdims=True) + RMS_EPS)
    return (y * g.astype(jnp.float32)).astype(x.dtype)


def l2norm(x):
    return x * lax.rsqrt(jnp.sum(x * x, axis=-1, keepdims=True) + RMS_EPS)


def partial_rope(x, positions):
    half = ROPE_DIM // 2
    inv = jnp.power(ROPE_THETA, -jnp.arange(half, dtype=jnp.float32) * (2.0 / ROPE_DIM))
    ang = positions.astype(jnp.float32)[..., None] * inv
    cos = jnp.cos(ang)[:, :, None, :]
    sin = jnp.sin(ang)[:, :, None, :]
    xr = x[..., :ROPE_DIM].astype(jnp.float32)
    x1, x2 = xr[..., :half], xr[..., half:]
    rot = jnp.concatenate([x1 * cos - x2 * sin, x2 * cos + x1 * sin], axis=-1)
    return jnp.concatenate([rot.astype(x.dtype), x[..., ROPE_DIM:]], axis=-1)


def masked_softmax(s, mask):
    return jax.nn.softmax(jnp.where(mask, s, NEG_INF), axis=-1)


def nsa_compressed(q, k, v, cmp_pos, cmp_w1, cmp_w2):
    B, T = q.shape[0], q.shape[1]
    n_cmp = (T - CMP_BLOCK) // CMP_STRIDE + 1
    starts = jnp.arange(n_cmp) * CMP_STRIDE
    gidx = starts[:, None] + jnp.arange(CMP_BLOCK)[None, :]

    def compress(x, i):
        blk = x[:, gidx] + cmp_pos[i][None, None, :, None, :]
        blk = jnp.swapaxes(blk, 2, 3).reshape(B, n_cmp, NSA_KV_GROUPS, CMP_BLOCK * NSA_HEAD_DIM)
        return jax.nn.silu(blk @ cmp_w1[i]) @ cmp_w2[i]

    kc = compress(k, 0)
    vc = compress(v, 1)
    s = jnp.einsum('btghd,bngd->bghtn', q, kc, preferred_element_type=jnp.float32) * (NSA_HEAD_DIM ** -0.5)
    t = jnp.arange(T)
    valid = (starts + CMP_BLOCK - 1)[None, :] <= t[:, None]
    has_any = jnp.any(valid, axis=-1)[:, None].astype(jnp.float32)
    p = masked_softmax(s, valid) * has_any
    o = jnp.einsum('bghtn,bngd->btghd', p.astype(vc.dtype), vc)
    return o, p


def nsa_select_indices(p_cmp, T):
    n_cmp = p_cmp.shape[-1]
    n_sel = T // SEL_BLOCK
    cs = jnp.arange(n_cmp) * CMP_STRIDE
    ss = jnp.arange(n_sel) * SEL_BLOCK
    overlap = ((cs[:, None] < ss[None, :] + SEL_BLOCK) & (cs[:, None] + CMP_BLOCK > ss[None, :])).astype(jnp.float32)
    imp = jnp.einsum('bghtn,ns->bgts', p_cmp, overlap)
    cur = jnp.arange(T) // SEL_BLOCK
    j = jnp.arange(n_sel)[None, :]
    causal = j <= cur[:, None]
    forced = (j == 0) | (j == cur[:, None]) | (j == cur[:, None] - 1)
    score = jnp.where(forced, FORCE_SCORE, jnp.where(causal, imp, NEG_INF))
    n_top = min(SEL_TOP_N, n_sel)
    _, idx = lax.top_k(score, n_top)
    return idx


def nsa_selected(q, k, v, idx):
    B, T = q.shape[0], q.shape[1]
    n_sel = T // SEL_BLOCK
    n_top = idx.shape[-1]
    nq = T // SEL_QBLK
    kb = k.reshape(B, n_sel, SEL_BLOCK, NSA_KV_GROUPS, NSA_HEAD_DIM).transpose(0, 3, 1, 2, 4)
    vb = v.reshape(B, n_sel, SEL_BLOCK, NSA_KV_GROUPS, NSA_HEAD_DIM).transpose(0, 3, 1, 2, 4)
    qc = q.reshape(B, nq, SEL_QBLK, NSA_KV_GROUPS, NSA_HPG, NSA_HEAD_DIM).swapaxes(0, 1)
    ic = idx.reshape(B, NSA_KV_GROUPS, nq, SEL_QBLK, n_top).transpose(2, 0, 1, 3, 4)
    gather = jax.vmap(jax.vmap(lambda blocks, ix: blocks[ix]))
    n_keys = n_top * SEL_BLOCK

    def step(args):
        qb, ib, c = args
        kg = gather(kb, ib).reshape(B, NSA_KV_GROUPS, SEL_QBLK, n_keys, NSA_HEAD_DIM)
        vg = gather(vb, ib).reshape(B, NSA_KV_GROUPS, SEL_QBLK, n_keys, NSA_HEAD_DIM)
        s = jnp.einsum('bcghd,bgckd->bghck', qb, kg, preferred_element_type=jnp.float32) * (NSA_HEAD_DIM ** -0.5)
        t = c * SEL_QBLK + jnp.arange(SEL_QBLK)
        kpos = (ib[..., None] * SEL_BLOCK + jnp.arange(SEL_BLOCK)).reshape(B, NSA_KV_GROUPS, SEL_QBLK, n_keys)
        mask = (kpos <= t[None, None, :, None])[:, :, None]
        p = masked_softmax(s, mask)
        return jnp.einsum('bghck,bgckd->bcghd', p.astype(vg.dtype), vg)

    o = lax.map(step, (qc, ic, jnp.arange(nq)))
    return o.swapaxes(0, 1).reshape(B, T, NSA_KV_GROUPS, NSA_HPG, NSA_HEAD_DIM)


def nsa_window(q, k, v):
    B, T = q.shape[0], q.shape[1]
    nb = T // WIN_QBLK
    span = WIN_QBLK + WINDOW
    kp = jnp.pad(k, ((0, 0), (WINDOW, 0), (0, 0), (0, 0)))
    vp = jnp.pad(v, ((0, 0), (WINDOW, 0), (0, 0), (0, 0)))
    qc = q.reshape(B, nb, WIN_QBLK, NSA_KV_GROUPS, NSA_HPG, NSA_HEAD_DIM).swapaxes(0, 1)

    def step(args):
        qb, c = args
        start = c * WIN_QBLK
        kb = lax.dynamic_slice_in_dim(kp, start, span, axis=1)
        vb = lax.dynamic_slice_in_dim(vp, start, span, axis=1)
        s = jnp.einsum('bcghd,bkgd->bghck', qb, kb, preferred_element_type=jnp.float32) * (NSA_HEAD_DIM ** -0.5)
        t = start + jnp.arange(WIN_QBLK)
        kpos = start - WINDOW + jnp.arange(span)
        diff = t[:, None] - kpos[None, :]
        mask = (kpos[None, :] >= 0) & (diff >= 0) & (diff < WINDOW)
        p = masked_softmax(s, mask)
        return jnp.einsum('bghck,bkgd->bcghd', p.astype(vb.dtype), vb)

    o = lax.map(step, (qc, jnp.arange(nb)))
    return o.swapaxes(0, 1).reshape(B, T, NSA_KV_GROUPS, NSA_HPG, NSA_HEAD_DIM)


def causal_depthwise_conv(x, w):
    return lax.conv_general_dilated(x, w[:, None, :], window_strides=(1,), padding=[(GDN_CONV - 1, 0)],
                                    dimension_numbers=('NWC', 'WIO', 'NWC'), feature_group_count=x.shape[-1])


def gated_delta_rule(q, k, v, g, beta):
    B, T, H, Dk = k.shape
    Dv = v.shape[-1]
    C = GDN_CHUNK
    N = T // C

    def chunks(x):
        return x.reshape(B, N, C, H, x.shape[-1]).transpose(1, 0, 3, 2, 4)

    q = chunks(q) * (Dk ** -0.5)
    k = chunks(k)
    v = chunks(v)
    g = g.reshape(B, N, C, H).transpose(1, 0, 3, 2)
    beta = beta.reshape(B, N, C, H).transpose(1, 0, 3, 2)
    gc = jnp.cumsum(g, axis=-1)
    incl = jnp.tril(jnp.ones((C, C), dtype=bool))
    strict = jnp.tril(jnp.ones((C, C), dtype=bool), -1)
    decay = jnp.exp(jnp.where(incl, gc[..., :, None] - gc[..., None, :], -jnp.inf))
    kb = k * beta[..., None]
    a = jnp.where(strict, jnp.einsum('nbhcd,nbhed->nbhce', kb, k) * decay, 0.0)
    eye = jnp.eye(C, dtype=a.dtype)
    tmat = lax.linalg.triangular_solve(a + eye, jnp.broadcast_to(eye, a.shape), left_side=True,
                                       lower=True, unit_diagonal=True)
    u = tmat @ (v * beta[..., None])
    w = tmat @ (kb * jnp.exp(gc)[..., None])
    qk = jnp.where(incl, jnp.einsum('nbhcd,nbhed->nbhce', q, k) * decay, 0.0)

    def step(S, xs):
        q_i, k_i, u_i, w_i, gc_i, qk_i = xs
        v_new = u_i - w_i @ S
        o = (q_i * jnp.exp(gc_i)[..., None]) @ S + qk_i @ v_new
        g_last = gc_i[..., -1:]
        S = S * jnp.exp(g_last)[..., None] + jnp.einsum('bhck,bhcv->bhkv', k_i * jnp.exp(g_last - gc_i)[..., None], v_new)
        return S, o

    S0 = jnp.zeros((B, H, Dk, Dv), jnp.float32)
    _, o = lax.scan(step, S0, (q, k, u, w, gc, qk))
    return o.transpose(1, 0, 3, 2, 4).reshape(B, T, H, Dv)


def gdn_mixer(qkv, z, b, a, conv_w, a_log, dt_bias, norm_g):
    B, T = qkv.shape[0], qkv.shape[1]
    qkv = jax.nn.silu(causal_depthwise_conv(qkv, conv_w)).astype(jnp.float32)
    q, k, v = jnp.split(qkv, 3, axis=-1)
    q = l2norm(q.reshape(B, T, GDN_HEADS, GDN_HEAD_DIM))
    k = l2norm(k.reshape(B, T, GDN_HEADS, GDN_HEAD_DIM))
    v = v.reshape(B, T, GDN_HEADS, GDN_HEAD_DIM)
    beta = jax.nn.sigmoid(b.astype(jnp.float32))
    g = -jnp.exp(a_log.astype(jnp.float32)) * jax.nn.softplus(a.astype(jnp.float32) + dt_bias.astype(jnp.float32))
    o = gated_delta_rule(q, k, v, g, beta)
    o = rmsnorm(o, norm_g) * jax.nn.silu(z.reshape(B, T, GDN_HEADS, GDN_HEAD_DIM).astype(jnp.float32))
    return o.reshape(B, T, GDN_WIDTH).astype(z.dtype)


def hybrid_layer(x, positions, norm1_g, w_in, cmp_pos, cmp_w1, cmp_w2, nsa_norm_g, gdn_conv_w, gdn_a_log,
                 gdn_dt_bias, gdn_norm_g, w_out, norm2_g, w_gate, w_up, w_down):
    B, T = x.shape[0], x.shape[1]
    h = rmsnorm(x, norm1_g)
    proj = h @ w_in
    nsa_q, nsa_kv, nsa_gate, gdn_qkv, gdn_z, gdn_b, gdn_a = jnp.split(proj, SPLIT_POINTS, axis=-1)
    q = partial_rope(nsa_q.reshape(B, T, NSA_HEADS, NSA_HEAD_DIM), positions)
    q = q.reshape(B, T, NSA_KV_GROUPS, NSA_HPG, NSA_HEAD_DIM)
    kv = nsa_kv.reshape(B, T, 6, NSA_KV_GROUPS, NSA_HEAD_DIM)
    k_cmp = partial_rope(kv[:, :, 0], positions)
    k_slc = partial_rope(kv[:, :, 2], positions)
    k_win = partial_rope(kv[:, :, 4], positions)
    o_cmp, p_cmp = nsa_compressed(q, k_cmp, kv[:, :, 1], cmp_pos, cmp_w1, cmp_w2)
    idx = nsa_select_indices(p_cmp, T)
    o_slc = nsa_selected(q, k_slc, kv[:, :, 3], idx)
    o_win = nsa_window(q, k_win, kv[:, :, 5])
    gates = jax.nn.sigmoid(nsa_gate.astype(jnp.float32)).reshape(B, T, NSA_KV_GROUPS, NSA_HPG, 3)
    o_nsa = gates[..., 0:1] * o_cmp + gates[..., 1:2] * o_slc + gates[..., 2:3] * o_win
    o_nsa = rmsnorm(o_nsa.reshape(B, T, NSA_WIDTH).astype(x.dtype), nsa_norm_g)
    o_gdn = gdn_mixer(gdn_qkv, gdn_z, gdn_b, gdn_a, gdn_conv_w, gdn_a_log, gdn_dt_bias, gdn_norm_g)
    x = x + jnp.concatenate([o_nsa, o_gdn], axis=-1) @ w_out
    h = rmsnorm(x, norm2_g)
    x = x + (jax.nn.silu(h @ w_gate) * (h @ w_up)) @ w_down
    return x


def setup_inputs(seed: int = 0) -> dict:
    key = jax.random.key(seed)
    ks = jax.random.split(key, 20)
    f32 = jnp.float32
    L = DEPTH

    def nrm(k, shape, scale):
        return jax.random.normal(k, shape, f32) * scale

    x = nrm(ks[0], (BATCH, SEQ, D_MODEL), 1.0)
    positions = jnp.tile(jnp.arange(SEQ, dtype=jnp.int32)[None, :], (BATCH, 1))
    norm1_g = 1.0 + nrm(ks[1], (L, D_MODEL), 0.02)
    w_in = nrm(ks[2], (L, D_MODEL, IN_WIDTH), D_MODEL ** -0.5)
    cmp_pos = nrm(ks[3], (L, 2, CMP_BLOCK, NSA_HEAD_DIM), 0.1)
    cmp_w1 = nrm(ks[4], (L, 2, CMP_BLOCK * NSA_HEAD_DIM, CMP_HIDDEN), (CMP_BLOCK * NSA_HEAD_DIM) ** -0.5)
    cmp_w2 = nrm(ks[5], (L, 2, CMP_HIDDEN, NSA_HEAD_DIM), CMP_HIDDEN ** -0.5)
    nsa_norm_g = 1.0 + nrm(ks[6], (L, NSA_WIDTH), 0.02)
    gdn_conv_w = nrm(ks[7], (L, GDN_CONV, 3 * GDN_WIDTH), GDN_CONV ** -0.5)
    gdn_a_log = jnp.log(jax.random.uniform(ks[8], (L, GDN_HEADS), f32, 1.0, 16.0))
    dt = jnp.exp(jax.random.uniform(ks[9], (L, GDN_HEADS), f32, math.log(1e-3), math.log(1e-1)))
    gdn_dt_bias = dt + jnp.log(-jnp.expm1(-dt))
    gdn_norm_g = 1.0 + nrm(ks[10], (L, GDN_HEAD_DIM), 0.02)
    w_out = nrm(ks[11], (L, MIX_WIDTH, D_MODEL), MIX_WIDTH ** -0.5)
    norm2_g = 1.0 + nrm(ks[12], (L, D_MODEL), 0.02)
    w_gate = nrm(ks[13], (L, D_MODEL, D_FF), D_MODEL ** -0.5)
    w_up = nrm(ks[14], (L, D_MODEL, D_FF), D_MODEL ** -0.5)
    w_down = nrm(ks[15], (L, D_FF, D_MODEL), D_FF ** -0.5)
    final_g = 1.0 + nrm(ks[16], (D_MODEL,), 0.02)
    return {"x": x, "positions": positions, "norm1_g": norm1_g, "w_in": w_in, "cmp_pos": cmp_pos,
            "cmp_w1": cmp_w1, "cmp_w2": cmp_w2, "nsa_norm_g": nsa_norm_g, "gdn_conv_w": gdn_conv_w,
            "gdn_a_log": gdn_a_log, "gdn_dt_bias": gdn_dt_bias, "gdn_norm_g": gdn_norm_g, "w_out": w_out,
            "norm2_g": norm2_g, "w_gate": w_gate, "w_up": w_up, "w_down": w_down, "final_g": final_g}


def reference(x, positions, norm1_g, w_in, cmp_pos, cmp_w1, cmp_w2, nsa_norm_g, gdn_conv_w, gdn_a_log,
              gdn_dt_bias, gdn_norm_g, w_out, norm2_g, w_gate, w_up, w_down, final_g):
    for l in range(DEPTH):
        x = hybrid_layer(x, positions, norm1_g[l], w_in[l], cmp_pos[l], cmp_w1[l], cmp_w2[l], nsa_norm_g[l],
                         gdn_conv_w[l], gdn_a_log[l], gdn_dt_bias[l], gdn_norm_g[l], w_out[l], norm2_g[l],
                         w_gate[l], w_up[l], w_down[l])
    return rmsnorm(x, final_g)
```

```python
import functools
import math

import jax
import jax.numpy as jnp
from jax import lax
from jax.experimental import pallas as pl
from jax.experimental.pallas import tpu as pltpu

F32 = jnp.float32
BF16 = jnp.bfloat16

LANES = 128
SUBLANES = 8

D_MODEL = 1024
N_HEADS = 8
N_GROUPS = 2
HEAD_DIM = 64
CMP_BLOCK = 32
CMP_STRIDE = 16
CMP_HIDDEN = 128
SEL_BLOCK = 64
SEL_TOP_N = 8
WINDOW = 512
ROPE_THETA = 500000.0
ROPE_DIM = 16
GDN_HEADS = 4
GDN_DIM = 128
GDN_CONV = 4
GDN_CHUNK = 64
NSA_WIDTH = 512
GDN_WIDTH = 512
D_FF = 2816
RMS_EPS = 1e-6
NEG_INF = -1e30
MASK_BIAS = -1e9
FORCE_SCORE = 1e9

OFF_KV = 512
OFF_GATE = OFF_KV + 6 * 128
OFF_GQKV = OFF_GATE + 24
OFF_Z = OFF_GQKV + 3 * GDN_WIDTH
OFF_B = OFF_Z + GDN_WIDTH
OFF_A = OFF_B + GDN_HEADS

SEC_ROPE = 0
SEC_V = 896
SEC_GDN = 1280
SEC_Z = 2816
SEC_SMALL = 3328
W_COLS = 3456
SMALL_B = 24
SMALL_A = 28

TM = 512
VMEM_LIMIT = 56 * 1024 * 1024


def _dot(a, b):
    return jnp.dot(a, b, preferred_element_type=F32)


def _dot_nt(a, b):
    return lax.dot_general(a, b, (((1,), (1,)), ((), ())), preferred_element_type=F32)


def _dot_exact(a, b):
    return jnp.dot(a, b, preferred_element_type=F32, precision=lax.Precision.HIGHEST)


def _sigmoid(x):
    return 1.0 / (1.0 + jnp.exp(-x))


def _silu(x):
    return x * _sigmoid(x)


def _softplus(x):
    return jnp.maximum(x, 0.0) + jnp.log(1.0 + jnp.exp(-jnp.abs(x)))


def _in_proj_kernel(tiles_per_seq, x_ref, pos_ref, g1_ref, w_ref, wa_ref, invf_ref, convw_ref,
                    nega_ref, dtb_ref, negat_ref, dtbt_ref,
                    qkk_ref, vv_ref, cmpk_ref, cmpv_ref, gq_ref, gk_ref, gv_ref, z_ref, small_ref, gt_ref,
                    cbuf):
    tm = x_ref.shape[0]
    x = x_ref[...]
    hn = x * lax.rsqrt(jnp.mean(x * x, axis=-1, keepdims=True) + RMS_EPS) * g1_ref[...]
    hb = hn.astype(BF16)

    ang = pos_ref[...].astype(F32) * invf_ref[...]
    cosv = jnp.cos(ang)
    sinv = jnp.sin(ang)
    lane = lax.broadcasted_iota(jnp.int32, (1, LANES), 1) % HEAD_DIM
    s_lo = jnp.where(lane < ROPE_DIM // 2, -sinv, 0.0)
    s_hi = jnp.where((lane >= ROPE_DIM // 2) & (lane < ROPE_DIM), sinv, 0.0)
    for j in range(7):
        blk = _dot(hb, w_ref[:, SEC_ROPE + j * LANES:SEC_ROPE + (j + 1) * LANES])
        rot = (blk * cosv + pltpu.roll(blk, LANES - ROPE_DIM // 2, 1) * s_lo
               + pltpu.roll(blk, ROPE_DIM // 2, 1) * s_hi)
        if j < 4:
            qkk_ref[:, j * LANES:(j + 1) * LANES] = (rot * (HEAD_DIM ** -0.5)).astype(BF16)
        elif j == 4:
            cmpk_ref[...] = rot
        else:
            qkk_ref[:, (j - 1) * LANES:j * LANES] = rot.astype(BF16)

    vblk = _dot(hb, w_ref[:, SEC_V:SEC_V + 3 * LANES])
    cmpv_ref[...] = vblk[:, 0:LANES]
    vv_ref[...] = vblk[:, LANES:3 * LANES].astype(BF16)

    first = pl.program_id(0) % tiles_per_seq == 0

    @pl.when(first)
    def _():
        cbuf[0:SUBLANES, :] = jnp.zeros((SUBLANES, 3 * GDN_WIDTH), F32)

    cbuf[SUBLANES:SUBLANES + tm, :] = _dot(hb, w_ref[:, SEC_GDN:SEC_GDN + 3 * GDN_WIDTH])
    for cb in range(12):
        cols = slice(cb * LANES, (cb + 1) * LANES)
        y = jnp.zeros((tm, LANES), F32)
        for k in range(GDN_CONV):
            y = y + cbuf[SUBLANES - (GDN_CONV - 1) + k:SUBLANES - (GDN_CONV - 1) + k + tm, cols] * convw_ref[k:k + 1, cols]
        y = _silu(y)
        if cb < 8:
            y = y * lax.rsqrt(jnp.sum(y * y, axis=-1, keepdims=True) + RMS_EPS)
        if cb < 4:
            gq_ref[:, cols] = (y * (GDN_DIM ** -0.5)).astype(BF16)
        elif cb < 8:
            gk_ref[:, (cb - 4) * LANES:(cb - 3) * LANES] = y.astype(BF16)
        else:
            gv_ref[:, (cb - 8) * LANES:(cb - 7) * LANES] = y.astype(BF16)
    cbuf[0:SUBLANES, :] = cbuf[tm:tm + SUBLANES, :]

    z_ref[...] = _dot(hb, w_ref[:, SEC_Z:SEC_Z + GDN_WIDTH]).astype(BF16)

    sm = _dot(hb, w_ref[:, SEC_SMALL:SEC_SMALL + LANES])
    lane1 = lax.broadcasted_iota(jnp.int32, (1, LANES), 1)
    gdec = nega_ref[...] * _softplus(sm + dtb_ref[...])
    small_ref[...] = jnp.where(lane1 < SMALL_A, _sigmoid(sm), gdec)
    at = _dot_nt(wa_ref[...], hb)
    gtr = negat_ref[...] * _softplus(at + dtbt_ref[...])
    for m in range(tm // LANES):
        gt_ref[m] = gtr[:, m * LANES:(m + 1) * LANES]


def _in_proj(x2, pos2, g1, w_r, wa_t, invf, convw, nega, dtb, negat, dtbt, seq_len):
    n = x2.shape[0]
    grid = (n // TM,)
    row = lambda i: (i, 0)
    const = lambda i: (0, 0)
    out_shapes = (
        jax.ShapeDtypeStruct((n, 768), BF16),
        jax.ShapeDtypeStruct((n, 256), BF16),
        jax.ShapeDtypeStruct((n, LANES), F32),
        jax.ShapeDtypeStruct((n, LANES), F32),
        jax.ShapeDtypeStruct((n, GDN_WIDTH), BF16),
        jax.ShapeDtypeStruct((n, GDN_WIDTH), BF16),
        jax.ShapeDtypeStruct((n, GDN_WIDTH), BF16),
        jax.ShapeDtypeStruct((n, GDN_WIDTH), BF16),
        jax.ShapeDtypeStruct((n, LANES), F32),
        jax.ShapeDtypeStruct((n // LANES, SUBLANES, LANES), F32),
    )
    out_specs = (
        pl.BlockSpec((TM, 768), row), pl.BlockSpec((TM, 256), row),
        pl.BlockSpec((TM, LANES), row), pl.BlockSpec((TM, LANES), row),
        pl.BlockSpec((TM, GDN_WIDTH), row), pl.BlockSpec((TM, GDN_WIDTH), row),
        pl.BlockSpec((TM, GDN_WIDTH), row), pl.BlockSpec((TM, GDN_WIDTH), row),
        pl.BlockSpec((TM, LANES), row),
        pl.BlockSpec((TM // LANES, SUBLANES, LANES), lambda i: (i, 0, 0)),
    )
    in_specs = [
        pl.BlockSpec((TM, D_MODEL), row), pl.BlockSpec((TM, 1), row), pl.BlockSpec((1, D_MODEL), const),
        pl.BlockSpec((D_MODEL, W_COLS), const), pl.BlockSpec((SUBLANES, D_MODEL), const),
        pl.BlockSpec((1, LANES), const), pl.BlockSpec((GDN_CONV, 3 * GDN_WIDTH), const),
        pl.BlockSpec((1, LANES), const), pl.BlockSpec((1, LANES), const),
        pl.BlockSpec((SUBLANES, 1), const), pl.BlockSpec((SUBLANES, 1), const),
    ]
    return pl.pallas_call(
        functools.partial(_in_proj_kernel, seq_len // TM),
        grid=grid, in_specs=in_specs, out_specs=out_specs, out_shape=out_shapes,
        scratch_shapes=[pltpu.VMEM((TM + 2 * SUBLANES, 3 * GDN_WIDTH), F32)],
        compiler_params=pltpu.CompilerParams(dimension_semantics=("arbitrary",),
                                             vmem_limit_bytes=VMEM_LIMIT),
        name="in_proj",
    )(x2, pos2, g1, w_r, wa_t, invf, convw, nega, dtb, negat, dtbt)


def _prep_in_proj_weights(w_in, gdn_a_log, gdn_dt_bias):
    q = w_in[:, :NSA_WIDTH].reshape(D_MODEL, 2, 4, HEAD_DIM)
    q = jnp.transpose(q, (0, 2, 1, 3)).reshape(D_MODEL, NSA_WIDTH)
    kv = w_in[:, OFF_KV:OFF_GATE].reshape(D_MODEL, 6, LANES)
    gate = w_in[:, OFF_GATE:OFF_GQKV].reshape(D_MODEL, 2, 4, 3)
    gate = jnp.transpose(gate, (0, 3, 2, 1)).reshape(D_MODEL, 24)
    small = jnp.concatenate([gate, w_in[:, OFF_B:OFF_A], w_in[:, OFF_A:OFF_A + GDN_HEADS],
                             jnp.zeros((D_MODEL, LANES - 32), w_in.dtype)], axis=1)
    w_r = jnp.concatenate([q, kv[:, 0], kv[:, 2], kv[:, 4], kv[:, 1], kv[:, 3], kv[:, 5],
                           w_in[:, OFF_GQKV:OFF_Z], w_in[:, OFF_Z:OFF_B], small], axis=1).astype(BF16)
    wa_t = jnp.concatenate([w_in[:, OFF_A:OFF_A + GDN_HEADS].T,
                            jnp.zeros((SUBLANES - GDN_HEADS, D_MODEL), w_in.dtype)], axis=0).astype(BF16)
    neg_a = -jnp.exp(gdn_a_log.astype(F32))
    nega = jnp.zeros((1, LANES), F32).at[0, SMALL_A:SMALL_A + GDN_HEADS].set(neg_a)
    dtb = jnp.zeros((1, LANES), F32).at[0, SMALL_A:SMALL_A + GDN_HEADS].set(gdn_dt_bias.astype(F32))
    negat = jnp.zeros((SUBLANES, 1), F32).at[:GDN_HEADS, 0].set(neg_a)
    dtbt = jnp.zeros((SUBLANES, 1), F32).at[:GDN_HEADS, 0].set(gdn_dt_bias.astype(F32))
    return w_r, wa_t, nega, dtb, negat, dtbt


def _rope_inv_freq():
    half = ROPE_DIM // 2
    inv = jnp.power(ROPE_THETA, -jnp.arange(half, dtype=F32) * (2.0 / ROPE_DIM))
    lane = jnp.arange(LANES)
    return jnp.where((lane % HEAD_DIM) < ROPE_DIM, inv[lane % half], 0.0).astype(F32)[None, :]


def _compress_kernel(xk_ref, xv_ref, pos_ref, w1_ref, w2_ref, out_ref):
    nblk = xk_ref.shape[0] // CMP_STRIDE
    acc_lo = jnp.zeros((nblk, 4 * CMP_HIDDEN), F32)
    acc_hi = jnp.zeros((nblk, 4 * CMP_HIDDEN), F32)
    for j in range(CMP_STRIDE):
        xj = jnp.concatenate([xk_ref[pl.ds(j, nblk, stride=CMP_STRIDE), :],
                              xv_ref[pl.ds(j, nblk, stride=CMP_STRIDE), :]], axis=1)
        acc_lo = acc_lo + _dot((xj + pos_ref[j:j + 1, :]).astype(BF16), w1_ref[j])
        acc_hi = acc_hi + _dot((xj + pos_ref[CMP_STRIDE + j:CMP_STRIDE + j + 1, :]).astype(BF16),
                               w1_ref[CMP_STRIDE + j])
    pre = acc_lo + pltpu.roll(acc_hi, nblk - 1, 0)
    out_ref[...] = _dot(_silu(pre).astype(BF16), w2_ref[...])


def _compress(cmp_k, cmp_v, pos_rows, w1_bd, w2_bd, seq_len):
    n = cmp_k.shape[0]
    nb = n // seq_len
    nblk = seq_len // CMP_STRIDE
    return pl.pallas_call(
        _compress_kernel,
        grid=(nb,),
        in_specs=[pl.BlockSpec((seq_len, LANES), lambda b: (b, 0)),
                  pl.BlockSpec((seq_len, LANES), lambda b: (b, 0)),
                  pl.BlockSpec((CMP_BLOCK, 256), lambda b: (0, 0)),
                  pl.BlockSpec((CMP_BLOCK, 256, 4 * CMP_HIDDEN), lambda b: (0, 0, 0)),
                  pl.BlockSpec((4 * CMP_HIDDEN, 256), lambda b: (0, 0))],
        out_specs=pl.BlockSpec((nblk, 256), lambda b: (b, 0)),
        out_shape=jax.ShapeDtypeStruct((nb * nblk, 256), F32),
        compiler_params=pltpu.CompilerParams(dimension_semantics=("arbitrary",),
                                             vmem_limit_bytes=VMEM_LIMIT),
        name="nsa_compress",
    )(cmp_k, cmp_v, pos_rows, w1_bd, w2_bd)


def _prep_compress_weights(cmp_pos, cmp_w1, cmp_w2):
    w1 = cmp_w1.reshape(2, CMP_BLOCK, HEAD_DIM, CMP_HIDDEN)
    w1_bd = jnp.zeros((CMP_BLOCK, 4, HEAD_DIM, 4, CMP_HIDDEN), F32)
    w2_bd = jnp.zeros((4, CMP_HIDDEN, 4, HEAD_DIM), F32)
    for s in range(4):
        w1_bd = w1_bd.at[:, s, :, s, :].set(w1[s // 2])
        w2_bd = w2_bd.at[s, :, s, :].set(cmp_w2[s // 2])
    w1_bd = w1_bd.reshape(CMP_BLOCK, 256, 4 * CMP_HIDDEN).astype(BF16)
    w2_bd = w2_bd.reshape(4 * CMP_HIDDEN, 256).astype(BF16)
    pos_rows = jnp.concatenate([cmp_pos[0], cmp_pos[0], cmp_pos[1], cmp_pos[1]], axis=-1).astype(F32)
    return pos_rows, w1_bd, w2_bd


TQ = 128
KC = 256
WSPAN = WINDOW + TQ
N_SEL = 32
ROWS = N_HEADS * TQ


def _split3(x):
    h1 = x.astype(BF16)
    r1 = x - h1.astype(F32)
    h2 = r1.astype(BF16)
    h3 = (r1 - h2.astype(F32)).astype(BF16)
    return h1, h2, h3


def _nsa_kernel(q_ref, ks_ref, kw_ref, vs_ref, vw_ref, kvc_ref, gate_ref, ovt_ref, eblk_ref, o_ref,
                qaug, m_sc, l_sc, a_sc, acc_sc, p_sc, pw_sc, oc_sc):
    t0 = pl.program_id(1) * TQ
    lane = lax.broadcasted_iota(jnp.int32, (1, LANES), 1)
    lo = lane < HEAD_DIM
    trow = t0 + lax.broadcasted_iota(jnp.int32, (TQ, 1), 0)

    for m in range(4):
        qm = q_ref[:, m * LANES:(m + 1) * LANES]
        zero = jnp.zeros_like(qm)
        qaug[(2 * m) * TQ:(2 * m + 1) * TQ, 0:LANES] = jnp.where(lo, qm, zero)
        qaug[(2 * m + 1) * TQ:(2 * m + 2) * TQ, 0:LANES] = jnp.where(lo, zero, qm)

    kc = kvc_ref[:, 0:LANES].astype(BF16)
    vc = kvc_ref[:, LANES:2 * LANES].astype(BF16)
    s_c = _dot_nt(qaug[:, 0:LANES], kc)
    valid = (lane * CMP_STRIDE + (CMP_BLOCK - 1)) <= trow
    has_any = trow >= (CMP_BLOCK - 1)
    psum = [jnp.zeros((TQ, LANES), F32), jnp.zeros((TQ, LANES), F32)]
    for r in range(N_HEADS):
        sr = jnp.where(valid, s_c[r * TQ:(r + 1) * TQ], NEG_INF)
        e = jnp.exp(sr - jnp.max(sr, axis=-1, keepdims=True))
        p = e / jnp.sum(e, axis=-1, keepdims=True)
        p = jnp.where(has_any, p, 0.0)
        psum[r % 2] = psum[r % 2] + p
        p_sc[r * TQ:(r + 1) * TQ, 0:LANES] = p.astype(BF16)
    oc_sc[...] = _dot(p_sc[:, 0:LANES], vc)

    jrow = lax.broadcasted_iota(jnp.int32, (N_SEL, 1), 0)
    tcol = t0 + lax.broadcasted_iota(jnp.int32, (1, TQ), 1)
    cur = lax.shift_right_logical(tcol, 6)
    forced = (jrow == 0) | (jrow == cur) | (jrow == cur - 1)
    causal = jrow <= cur
    for g in range(N_GROUPS):
        h1, h2, h3 = _split3(psum[g])
        imp_t = _dot_nt(ovt_ref[...], h1) + _dot_nt(ovt_ref[...], h2) + _dot_nt(ovt_ref[...], h3)
        score = jnp.where(forced, FORCE_SCORE, jnp.where(causal, imp_t[0:N_SEL, :], NEG_INF))
        cnt = jnp.zeros((N_SEL, TQ), jnp.int32)
        for jp in range(N_SEL):
            rowv = score[jp:jp + 1, :]
            beats = (rowv > score) | ((rowv == score) & (jrow > jp))
            cnt = cnt + jnp.where(beats, 1, 0)
        sel = (cnt < SEL_TOP_N) & causal
        bias_t = jnp.concatenate([jnp.where(sel, 0.0, MASK_BIAS),
                                  jnp.zeros((LANES - N_SEL, TQ), F32)], axis=0)
        bias = bias_t.T.astype(BF16)
        for m in range(4):
            r = 2 * m + g
            qaug[r * TQ:(r + 1) * TQ, LANES:2 * LANES] = bias

    m_sc[...] = jnp.full(m_sc.shape, NEG_INF, F32)
    l_sc[...] = jnp.zeros(l_sc.shape, F32)
    acc_sc[...] = jnp.zeros(acc_sc.shape, F32)

    def sel_step(c, diagonal):
        k0 = pl.multiple_of(c * KC, KC)
        kaug = jnp.concatenate([ks_ref[pl.ds(k0, KC), :], eblk_ref[pl.ds(k0, KC), :]], axis=1)
        s = _dot_nt(qaug[...], kaug)
        if diagonal:
            kpos = k0 + lax.broadcasted_iota(jnp.int32, (1, KC), 1)
            cbias = jnp.where(kpos <= trow, 0.0, NEG_INF)
        for r in range(N_HEADS):
            rows = slice(r * TQ, (r + 1) * TQ)
            sr = s[rows]
            if diagonal:
                sr = sr + cbias
            m_old = m_sc[rows]
            m_new = jnp.maximum(m_old, jnp.max(sr, axis=-1, keepdims=True))
            alpha = jnp.exp(m_old - m_new)
            p = jnp.exp(sr - m_new)
            l_sc[rows] = alpha * l_sc[rows] + jnp.sum(p, axis=-1, keepdims=True)
            m_sc[rows] = m_new
            a_sc[rows] = alpha
            p_sc[rows] = p.astype(BF16)
        acc_sc[...] = acc_sc[...] * a_sc[...] + _dot(p_sc[...], vs_ref[pl.ds(k0, KC), :])

    n_full = t0 // KC

    def body(c, carry):
        sel_step(c, False)
        return carry

    lax.fori_loop(0, n_full, body, 0)
    sel_step(n_full, True)

    w0 = pl.multiple_of(jnp.maximum(t0 - WINDOW, 0), TQ)
    s_w = _dot_nt(qaug[:, 0:LANES], kw_ref[pl.ds(w0, WSPAN), :])
    wdiff = trow - (w0 + lax.broadcasted_iota(jnp.int32, (1, WSPAN), 1))
    wbias = jnp.where((wdiff >= 0) & (wdiff < WINDOW), 0.0, NEG_INF)
    for r in range(N_HEADS):
        rows = slice(r * TQ, (r + 1) * TQ)
        sr = s_w[rows] + wbias
        p = jnp.exp(sr - jnp.max(sr, axis=-1, keepdims=True))
        a_sc[rows] = jnp.sum(p, axis=-1, keepdims=True)
        pw_sc[rows] = p.astype(BF16)
    o_w = _dot(pw_sc[...], vw_ref[pl.ds(w0, WSPAN), :])

    gates = gate_ref[...]
    for m in range(4):
        halves = []
        for g in range(N_GROUPS):
            r = 2 * m + g
            rows = slice(r * TQ, (r + 1) * TQ)
            halves.append(gates[:, r:r + 1] * oc_sc[rows]
                          + gates[:, 8 + r:9 + r] * (acc_sc[rows] / l_sc[rows])
                          + gates[:, 16 + r:17 + r] * (o_w[rows] / a_sc[rows]))
        o_ref[:, m * LANES:(m + 1) * LANES] = jnp.where(lo, halves[0], halves[1]).astype(BF16)


def _nsa(qkk, vv, kvc, small, ovt, eblk, seq_len):
    n = qkk.shape[0]
    nb = n // seq_len
    nq = seq_len // TQ
    nblk = seq_len // CMP_STRIDE
    col = lambda c: (lambda b, i: (b, c))
    return pl.pallas_call(
        _nsa_kernel,
        grid=(nb, nq),
        in_specs=[pl.BlockSpec((TQ, NSA_WIDTH), lambda b, i: (b * nq + i, 0)),
                  pl.BlockSpec((seq_len, LANES), col(4)), pl.BlockSpec((seq_len, LANES), col(5)),
                  pl.BlockSpec((seq_len, LANES), col(0)), pl.BlockSpec((seq_len, LANES), col(1)),
                  pl.BlockSpec((nblk, 256), col(0)),
                  pl.BlockSpec((TQ, LANES), lambda b, i: (b * nq + i, 0)),
                  pl.BlockSpec((LANES, LANES), lambda b, i: (0, 0)),
                  pl.BlockSpec((seq_len, LANES), lambda b, i: (0, 0))],
        out_specs=pl.BlockSpec((TQ, NSA_WIDTH), lambda b, i: (b * nq + i, 0)),
        out_shape=jax.ShapeDtypeStruct((n, NSA_WIDTH), BF16),
        scratch_shapes=[pltpu.VMEM((ROWS, 2 * LANES), BF16),
                        pltpu.VMEM((ROWS, 1), F32), pltpu.VMEM((ROWS, 1), F32), pltpu.VMEM((ROWS, 1), F32),
                        pltpu.VMEM((ROWS, LANES), F32),
                        pltpu.VMEM((ROWS, KC), BF16),
                        pltpu.VMEM((ROWS, WSPAN), BF16),
                        pltpu.VMEM((ROWS, LANES), F32)],
        compiler_params=pltpu.CompilerParams(dimension_semantics=("arbitrary", "arbitrary"),
                                             vmem_limit_bytes=VMEM_LIMIT),
        name="nsa_attention",
    )(qkk, qkk, qkk, vv, vv, kvc, small, ovt, eblk)


def _nsa_constants(seq_len):
    n_cmp = (seq_len - CMP_BLOCK) // CMP_STRIDE + 1
    s = jnp.arange(LANES)[:, None]
    nn = jnp.arange(LANES)[None, :]
    cs = nn * CMP_STRIDE
    ss = s * SEL_BLOCK
    ovt = (cs < ss + SEL_BLOCK) & (cs + CMP_BLOCK > ss) & (s < seq_len // SEL_BLOCK) & (nn < n_cmp)
    k = jnp.arange(seq_len)[:, None]
    eblk = (k // SEL_BLOCK) == jnp.arange(LANES)[None, :]
    return ovt.astype(BF16), eblk.astype(BF16)


PAIR = 2 * GDN_CHUNK
N_DOUBLINGS = 5


def _dot3(mat01, x, nt=False):
    f = (lambda a: _dot_nt(a, mat01)) if nt else (lambda a: _dot(mat01, a))
    h1, h2, h3 = _split3(x)
    return f(h1) + f(h2) + f(h3)


def _gdn_kernel(q_ref, k_ref, v_ref, z_ref, small_ref, gt_ref, ng_ref, o_ref, s_sc):
    seq_len = q_ref.shape[0]
    ri = lax.broadcasted_iota(jnp.int32, (PAIR, PAIR), 0)
    ci = lax.broadcasted_iota(jnp.int32, (PAIR, PAIR), 1)
    same = lax.shift_right_logical(ri, 6) == lax.shift_right_logical(ci, 6)
    incl = same & (ri >= ci)
    strict = same & (ri > ci)
    first_cols = ci < GDN_CHUNK
    ltri = jnp.where(incl, 1.0, 0.0).astype(BF16)
    csum = jnp.where(same, 1.0, 0.0).astype(BF16)
    tot = [jnp.where(first_cols, 1.0, 0.0).astype(BF16),
           jnp.where(first_cols, 0.0, 1.0).astype(BF16)]
    s_sc[...] = jnp.zeros(s_sc.shape, F32)

    def body(pi, carry):
        r0 = pl.multiple_of(pi * PAIR, PAIR)
        rows = pl.ds(r0, PAIR)
        sm = small_ref[rows, :]
        cs = _dot3(ltri, sm)
        ctot = _dot3(csum, sm)
        gt = gt_ref[pi]
        csr = _dot3(ltri, gt, nt=True)
        glast = [_dot3(tot[0], gt, nt=True), _dot3(tot[1], gt, nt=True)]
        for h in range(GDN_HEADS):
            cols = slice(h * GDN_DIM, (h + 1) * GDN_DIM)
            q = q_ref[rows, cols]
            k = k_ref[rows, cols]
            kf = k.astype(F32)
            beta = sm[:, SMALL_B + h:SMALL_B + h + 1]
            gcc = cs[:, SMALL_A + h:SMALL_A + h + 1]
            gcr = csr[h:h + 1, :]
            decay = jnp.exp(jnp.where(incl, gcc - gcr, NEG_INF))
            egc = jnp.exp(gcc)
            kb = kf * beta
            a = jnp.where(strict, _dot_nt(kb.astype(BF16), k) * decay, 0.0)
            rhs = jnp.concatenate([v_ref[rows, cols].astype(F32) * beta, kb * egc], axis=1)
            x = rhs - _dot(a.astype(BF16), rhs.astype(BF16))
            p = a
            for _ in range(N_DOUBLINGS):
                pb = p.astype(BF16)
                p = _dot(pb, pb)
                x = x + _dot(p.astype(BF16), x.astype(BF16))
            u = x[:, 0:GDN_DIM]
            w = x[:, GDN_DIM:2 * GDN_DIM].astype(BF16)
            qk = (_dot_nt(q, k) * decay).astype(BF16)
            qe = (q.astype(F32) * egc).astype(BF16)
            ke_t = (kf * jnp.exp(ctot[:, SMALL_A + h:SMALL_A + h + 1] - gcc)).T
            s = s_sc[h]
            vn, qs = [], []
            for c in range(2):
                half = slice(c * GDN_CHUNK, (c + 1) * GDN_CHUNK)
                wq = _dot(jnp.concatenate([w[half], qe[half]], axis=0), s.astype(BF16))
                v_new = u[half] - wq[0:GDN_CHUNK]
                vn.append(v_new)
                qs.append(wq[GDN_CHUNK:PAIR])
                keep = first_cols if c == 0 else jnp.logical_not(first_cols)
                vn_pair = jnp.concatenate([v_new, v_new], axis=0).astype(BF16)
                ke_c = jnp.where(keep, ke_t, 0.0).astype(BF16)
                s = s * jnp.exp(glast[c][h:h + 1, :]) + _dot(ke_c, vn_pair)
            s_sc[h] = s
            o = jnp.concatenate(qs, axis=0) + _dot(qk, jnp.concatenate(vn, axis=0).astype(BF16))
            o = o * lax.rsqrt(jnp.mean(o * o, axis=-1, keepdims=True) + RMS_EPS) * ng_ref[...]
            o_ref[rows, cols] = (o * _silu(z_ref[rows, cols].astype(F32))).astype(BF16)
        return carry

    lax.fori_loop(0, seq_len // PAIR, body, 0)


def _gdn(gq, gk, gv, z, small, gt, norm_g, seq_len):
    n = gq.shape[0]
    nb = n // seq_len
    seq = lambda b: (b, 0)
    return pl.pallas_call(
        _gdn_kernel,
        grid=(nb,),
        in_specs=[pl.BlockSpec((seq_len, GDN_WIDTH), seq), pl.BlockSpec((seq_len, GDN_WIDTH), seq),
                  pl.BlockSpec((seq_len, GDN_WIDTH), seq), pl.BlockSpec((seq_len, GDN_WIDTH), seq),
                  pl.BlockSpec((seq_len, LANES), seq),
                  pl.BlockSpec((seq_len // PAIR, SUBLANES, PAIR), lambda b: (b, 0, 0)),
                  pl.BlockSpec((1, GDN_DIM), lambda b: (0, 0))],
        out_specs=pl.BlockSpec((seq_len, GDN_WIDTH), seq),
        out_shape=jax.ShapeDtypeStruct((n, GDN_WIDTH), BF16),
        scratch_shapes=[pltpu.VMEM((GDN_HEADS, GDN_DIM, GDN_DIM), F32)],
        compiler_params=pltpu.CompilerParams(dimension_semantics=("arbitrary",),
                                             vmem_limit_bytes=VMEM_LIMIT),
        name="gated_delta_rule",
    )(gq, gk, gv, z, small, gt, norm_g)


FF_CHUNK = 256


def _rms(x, g):
    return x * lax.rsqrt(jnp.mean(x * x, axis=-1, keepdims=True) + RMS_EPS) * g


def _out_mlp_kernel(x_ref, on_ref, og_ref, gn_ref, wo_ref, g2_ref, wg_ref, wu_ref, wd_ref, gf_ref, out_ref):
    o_nsa = _rms(on_ref[...].astype(F32), gn_ref[...]).astype(BF16)
    mix = jnp.concatenate([o_nsa, og_ref[...]], axis=1)
    h = x_ref[...] + _dot(mix, wo_ref[...])
    hn = _rms(h, g2_ref[...]).astype(BF16)
    y = jnp.zeros_like(h)
    for c in range(D_FF // FF_CHUNK):
        cols = slice(c * FF_CHUNK, (c + 1) * FF_CHUNK)
        act = _silu(_dot(hn, wg_ref[:, cols])) * _dot(hn, wu_ref[:, cols])
        y = y + _dot(act.astype(BF16), wd_ref[cols, :])
    out_ref[...] = _rms(h + y, gf_ref[...])


def _out_mlp(x2, o_nsa, o_gdn, gn, wo, g2, wg, wu, wd, gf):
    n = x2.shape[0]
    row = lambda i: (i, 0)
    const = lambda i: (0, 0)
    resident = lambda shape: pl.BlockSpec(shape, const, pipeline_mode=pl.Buffered(1))
    return pl.pallas_call(
        _out_mlp_kernel,
        grid=(n // TM,),
        in_specs=[pl.BlockSpec((TM, D_MODEL), row), pl.BlockSpec((TM, NSA_WIDTH), row),
                  pl.BlockSpec((TM, GDN_WIDTH), row), pl.BlockSpec((1, NSA_WIDTH), const),
                  resident((D_MODEL, D_MODEL)), pl.BlockSpec((1, D_MODEL), const),
                  resident((D_MODEL, D_FF)), resident((D_MODEL, D_FF)), resident((D_FF, D_MODEL)),
                  pl.BlockSpec((1, D_MODEL), const)],
        out_specs=pl.BlockSpec((TM, D_MODEL), row),
        out_shape=jax.ShapeDtypeStruct((n, D_MODEL), F32),
        compiler_params=pltpu.CompilerParams(dimension_semantics=("arbitrary",),
                                             vmem_limit_bytes=VMEM_LIMIT),
        name="out_mlp",
    )(x2, o_nsa, o_gdn, gn, wo, g2, wg, wu, wd, gf)


def _nsa_head_perm():
    c = jnp.arange(NSA_WIDTH)
    return ((c // LANES) + 4 * ((c % LANES) // HEAD_DIM)) * HEAD_DIM + c % HEAD_DIM


def _layer(x2, pos2, seq_len, norm1_g, w_in, cmp_pos, cmp_w1, cmp_w2, nsa_norm_g, gdn_conv_w, gdn_a_log,
           gdn_dt_bias, gdn_norm_g, w_out, norm2_g, w_gate, w_up, w_down, out_g):
    w_r, wa_t, nega, dtb, negat, dtbt = _prep_in_proj_weights(w_in, gdn_a_log, gdn_dt_bias)
    qkk, vv, cmp_k, cmp_v, gq, gk, gv, z, small, gt = _in_proj(
        x2, pos2, norm1_g[None, :], w_r, wa_t, _rope_inv_freq(), gdn_conv_w.astype(F32),
        nega, dtb, negat, dtbt, seq_len)
    kvc = _compress(cmp_k, cmp_v, *_prep_compress_weights(cmp_pos, cmp_w1, cmp_w2), seq_len)
    ovt, eblk = _nsa_constants(seq_len)
    o_nsa = _nsa(qkk, vv, kvc, small, ovt, eblk, seq_len)
    o_gdn = _gdn(gq, gk, gv, z, small, gt, gdn_norm_g[None, :].astype(F32), seq_len)
    perm = _nsa_head_perm()
    wo = jnp.concatenate([w_out[:NSA_WIDTH][perm], w_out[NSA_WIDTH:]], axis=0).astype(BF16)
    return _out_mlp(x2, o_nsa, o_gdn, nsa_norm_g[perm][None, :].astype(F32), wo, norm2_g[None, :].astype(F32),
                    w_gate.astype(BF16), w_up.astype(BF16), w_down.astype(BF16), out_g[None, :].astype(F32))


def kernel(x, positions, norm1_g, w_in, cmp_pos, cmp_w1, cmp_w2, nsa_norm_g, gdn_conv_w, gdn_a_log,
           gdn_dt_bias, gdn_norm_g, w_out, norm2_g, w_gate, w_up, w_down, final_g):
    nb, seq_len, d = x.shape
    depth = w_in.shape[0]
    assert d == D_MODEL and seq_len % TM == 0 and seq_len // SEL_BLOCK == N_SEL and depth == 1
    x2 = x.reshape(nb * seq_len, d)
    pos2 = positions.reshape(nb * seq_len, 1)
    out = _layer(x2, pos2, seq_len, norm1_g[0], w_in[0], cmp_pos[0], cmp_w1[0], cmp_w2[0], nsa_norm_g[0],
                 gdn_conv_w[0], gdn_a_log[0], gdn_dt_bias[0], gdn_norm_g[0], w_out[0], norm2_g[0],
                 w_gate[0], w_up[0], w_down[0], final_g)
    return out.reshape(nb, seq_len, d)
```

```python
import functools

import jax
import jax.numpy as jnp
from jax import lax
from jax.experimental import pallas as pl
from jax.experimental.pallas import tpu as pltpu

F32 = jnp.float32
BF16 = jnp.bfloat16

LANES = 128
SUBLANES = 8

D_MODEL = 1024
N_HEADS = 8
N_GROUPS = 2
HEAD_DIM = 64
CMP_BLOCK = 32
CMP_STRIDE = 16
CMP_HIDDEN = 128
SEL_BLOCK = 64
SEL_TOP_N = 8
WINDOW = 512
ROPE_THETA = 500000.0
ROPE_DIM = 16
GDN_HEADS = 4
GDN_DIM = 128
GDN_CONV = 4
GDN_CHUNK = 64
NSA_WIDTH = 512
GDN_WIDTH = 512
D_FF = 2816
RMS_EPS = 1e-6
NEG_INF = -1e30
MASK_BIAS = -1e9
FORCE_SCORE = 1e9

OFF_KV = 512
OFF_GATE = OFF_KV + 6 * 128
OFF_GQKV = OFF_GATE + 24
OFF_Z = OFF_GQKV + 3 * GDN_WIDTH
OFF_B = OFF_Z + GDN_WIDTH
OFF_A = OFF_B + GDN_HEADS

TR_Q = 0
TR_K = 512
TR_V = 896
TR_SMALL = 1152
TR_ROWS = 1184
TM_VCMP = 0
TM_GDN = 128
TM_Z = 1664
TM_SMALL = 2176
TM_COLS = 2304
SMALL_B = 24
SMALL_A = 28
N_SMALL = 32

TM = 512
VMEM_LIMIT = 56 * 1024 * 1024


def _dot(a, b):
    return jnp.dot(a, b, preferred_element_type=F32)


def _dot_nt(a, b):
    return lax.dot_general(a, b, (((1,), (1,)), ((), ())), preferred_element_type=F32)


def _sigmoid(x):
    return 1.0 / (1.0 + jnp.exp(-x))


def _silu(x):
    return x * _sigmoid(x)


def _softplus(x):
    return jnp.maximum(x, 0.0) + jnp.log(1.0 + jnp.exp(-jnp.abs(x)))


def _split3(x):
    h1 = x.astype(BF16)
    r1 = x - h1.astype(F32)
    h2 = r1.astype(BF16)
    h3 = (r1 - h2.astype(F32)).astype(BF16)
    return h1, h2, h3


def _dot3(mat01, x, nt=False):
    f = (lambda a: _dot_nt(a, mat01)) if nt else (lambda a: _dot(mat01, a))
    h1, h2, h3 = _split3(x)
    return f(h1) + f(h2) + f(h3)


def _rope_t(blk, cosv, sinv):
    half = ROPE_DIM // 2
    parts = []
    for h in range(blk.shape[0] // HEAD_DIM):
        b = h * HEAD_DIM
        x0 = blk[b:b + half]
        x1 = blk[b + half:b + ROPE_DIM]
        parts += [x0 * cosv - x1 * sinv, x1 * cosv + x0 * sinv, blk[b + ROPE_DIM:b + HEAD_DIM]]
    return jnp.concatenate(parts, axis=0)


def _in_proj_kernel(tiles_per_seq, x_ref, pos_ref, g1_ref, wt_ref, w_ref, invf_ref, convw_ref,
                    alog_ref, dtb_ref, alogt_ref, dtbt_ref,
                    qt_ref, kc_ref, ks_ref, kw_ref, vst_ref, vwt_ref, cmpv_ref,
                    gq_ref, gk_ref, gv_ref, z_ref, small_ref, smallt_ref, gt_ref,
                    cbuf):
    tm = x_ref.shape[0]
    nlb = tm // LANES
    x = x_ref[...]
    hn = x * lax.rsqrt(jnp.mean(x * x, axis=-1, keepdims=True) + RMS_EPS) * g1_ref[...]
    hb = hn.astype(BF16)

    yt = _dot_nt(wt_ref[...], hb)
    ang = invf_ref[...] * pos_ref[...].astype(F32)
    cosv = jnp.cos(ang)
    sinv = jnp.sin(ang)
    for m in range(4):
        blk = _rope_t(yt[TR_Q + m * LANES:TR_Q + (m + 1) * LANES], cosv, sinv)
        qt_ref[m * LANES:(m + 1) * LANES, :] = (blk * (HEAD_DIM ** -0.5)).astype(BF16)
    for j, ref in enumerate((kc_ref, ks_ref, kw_ref)):
        blk = _rope_t(yt[TR_K + j * LANES:TR_K + (j + 1) * LANES], cosv, sinv)
        for c in range(nlb):
            ref[c * LANES:(c + 1) * LANES, :] = blk[:, c * LANES:(c + 1) * LANES].T.astype(ref.dtype)
    for j, ref in enumerate((vst_ref, vwt_ref)):
        blk = yt[TR_V + j * LANES:TR_V + (j + 1) * LANES].astype(BF16)
        for c in range(nlb):
            ref[c] = blk[:, c * LANES:(c + 1) * LANES]
    st = yt[TR_SMALL:TR_SMALL + N_SMALL]
    srow = lax.broadcasted_iota(jnp.int32, (N_SMALL, 1), 0)
    gdec_t = -jnp.exp(alogt_ref[...]) * _softplus(st + dtbt_ref[...])
    small_t = jnp.where(srow < SMALL_A, _sigmoid(st), gdec_t)
    smallt_ref[...] = small_t
    for c in range(nlb):
        gt_ref[c] = small_t[SMALL_B:N_SMALL, c * LANES:(c + 1) * LANES]

    cmpv_ref[...] = _dot(hb, w_ref[:, TM_VCMP:TM_VCMP + LANES])

    first = pl.program_id(0) % tiles_per_seq == 0

    @pl.when(first)
    def _():
        cbuf[0:SUBLANES, :] = jnp.zeros((SUBLANES, 3 * GDN_WIDTH), F32)

    cbuf[SUBLANES:SUBLANES + tm, :] = _dot(hb, w_ref[:, TM_GDN:TM_GDN + 3 * GDN_WIDTH])
    for cb in range(12):
        cols = slice(cb * LANES, (cb + 1) * LANES)
        y = jnp.zeros((tm, LANES), F32)
        for k in range(GDN_CONV):
            r0 = SUBLANES - (GDN_CONV - 1) + k
            y = y + cbuf[r0:r0 + tm, cols] * convw_ref[k:k + 1, cols]
        y = _silu(y)
        if cb < 8:
            y = y * lax.rsqrt(jnp.sum(y * y, axis=-1, keepdims=True) + RMS_EPS)
        if cb < 4:
            gq_ref[:, cols] = (y * (GDN_DIM ** -0.5)).astype(BF16)
        elif cb < 8:
            gk_ref[:, (cb - 4) * LANES:(cb - 3) * LANES] = y.astype(BF16)
        else:
            gv_ref[:, (cb - 8) * LANES:(cb - 7) * LANES] = y.astype(BF16)
    cbuf[0:SUBLANES, :] = cbuf[tm:tm + SUBLANES, :]

    z_ref[...] = _dot(hb, w_ref[:, TM_Z:TM_Z + GDN_WIDTH]).astype(BF16)

    sm = _dot(hb, w_ref[:, TM_SMALL:TM_SMALL + LANES])
    lane1 = lax.broadcasted_iota(jnp.int32, (1, LANES), 1)
    gdec = -jnp.exp(alog_ref[...]) * _softplus(sm + dtb_ref[...])
    small_ref[...] = jnp.where(lane1 < SMALL_A, _sigmoid(sm), gdec)


def _in_proj(x2, pos_row, g1, w_t, w_r, invf, convw, alog, dtb, alogt, dtbt, seq_len):
    n = x2.shape[0]
    row = lambda i: (i, 0)
    colb = lambda i: (0, i)
    lead = lambda i: (i, 0, 0)
    const = lambda i: (0, 0)
    nlb = TM // LANES
    out_shapes = (
        jax.ShapeDtypeStruct((NSA_WIDTH, n), BF16),
        jax.ShapeDtypeStruct((n, LANES), F32),
        jax.ShapeDtypeStruct((n, LANES), BF16),
        jax.ShapeDtypeStruct((n, LANES), BF16),
        jax.ShapeDtypeStruct((n // LANES, LANES, LANES), BF16),
        jax.ShapeDtypeStruct((n // LANES, LANES, LANES), BF16),
        jax.ShapeDtypeStruct((n, LANES), F32),
        jax.ShapeDtypeStruct((n, GDN_WIDTH), BF16),
        jax.ShapeDtypeStruct((n, GDN_WIDTH), BF16),
        jax.ShapeDtypeStruct((n, GDN_WIDTH), BF16),
        jax.ShapeDtypeStruct((n, GDN_WIDTH), BF16),
        jax.ShapeDtypeStruct((n, LANES), F32),
        jax.ShapeDtypeStruct((N_SMALL, n), F32),
        jax.ShapeDtypeStruct((n // LANES, SUBLANES, LANES), F32),
    )
    out_specs = (
        pl.BlockSpec((NSA_WIDTH, TM), colb),
        pl.BlockSpec((TM, LANES), row), pl.BlockSpec((TM, LANES), row), pl.BlockSpec((TM, LANES), row),
        pl.BlockSpec((nlb, LANES, LANES), lead), pl.BlockSpec((nlb, LANES, LANES), lead),
        pl.BlockSpec((TM, LANES), row),
        pl.BlockSpec((TM, GDN_WIDTH), row), pl.BlockSpec((TM, GDN_WIDTH), row),
        pl.BlockSpec((TM, GDN_WIDTH), row), pl.BlockSpec((TM, GDN_WIDTH), row),
        pl.BlockSpec((TM, LANES), row),
        pl.BlockSpec((N_SMALL, TM), colb),
        pl.BlockSpec((nlb, SUBLANES, LANES), lead),
    )
    in_specs = [
        pl.BlockSpec((TM, D_MODEL), row), pl.BlockSpec((1, TM), colb), pl.BlockSpec((1, D_MODEL), const),
        pl.BlockSpec((TR_ROWS, D_MODEL), const), pl.BlockSpec((D_MODEL, TM_COLS), const),
        pl.BlockSpec((ROPE_DIM // 2, 1), const), pl.BlockSpec((GDN_CONV, 3 * GDN_WIDTH), const),
        pl.BlockSpec((1, LANES), const), pl.BlockSpec((1, LANES), const),
        pl.BlockSpec((N_SMALL, 1), const), pl.BlockSpec((N_SMALL, 1), const),
    ]
    return pl.pallas_call(
        functools.partial(_in_proj_kernel, seq_len // TM),
        grid=(n // TM,), in_specs=in_specs, out_specs=out_specs, out_shape=out_shapes,
        scratch_shapes=[pltpu.VMEM((TM + 2 * SUBLANES, 3 * GDN_WIDTH), F32)],
        compiler_params=pltpu.CompilerParams(dimension_semantics=("arbitrary",),
                                             vmem_limit_bytes=VMEM_LIMIT),
        name="in_proj",
    )(x2, pos_row, g1, w_t, w_r, invf, convw, alog, dtb, alogt, dtbt)


def _prep_in_proj_weights(w_in, gdn_a_log, gdn_dt_bias):
    q = w_in[:, :NSA_WIDTH].reshape(D_MODEL, 2, 4, HEAD_DIM)
    q = jnp.transpose(q, (0, 2, 1, 3)).reshape(D_MODEL, NSA_WIDTH)
    kv = w_in[:, OFF_KV:OFF_GATE].reshape(D_MODEL, 6, LANES)
    gate = w_in[:, OFF_GATE:OFF_GQKV].reshape(D_MODEL, 2, 4, 3)
    gate = jnp.transpose(gate, (0, 3, 2, 1)).reshape(D_MODEL, 24)
    small = jnp.concatenate([gate, w_in[:, OFF_B:OFF_A], w_in[:, OFF_A:OFF_A + GDN_HEADS]], axis=1)
    w_t = jnp.concatenate([q, kv[:, 0], kv[:, 2], kv[:, 4], kv[:, 3], kv[:, 5], small], axis=1).T.astype(BF16)
    w_r = jnp.concatenate([kv[:, 1], w_in[:, OFF_GQKV:OFF_Z], w_in[:, OFF_Z:OFF_B], small,
                           jnp.zeros((D_MODEL, LANES - N_SMALL), w_in.dtype)], axis=1).astype(BF16)
    alog = jnp.zeros((LANES,), F32).at[SMALL_A:SMALL_A + GDN_HEADS].set(gdn_a_log.astype(F32))
    dtb = jnp.zeros((LANES,), F32).at[SMALL_A:SMALL_A + GDN_HEADS].set(gdn_dt_bias.astype(F32))
    return w_t, w_r, alog[None, :], dtb[None, :], alog[:N_SMALL, None], dtb[:N_SMALL, None]


def _rope_inv_freq():
    half = ROPE_DIM // 2
    return jnp.power(ROPE_THETA, -jnp.arange(half, dtype=F32) * (2.0 / ROPE_DIM))[:, None]


def _compress_kernel(xk_ref, xv_ref, pos_ref, w1_ref, w2_ref, out_ref):
    nblk = xk_ref.shape[0] // CMP_STRIDE
    acc_lo = jnp.zeros((nblk, 4 * CMP_HIDDEN), F32)
    acc_hi = jnp.zeros((nblk, 4 * CMP_HIDDEN), F32)
    for j in range(CMP_STRIDE):
        xj = jnp.concatenate([xk_ref[pl.ds(j, nblk, stride=CMP_STRIDE), :],
                              xv_ref[pl.ds(j, nblk, stride=CMP_STRIDE), :]], axis=1)
        acc_lo = acc_lo + _dot((xj + pos_ref[j:j + 1, :]).astype(BF16), w1_ref[j])
        acc_hi = acc_hi + _dot((xj + pos_ref[CMP_STRIDE + j:CMP_STRIDE + j + 1, :]).astype(BF16),
                               w1_ref[CMP_STRIDE + j])
    pre = acc_lo + pltpu.roll(acc_hi, nblk - 1, 0)
    kvc = _dot(_silu(pre).astype(BF16), w2_ref[...])
    out_ref[0:nblk, :] = kvc[:, 0:LANES]
    out_ref[nblk:2 * nblk, :] = kvc[:, LANES:2 * LANES].T


def _compress(cmp_k, cmp_v, pos_rows, w1_bd, w2_bd, seq_len):
    n = cmp_k.shape[0]
    nb = n // seq_len
    nblk = seq_len // CMP_STRIDE
    assert nblk == LANES
    return pl.pallas_call(
        _compress_kernel,
        grid=(nb,),
        in_specs=[pl.BlockSpec((seq_len, LANES), lambda b: (b, 0)),
                  pl.BlockSpec((seq_len, LANES), lambda b: (b, 0)),
                  pl.BlockSpec((CMP_BLOCK, 256), lambda b: (0, 0)),
                  pl.BlockSpec((CMP_BLOCK, 256, 4 * CMP_HIDDEN), lambda b: (0, 0, 0)),
                  pl.BlockSpec((4 * CMP_HIDDEN, 256), lambda b: (0, 0))],
        out_specs=pl.BlockSpec((2 * nblk, LANES), lambda b: (b, 0)),
        out_shape=jax.ShapeDtypeStruct((nb * 2 * nblk, LANES), F32),
        compiler_params=pltpu.CompilerParams(dimension_semantics=("arbitrary",),
                                             vmem_limit_bytes=VMEM_LIMIT),
        name="nsa_compress",
    )(cmp_k, cmp_v, pos_rows, w1_bd, w2_bd)


def _prep_compress_weights(cmp_pos, cmp_w1, cmp_w2):
    w1 = cmp_w1.reshape(2, CMP_BLOCK, HEAD_DIM, CMP_HIDDEN)
    w1_bd = jnp.zeros((CMP_BLOCK, 4, HEAD_DIM, 4, CMP_HIDDEN), F32)
    w2_bd = jnp.zeros((4, CMP_HIDDEN, 4, HEAD_DIM), F32)
    for s in range(4):
        w1_bd = w1_bd.at[:, s, :, s, :].set(w1[s // 2])
        w2_bd = w2_bd.at[s, :, s, :].set(cmp_w2[s // 2])
    w1_bd = w1_bd.reshape(CMP_BLOCK, 256, 4 * CMP_HIDDEN).astype(BF16)
    w2_bd = w2_bd.reshape(4 * CMP_HIDDEN, 256).astype(BF16)
    pos_rows = jnp.concatenate([cmp_pos[0], cmp_pos[0], cmp_pos[1], cmp_pos[1]], axis=-1).astype(F32)
    return pos_rows, w1_bd, w2_bd


TQ = 128
KC = 256
WSPAN = WINDOW + TQ
N_SEL = 32
ROWS = N_HEADS * TQ


def _tile_heads(x):
    return jnp.concatenate([x] * N_HEADS, axis=1)


def _nsa_kernel(qt_ref, ks_ref, kw_ref, vst_ref, vwt_ref, kvc_ref, gt_ref, ovt_ref, eblk_ref, o_ref,
                qaug, m_sc, l_sc, acc_sc):
    t0 = pl.program_id(1) * TQ
    tcol = t0 + lax.broadcasted_iota(jnp.int32, (1, TQ), 1)

    zhalf = jnp.zeros((HEAD_DIM, TQ), BF16)
    for m in range(4):
        blk = qt_ref[m * LANES:(m + 1) * LANES, :]
        qaug[0:LANES, (2 * m) * TQ:(2 * m + 1) * TQ] = jnp.concatenate([blk[0:HEAD_DIM], zhalf], axis=0)
        qaug[0:LANES, (2 * m + 1) * TQ:(2 * m + 2) * TQ] = jnp.concatenate([zhalf, blk[HEAD_DIM:LANES]], axis=0)
    qs = qaug[0:LANES, :]

    nblk = kvc_ref.shape[0] // 2
    kc = kvc_ref[0:nblk, :].astype(BF16)
    vct = kvc_ref[nblk:2 * nblk, :].astype(BF16)
    nrow = lax.broadcasted_iota(jnp.int32, (nblk, 1), 0)
    vbias = jnp.where(nrow * CMP_STRIDE + (CMP_BLOCK - 1) <= tcol, 0.0, NEG_INF)
    s_c = _dot(kc, qs) + _tile_heads(vbias)
    e_c = jnp.exp(s_c - jnp.max(s_c, axis=0, keepdims=True))
    p_c = e_c * (1.0 / jnp.sum(e_c, axis=0, keepdims=True))
    p_c = jnp.where(_tile_heads(tcol >= CMP_BLOCK - 1), p_c, 0.0)
    o_c = _dot(vct, p_c.astype(BF16))

    jrow = lax.broadcasted_iota(jnp.int32, (N_SEL, 1), 0)
    cur = lax.shift_right_logical(tcol, 6)
    forced = (jrow == 0) | (jrow == cur) | (jrow == cur - 1)
    causal = jrow <= cur
    for g in range(N_GROUPS):
        psum = p_c[:, g * TQ:(g + 1) * TQ]
        for m in range(1, 4):
            psum = psum + p_c[:, (2 * m + g) * TQ:(2 * m + g + 1) * TQ]
        imp_t = _dot3(ovt_ref[...], psum)
        score = jnp.where(forced, FORCE_SCORE, jnp.where(causal, imp_t[0:N_SEL, :], NEG_INF))
        cnt = jnp.zeros((N_SEL, TQ), jnp.int32)
        for jp in range(N_SEL):
            rowv = score[jp:jp + 1, :]
            beats = (rowv > score) | ((rowv == score) & (jrow > jp))
            cnt = cnt + jnp.where(beats, 1, 0)
        sel = (cnt < SEL_TOP_N) & causal
        bias = jnp.concatenate([jnp.where(sel, 0.0, MASK_BIAS),
                                jnp.zeros((LANES - N_SEL, TQ), F32)], axis=0).astype(BF16)
        for m in range(4):
            r = 2 * m + g
            qaug[LANES:2 * LANES, r * TQ:(r + 1) * TQ] = bias

    m_sc[...] = jnp.full(m_sc.shape, NEG_INF, F32)
    l_sc[...] = jnp.zeros(l_sc.shape, F32)
    acc_sc[...] = jnp.zeros(acc_sc.shape, F32)

    def sel_step(c, diagonal):
        k0 = pl.multiple_of(c * KC, KC)
        kaug = jnp.concatenate([ks_ref[pl.ds(k0, KC), :], eblk_ref[pl.ds(k0, KC), :]], axis=1)
        s = _dot(kaug, qaug[...])
        if diagonal:
            kpos = k0 + lax.broadcasted_iota(jnp.int32, (KC, 1), 0)
            s = s + _tile_heads(jnp.where(kpos <= tcol, 0.0, NEG_INF))
        m_old = m_sc[...]
        m_new = jnp.maximum(m_old, jnp.max(s, axis=0, keepdims=True))
        alpha = jnp.exp(m_old - m_new)
        p = jnp.exp(s - m_new)
        l_sc[...] = alpha * l_sc[...] + jnp.sum(p, axis=0, keepdims=True)
        m_sc[...] = m_new
        vt = jnp.concatenate([vst_ref[2 * c], vst_ref[2 * c + 1]], axis=1)
        acc_sc[...] = acc_sc[...] * alpha + _dot(vt, p.astype(BF16))

    n_full = t0 // KC

    def body(c, carry):
        sel_step(c, False)
        return carry

    lax.fori_loop(0, n_full, body, 0)
    sel_step(n_full, True)

    w0 = pl.multiple_of(jnp.maximum(t0 - WINDOW, 0), TQ)
    wb = w0 // TQ
    wdiff = tcol - (w0 + lax.broadcasted_iota(jnp.int32, (WSPAN, 1), 0))
    wbias = jnp.where((wdiff >= 0) & (wdiff < WINDOW), 0.0, NEG_INF)
    s_w = _dot(kw_ref[pl.ds(w0, WSPAN), :], qs) + _tile_heads(wbias)
    p_w = jnp.exp(s_w - jnp.max(s_w, axis=0, keepdims=True))
    l_w = jnp.sum(p_w, axis=0, keepdims=True)
    vwt = jnp.concatenate([vwt_ref[wb + j] for j in range(WSPAN // TQ)], axis=1)
    o_w = _dot(vwt, p_w.astype(BF16))

    gates = gt_ref[...]
    o_s = acc_sc[...] * (1.0 / l_sc[...])
    o_w = o_w * (1.0 / l_w)
    for m in range(4):
        halves = []
        for g in range(N_GROUPS):
            r = 2 * m + g
            cols = slice(r * TQ, (r + 1) * TQ)
            o_r = (gates[r:r + 1] * o_c[:, cols] + gates[8 + r:9 + r] * o_s[:, cols]
                   + gates[16 + r:17 + r] * o_w[:, cols])
            halves.append(o_r[g * HEAD_DIM:(g + 1) * HEAD_DIM])
        o_ref[:, m * LANES:(m + 1) * LANES] = jnp.concatenate(halves, axis=0).T.astype(BF16)


def _nsa(qt, ks, kw, vst, vwt, kvc, small_t, ovt, eblk, seq_len):
    n = ks.shape[0]
    nb = n // seq_len
    nq = seq_len // TQ
    nkb = seq_len // LANES
    seq = lambda b, i: (b, 0)
    return pl.pallas_call(
        _nsa_kernel,
        grid=(nb, nq),
        in_specs=[pl.BlockSpec((NSA_WIDTH, TQ), lambda b, i: (0, b * nq + i)),
                  pl.BlockSpec((seq_len, LANES), seq), pl.BlockSpec((seq_len, LANES), seq),
                  pl.BlockSpec((nkb, LANES, LANES), lambda b, i: (b, 0, 0)),
                  pl.BlockSpec((nkb, LANES, LANES), lambda b, i: (b, 0, 0)),
                  pl.BlockSpec((2 * LANES, LANES), seq),
                  pl.BlockSpec((N_SMALL, TQ), lambda b, i: (0, b * nq + i)),
                  pl.BlockSpec((LANES, LANES), lambda b, i: (0, 0)),
                  pl.BlockSpec((seq_len, LANES), lambda b, i: (0, 0))],
        out_specs=pl.BlockSpec((TQ, NSA_WIDTH), lambda b, i: (b * nq + i, 0)),
        out_shape=jax.ShapeDtypeStruct((n, NSA_WIDTH), BF16),
        scratch_shapes=[pltpu.VMEM((2 * LANES, ROWS), BF16),
                        pltpu.VMEM((1, ROWS), F32), pltpu.VMEM((1, ROWS), F32),
                        pltpu.VMEM((LANES, ROWS), F32)],
        compiler_params=pltpu.CompilerParams(dimension_semantics=("arbitrary", "arbitrary"),
                                             vmem_limit_bytes=VMEM_LIMIT),
        name="nsa_attention",
    )(qt, ks, kw, vst, vwt, kvc, small_t, ovt, eblk)


def _nsa_constants(seq_len):
    n_cmp = (seq_len - CMP_BLOCK) // CMP_STRIDE + 1
    s = jnp.arange(LANES)[:, None]
    nn = jnp.arange(LANES)[None, :]
    cs = nn * CMP_STRIDE
    ss = s * SEL_BLOCK
    ovt = (cs < ss + SEL_BLOCK) & (cs + CMP_BLOCK > ss) & (s < seq_len // SEL_BLOCK) & (nn < n_cmp)
    k = jnp.arange(seq_len)[:, None]
    eblk = (k // SEL_BLOCK) == jnp.arange(LANES)[None, :]
    return ovt.astype(BF16), eblk.astype(BF16)


PAIR = 2 * GDN_CHUNK
N_DOUBLINGS = 5
GT_G = 4


def _gdn_kernel(q_ref, k_ref, v_ref, z_ref, small_ref, gt_ref, ng_ref, o_ref, s_sc):
    seq_len = q_ref.shape[0]
    ri = lax.broadcasted_iota(jnp.int32, (PAIR, PAIR), 0)
    ci = lax.broadcasted_iota(jnp.int32, (PAIR, PAIR), 1)
    same = lax.shift_right_logical(ri, 6) == lax.shift_right_logical(ci, 6)
    incl = same & (ri >= ci)
    strict = same & (ri > ci)
    first_cols = ci < GDN_CHUNK
    ltri = jnp.where(incl, 1.0, 0.0).astype(BF16)
    csum = jnp.where(same, 1.0, 0.0).astype(BF16)
    tot = [jnp.where(first_cols, 1.0, 0.0).astype(BF16),
           jnp.where(first_cols, 0.0, 1.0).astype(BF16)]
    s_sc[...] = jnp.zeros(s_sc.shape, F32)

    def body(pi, carry):
        r0 = pl.multiple_of(pi * PAIR, PAIR)
        rows = pl.ds(r0, PAIR)
        sm = small_ref[rows, :]
        cs = _dot3(ltri, sm)
        ctot = _dot3(csum, sm)
        gt = gt_ref[pi]
        csr = _dot3(ltri, gt, nt=True)
        glast = [_dot3(tot[0], gt, nt=True), _dot3(tot[1], gt, nt=True)]
        for h in range(GDN_HEADS):
            cols = slice(h * GDN_DIM, (h + 1) * GDN_DIM)
            q = q_ref[rows, cols]
            k = k_ref[rows, cols]
            kf = k.astype(F32)
            beta = sm[:, SMALL_B + h:SMALL_B + h + 1]
            gcc = cs[:, SMALL_A + h:SMALL_A + h + 1]
            gcr = csr[GT_G + h:GT_G + h + 1, :]
            decay = jnp.exp(jnp.where(incl, gcc - gcr, NEG_INF))
            egc = jnp.exp(gcc)
            kb = kf * beta
            a = jnp.where(strict, _dot_nt(kb.astype(BF16), k) * decay, 0.0)
            rhs = jnp.concatenate([v_ref[rows, cols].astype(F32) * beta, kb * egc], axis=1)
            x = rhs - _dot(a.astype(BF16), rhs.astype(BF16))
            p = a
            for _ in range(N_DOUBLINGS):
                pb = p.astype(BF16)
                p = _dot(pb, pb)
                x = x + _dot(p.astype(BF16), x.astype(BF16))
            u = x[:, 0:GDN_DIM]
            w = x[:, GDN_DIM:2 * GDN_DIM].astype(BF16)
            qk = (_dot_nt(q, k) * decay).astype(BF16)
            qe = (q.astype(F32) * egc).astype(BF16)
            ke_t = (kf * jnp.exp(ctot[:, SMALL_A + h:SMALL_A + h + 1] - gcc)).T
            s = s_sc[h]
            vn, qs = [], []
            for c in range(2):
                half = slice(c * GDN_CHUNK, (c + 1) * GDN_CHUNK)
                wq = _dot(jnp.concatenate([w[half], qe[half]], axis=0), s.astype(BF16))
                v_new = u[half] - wq[0:GDN_CHUNK]
                vn.append(v_new)
                qs.append(wq[GDN_CHUNK:PAIR])
                keep = first_cols if c == 0 else jnp.logical_not(first_cols)
                vn_pair = jnp.concatenate([v_new, v_new], axis=0).astype(BF16)
                ke_c = jnp.where(keep, ke_t, 0.0).astype(BF16)
                s = s * jnp.exp(glast[c][GT_G + h:GT_G + h + 1, :]) + _dot(ke_c, vn_pair)
            s_sc[h] = s
            o = jnp.concatenate(qs, axis=0) + _dot(qk, jnp.concatenate(vn, axis=0).astype(BF16))
            o = o * lax.rsqrt(jnp.mean(o * o, axis=-1, keepdims=True) + RMS_EPS) * ng_ref[...]
            o_ref[rows, cols] = (o * _silu(z_ref[rows, cols].astype(F32))).astype(BF16)
        return carry

    lax.fori_loop(0, seq_len // PAIR, body, 0)


def _gdn(gq, gk, gv, z, small, gt, norm_g, seq_len):
    n = gq.shape[0]
    nb = n // seq_len
    seq = lambda b: (b, 0)
    return pl.pallas_call(
        _gdn_kernel,
        grid=(nb,),
        in_specs=[pl.BlockSpec((seq_len, GDN_WIDTH), seq), pl.BlockSpec((seq_len, GDN_WIDTH), seq),
                  pl.BlockSpec((seq_len, GDN_WIDTH), seq), pl.BlockSpec((seq_len, GDN_WIDTH), seq),
                  pl.BlockSpec((seq_len, LANES), seq),
                  pl.BlockSpec((seq_len // PAIR, SUBLANES, PAIR), lambda b: (b, 0, 0)),
                  pl.BlockSpec((1, GDN_DIM), lambda b: (0, 0))],
        out_specs=pl.BlockSpec((seq_len, GDN_WIDTH), seq),
        out_shape=jax.ShapeDtypeStruct((n, GDN_WIDTH), BF16),
        scratch_shapes=[pltpu.VMEM((GDN_HEADS, GDN_DIM, GDN_DIM), F32)],
        compiler_params=pltpu.CompilerParams(dimension_semantics=("arbitrary",),
                                             vmem_limit_bytes=VMEM_LIMIT),
        name="gated_delta_rule",
    )(gq, gk, gv, z, small, gt, norm_g)


FF_CHUNK = 256


def _rms(x, g):
    return x * lax.rsqrt(jnp.mean(x * x, axis=-1, keepdims=True) + RMS_EPS) * g


def _out_mlp_kernel(x_ref, on_ref, og_ref, gn_ref, wo_ref, g2_ref, wg_ref, wu_ref, wd_ref, gf_ref, out_ref):
    o_nsa = _rms(on_ref[...].astype(F32), gn_ref[...]).astype(BF16)
    mix = jnp.concatenate([o_nsa, og_ref[...]], axis=1)
    h = x_ref[...] + _dot(mix, wo_ref[...])
    hn = _rms(h, g2_ref[...]).astype(BF16)
    y = jnp.zeros_like(h)
    for c in range(D_FF // FF_CHUNK):
        cols = slice(c * FF_CHUNK, (c + 1) * FF_CHUNK)
        act = _silu(_dot(hn, wg_ref[:, cols])) * _dot(hn, wu_ref[:, cols])
        y = y + _dot(act.astype(BF16), wd_ref[cols, :])
    out_ref[...] = _rms(h + y, gf_ref[...])


def _out_mlp(x2, o_nsa, o_gdn, gn, wo, g2, wg, wu, wd, gf):
    n = x2.shape[0]
    row = lambda i: (i, 0)
    const = lambda i: (0, 0)
    resident = lambda shape: pl.BlockSpec(shape, const, pipeline_mode=pl.Buffered(1))
    return pl.pallas_call(
        _out_mlp_kernel,
        grid=(n // TM,),
        in_specs=[pl.BlockSpec((TM, D_MODEL), row), pl.BlockSpec((TM, NSA_WIDTH), row),
                  pl.BlockSpec((TM, GDN_WIDTH), row), pl.BlockSpec((1, NSA_WIDTH), const),
                  resident((D_MODEL, D_MODEL)), pl.BlockSpec((1, D_MODEL), const),
                  resident((D_MODEL, D_FF)), resident((D_MODEL, D_FF)), resident((D_FF, D_MODEL)),
                  pl.BlockSpec((1, D_MODEL), const)],
        out_specs=pl.BlockSpec((TM, D_MODEL), row),
        out_shape=jax.ShapeDtypeStruct((n, D_MODEL), F32),
        compiler_params=pltpu.CompilerParams(dimension_semantics=("arbitrary",),
                                             vmem_limit_bytes=VMEM_LIMIT),
        name="out_mlp",
    )(x2, o_nsa, o_gdn, gn, wo, g2, wg, wu, wd, gf)


def _nsa_head_perm():
    c = jnp.arange(NSA_WIDTH)
    return ((c // LANES) + 4 * ((c % LANES) // HEAD_DIM)) * HEAD_DIM + c % HEAD_DIM


def _layer(x2, pos_row, seq_len, norm1_g, w_in, cmp_pos, cmp_w1, cmp_w2, nsa_norm_g, gdn_conv_w, gdn_a_log,
           gdn_dt_bias, gdn_norm_g, w_out, norm2_g, w_gate, w_up, w_down, out_g):
    w_t, w_r, alog, dtb, alogt, dtbt = _prep_in_proj_weights(w_in, gdn_a_log, gdn_dt_bias)
    qt, cmp_k, ks, kw, vst, vwt, cmp_v, gq, gk, gv, z, small, small_t, gt = _in_proj(
        x2, pos_row, norm1_g[None, :].astype(F32), w_t, w_r, _rope_inv_freq(), gdn_conv_w.astype(F32),
        alog, dtb, alogt, dtbt, seq_len)
    kvc = _compress(cmp_k, cmp_v, *_prep_compress_weights(cmp_pos, cmp_w1, cmp_w2), seq_len)
    ovt, eblk = _nsa_constants(seq_len)
    o_nsa = _nsa(qt, ks, kw, vst, vwt, kvc, small_t, ovt, eblk, seq_len)
    o_gdn = _gdn(gq, gk, gv, z, small, gt, gdn_norm_g[None, :].astype(F32), seq_len)
    perm = _nsa_head_perm()
    wo = jnp.concatenate([w_out[:NSA_WIDTH][perm], w_out[NSA_WIDTH:]], axis=0).astype(BF16)
    return _out_mlp(x2, o_nsa, o_gdn, nsa_norm_g[perm][None, :].astype(F32), wo, norm2_g[None, :].astype(F32),
                    w_gate.astype(BF16), w_up.astype(BF16), w_down.astype(BF16), out_g[None, :].astype(F32))


def kernel(x, positions, norm1_g, w_in, cmp_pos, cmp_w1, cmp_w2, nsa_norm_g, gdn_conv_w, gdn_a_log,
           gdn_dt_bias, gdn_norm_g, w_out, norm2_g, w_gate, w_up, w_down, final_g):
    nb, seq_len, d = x.shape
    depth = w_in.shape[0]
    assert d == D_MODEL and seq_len % TM == 0 and seq_len // SEL_BLOCK == N_SEL and depth == 1
    x2 = x.reshape(nb * seq_len, d)
    pos_row = positions.reshape(1, nb * seq_len)
    out = _layer(x2, pos_row, seq_len, norm1_g[0], w_in[0], cmp_pos[0], cmp_w1[0], cmp_w2[0], nsa_norm_g[0],
                 gdn_conv_w[0], gdn_a_log[0], gdn_dt_bias[0], gdn_norm_g[0], w_out[0], norm2_g[0],
                 w_gate[0], w_up[0], w_down[0], final_g)
    return out.reshape(nb, seq_len, d)
```

```python
import functools

import jax
import jax.numpy as jnp
from jax import lax
from jax.experimental import pallas as pl
from jax.experimental.pallas import tpu as pltpu

F32 = jnp.float32
BF16 = jnp.bfloat16

LANES = 128
SUBLANES = 8

D_MODEL = 1024
N_HEADS = 8
N_GROUPS = 2
HEAD_DIM = 64
CMP_BLOCK = 32
CMP_STRIDE = 16
CMP_HIDDEN = 128
SEL_BLOCK = 64
SEL_TOP_N = 8
WINDOW = 512
ROPE_THETA = 500000.0
ROPE_DIM = 16
GDN_HEADS = 4
GDN_DIM = 128
GDN_CONV = 4
GDN_CHUNK = 64
NSA_WIDTH = 512
GDN_WIDTH = 512
D_FF = 2816
RMS_EPS = 1e-6
NEG_INF = -1e30
MASK_BIAS = -1e9
FORCE_SCORE = 1e9

OFF_KV = 512
OFF_GATE = OFF_KV + 6 * 128
OFF_GQKV = OFF_GATE + 24
OFF_Z = OFF_GQKV + 3 * GDN_WIDTH
OFF_B = OFF_Z + GDN_WIDTH
OFF_A = OFF_B + GDN_HEADS

TR_Q = 0
TR_K = 512
TR_V = 896
TR_SMALL = 1152
TR_ROWS = 1184
TM_VCMP = 0
TM_GDN = 128
TM_Z = 1664
TM_SMALL = 2176
TM_COLS = 2304
SMALL_B = 24
SMALL_A = 28
N_SMALL = 32

TM = 512
VMEM_LIMIT = 56 * 1024 * 1024


def _dot(a, b):
    return jnp.dot(a, b, preferred_element_type=F32)


def _dot_nt(a, b):
    return lax.dot_general(a, b, (((1,), (1,)), ((), ())), preferred_element_type=F32)


def _sigmoid(x):
    return 1.0 / (1.0 + jnp.exp(-x))


def _silu(x):
    return x * _sigmoid(x)


def _softplus(x):
    return jnp.maximum(x, 0.0) + jnp.log(1.0 + jnp.exp(-jnp.abs(x)))


def _split3(x):
    h1 = x.astype(BF16)
    r1 = x - h1.astype(F32)
    h2 = r1.astype(BF16)
    h3 = (r1 - h2.astype(F32)).astype(BF16)
    return h1, h2, h3


def _dot3(mat01, x, nt=False):
    f = (lambda a: _dot_nt(a, mat01)) if nt else (lambda a: _dot(mat01, a))
    h1, h2, h3 = _split3(x)
    return f(h1) + f(h2) + f(h3)


def _rope_t(blk, cosv, sinv):
    half = ROPE_DIM // 2
    parts = []
    for h in range(blk.shape[0] // HEAD_DIM):
        b = h * HEAD_DIM
        x0 = blk[b:b + half]
        x1 = blk[b + half:b + ROPE_DIM]
        parts += [x0 * cosv - x1 * sinv, x1 * cosv + x0 * sinv, blk[b + ROPE_DIM:b + HEAD_DIM]]
    return jnp.concatenate(parts, axis=0)


def _in_proj_kernel(tiles_per_seq, x_ref, pos_ref, g1_ref, wt_ref, w_ref, invf_ref, convw_ref,
                    alog_ref, dtb_ref, alogt_ref, dtbt_ref,
                    qt_ref, kc_ref, ks_ref, kw_ref, vst_ref, vwt_ref, cmpv_ref,
                    gq_ref, gk_ref, gv_ref, z_ref, small_ref, smallt_ref, gt_ref,
                    cbuf):
    tm = x_ref.shape[0]
    nlb = tm // LANES
    x = x_ref[...]
    hn = x * lax.rsqrt(jnp.mean(x * x, axis=-1, keepdims=True) + RMS_EPS) * g1_ref[...]
    hb = hn.astype(BF16)

    yt = _dot_nt(wt_ref[...], hb)
    ang = invf_ref[...] * pos_ref[...].astype(F32)
    cosv = jnp.cos(ang)
    sinv = jnp.sin(ang)
    for m in range(4):
        blk = _rope_t(yt[TR_Q + m * LANES:TR_Q + (m + 1) * LANES], cosv, sinv)
        qt_ref[m * LANES:(m + 1) * LANES, :] = (blk * (HEAD_DIM ** -0.5)).astype(BF16)
    for j, ref in enumerate((kc_ref, ks_ref, kw_ref)):
        blk = _rope_t(yt[TR_K + j * LANES:TR_K + (j + 1) * LANES], cosv, sinv)
        for c in range(nlb):
            ref[c * LANES:(c + 1) * LANES, :] = blk[:, c * LANES:(c + 1) * LANES].T.astype(ref.dtype)
    for j, ref in enumerate((vst_ref, vwt_ref)):
        blk = yt[TR_V + j * LANES:TR_V + (j + 1) * LANES].astype(BF16)
        for c in range(nlb):
            ref[c] = blk[:, c * LANES:(c + 1) * LANES]
    st = yt[TR_SMALL:TR_SMALL + N_SMALL]
    srow = lax.broadcasted_iota(jnp.int32, (N_SMALL, 1), 0)
    gdec_t = -jnp.exp(alogt_ref[...]) * _softplus(st + dtbt_ref[...])
    small_t = jnp.where(srow < SMALL_A, _sigmoid(st), gdec_t)
    smallt_ref[...] = small_t
    for c in range(nlb):
        gt_ref[c] = small_t[SMALL_B:N_SMALL, c * LANES:(c + 1) * LANES]

    cmpv_ref[...] = _dot(hb, w_ref[:, TM_VCMP:TM_VCMP + LANES])

    first = pl.program_id(0) % tiles_per_seq == 0

    @pl.when(first)
    def _():
        cbuf[0:SUBLANES, :] = jnp.zeros((SUBLANES, 3 * GDN_WIDTH), F32)

    cbuf[SUBLANES:SUBLANES + tm, :] = _dot(hb, w_ref[:, TM_GDN:TM_GDN + 3 * GDN_WIDTH])
    for cb in range(12):
        cols = slice(cb * LANES, (cb + 1) * LANES)
        y = jnp.zeros((tm, LANES), F32)
        for k in range(GDN_CONV):
            r0 = SUBLANES - (GDN_CONV - 1) + k
            y = y + cbuf[r0:r0 + tm, cols] * convw_ref[k:k + 1, cols]
        y = _silu(y)
        if cb < 8:
            y = y * lax.rsqrt(jnp.sum(y * y, axis=-1, keepdims=True) + RMS_EPS)
        if cb < 4:
            gq_ref[:, cols] = (y * (GDN_DIM ** -0.5)).astype(BF16)
        elif cb < 8:
            gk_ref[:, (cb - 4) * LANES:(cb - 3) * LANES] = y.astype(BF16)
        else:
            gv_ref[:, (cb - 8) * LANES:(cb - 7) * LANES] = y.astype(BF16)
    cbuf[0:SUBLANES, :] = cbuf[tm:tm + SUBLANES, :]

    z_ref[...] = _dot(hb, w_ref[:, TM_Z:TM_Z + GDN_WIDTH]).astype(BF16)

    sm = _dot(hb, w_ref[:, TM_SMALL:TM_SMALL + LANES])
    lane1 = lax.broadcasted_iota(jnp.int32, (1, LANES), 1)
    gdec = -jnp.exp(alog_ref[...]) * _softplus(sm + dtb_ref[...])
    small_ref[...] = jnp.where(lane1 < SMALL_A, _sigmoid(sm), gdec)


def _in_proj(x2, pos_row, g1, w_t, w_r, invf, convw, alog, dtb, alogt, dtbt, seq_len):
    n = x2.shape[0]
    row = lambda i: (i, 0)
    colb = lambda i: (0, i)
    lead = lambda i: (i, 0, 0)
    const = lambda i: (0, 0)
    nlb = TM // LANES
    out_shapes = (
        jax.ShapeDtypeStruct((NSA_WIDTH, n), BF16),
        jax.ShapeDtypeStruct((n, LANES), F32),
        jax.ShapeDtypeStruct((n, LANES), BF16),
        jax.ShapeDtypeStruct((n, LANES), BF16),
        jax.ShapeDtypeStruct((n // LANES, LANES, LANES), BF16),
        jax.ShapeDtypeStruct((n // LANES, LANES, LANES), BF16),
        jax.ShapeDtypeStruct((n, LANES), F32),
        jax.ShapeDtypeStruct((n, GDN_WIDTH), BF16),
        jax.ShapeDtypeStruct((n, GDN_WIDTH), BF16),
        jax.ShapeDtypeStruct((n, GDN_WIDTH), BF16),
        jax.ShapeDtypeStruct((n, GDN_WIDTH), BF16),
        jax.ShapeDtypeStruct((n, LANES), F32),
        jax.ShapeDtypeStruct((N_SMALL, n), F32),
        jax.ShapeDtypeStruct((n // LANES, SUBLANES, LANES), F32),
    )
    out_specs = (
        pl.BlockSpec((NSA_WIDTH, TM), colb),
        pl.BlockSpec((TM, LANES), row), pl.BlockSpec((TM, LANES), row), pl.BlockSpec((TM, LANES), row),
        pl.BlockSpec((nlb, LANES, LANES), lead), pl.BlockSpec((nlb, LANES, LANES), lead),
        pl.BlockSpec((TM, LANES), row),
        pl.BlockSpec((TM, GDN_WIDTH), row), pl.BlockSpec((TM, GDN_WIDTH), row),
        pl.BlockSpec((TM, GDN_WIDTH), row), pl.BlockSpec((TM, GDN_WIDTH), row),
        pl.BlockSpec((TM, LANES), row),
        pl.BlockSpec((N_SMALL, TM), colb),
        pl.BlockSpec((nlb, SUBLANES, LANES), lead),
    )
    in_specs = [
        pl.BlockSpec((TM, D_MODEL), row), pl.BlockSpec((1, TM), colb), pl.BlockSpec((1, D_MODEL), const),
        pl.BlockSpec((TR_ROWS, D_MODEL), const), pl.BlockSpec((D_MODEL, TM_COLS), const),
        pl.BlockSpec((ROPE_DIM // 2, 1), const), pl.BlockSpec((GDN_CONV, 3 * GDN_WIDTH), const),
        pl.BlockSpec((1, LANES), const), pl.BlockSpec((1, LANES), const),
        pl.BlockSpec((N_SMALL, 1), const), pl.BlockSpec((N_SMALL, 1), const),
    ]
    return pl.pallas_call(
        functools.partial(_in_proj_kernel, seq_len // TM),
        grid=(n // TM,), in_specs=in_specs, out_specs=out_specs, out_shape=out_shapes,
        scratch_shapes=[pltpu.VMEM((TM + 2 * SUBLANES, 3 * GDN_WIDTH), F32)],
        compiler_params=pltpu.CompilerParams(dimension_semantics=("arbitrary",),
                                             vmem_limit_bytes=VMEM_LIMIT),
        name="in_proj",
    )(x2, pos_row, g1, w_t, w_r, invf, convw, alog, dtb, alogt, dtbt)


def _prep_in_proj_weights(w_in, gdn_a_log, gdn_dt_bias):
    q = w_in[:, :NSA_WIDTH].reshape(D_MODEL, 2, 4, HEAD_DIM)
    q = jnp.transpose(q, (0, 2, 1, 3)).reshape(D_MODEL, NSA_WIDTH)
    kv = w_in[:, OFF_KV:OFF_GATE].reshape(D_MODEL, 6, LANES)
    gate = w_in[:, OFF_GATE:OFF_GQKV].reshape(D_MODEL, 2, 4, 3)
    gate = jnp.transpose(gate, (0, 3, 2, 1)).reshape(D_MODEL, 24)
    small = jnp.concatenate([gate, w_in[:, OFF_B:OFF_A], w_in[:, OFF_A:OFF_A + GDN_HEADS]], axis=1)
    w_t = jnp.concatenate([q, kv[:, 0], kv[:, 2], kv[:, 4], kv[:, 3], kv[:, 5], small], axis=1).T.astype(BF16)
    w_r = jnp.concatenate([kv[:, 1], w_in[:, OFF_GQKV:OFF_Z], w_in[:, OFF_Z:OFF_B], small,
                           jnp.zeros((D_MODEL, LANES - N_SMALL), w_in.dtype)], axis=1).astype(BF16)
    alog = jnp.zeros((LANES,), F32).at[SMALL_A:SMALL_A + GDN_HEADS].set(gdn_a_log.astype(F32))
    dtb = jnp.zeros((LANES,), F32).at[SMALL_A:SMALL_A + GDN_HEADS].set(gdn_dt_bias.astype(F32))
    return w_t, w_r, alog[None, :], dtb[None, :], alog[:N_SMALL, None], dtb[:N_SMALL, None]


def _rope_inv_freq():
    half = ROPE_DIM // 2
    return jnp.power(ROPE_THETA, -jnp.arange(half, dtype=F32) * (2.0 / ROPE_DIM))[:, None]


def _compress_kernel(xk_ref, xv_ref, pos_ref, w1_ref, w2_ref, out_ref):
    nblk = xk_ref.shape[0] // CMP_STRIDE
    acc_lo = jnp.zeros((nblk, 4 * CMP_HIDDEN), F32)
    acc_hi = jnp.zeros((nblk, 4 * CMP_HIDDEN), F32)
    for j in range(CMP_STRIDE):
        xj = jnp.concatenate([xk_ref[pl.ds(j, nblk, stride=CMP_STRIDE), :],
                              xv_ref[pl.ds(j, nblk, stride=CMP_STRIDE), :]], axis=1)
        acc_lo = acc_lo + _dot((xj + pos_ref[j:j + 1, :]).astype(BF16), w1_ref[j])
        acc_hi = acc_hi + _dot((xj + pos_ref[CMP_STRIDE + j:CMP_STRIDE + j + 1, :]).astype(BF16),
                               w1_ref[CMP_STRIDE + j])
    pre = acc_lo + pltpu.roll(acc_hi, nblk - 1, 0)
    kvc = _dot(_silu(pre).astype(BF16), w2_ref[...])
    out_ref[0:nblk, :] = kvc[:, 0:LANES]
    out_ref[nblk:2 * nblk, :] = kvc[:, LANES:2 * LANES].T


def _compress(cmp_k, cmp_v, pos_rows, w1_bd, w2_bd, seq_len):
    n = cmp_k.shape[0]
    nb = n // seq_len
    nblk = seq_len // CMP_STRIDE
    assert nblk == LANES
    return pl.pallas_call(
        _compress_kernel,
        grid=(nb,),
        in_specs=[pl.BlockSpec((seq_len, LANES), lambda b: (b, 0)),
                  pl.BlockSpec((seq_len, LANES), lambda b: (b, 0)),
                  pl.BlockSpec((CMP_BLOCK, 256), lambda b: (0, 0)),
                  pl.BlockSpec((CMP_BLOCK, 256, 4 * CMP_HIDDEN), lambda b: (0, 0, 0)),
                  pl.BlockSpec((4 * CMP_HIDDEN, 256), lambda b: (0, 0))],
        out_specs=pl.BlockSpec((2 * nblk, LANES), lambda b: (b, 0)),
        out_shape=jax.ShapeDtypeStruct((nb * 2 * nblk, LANES), F32),
        compiler_params=pltpu.CompilerParams(dimension_semantics=("arbitrary",),
                                             vmem_limit_bytes=VMEM_LIMIT),
        name="nsa_compress",
    )(cmp_k, cmp_v, pos_rows, w1_bd, w2_bd)


def _prep_compress_weights(cmp_pos, cmp_w1, cmp_w2):
    w1 = cmp_w1.reshape(2, CMP_BLOCK, HEAD_DIM, CMP_HIDDEN)
    w1_bd = jnp.zeros((CMP_BLOCK, 4, HEAD_DIM, 4, CMP_HIDDEN), F32)
    w2_bd = jnp.zeros((4, CMP_HIDDEN, 4, HEAD_DIM), F32)
    for s in range(4):
        w1_bd = w1_bd.at[:, s, :, s, :].set(w1[s // 2])
        w2_bd = w2_bd.at[s, :, s, :].set(cmp_w2[s // 2])
    w1_bd = w1_bd.reshape(CMP_BLOCK, 256, 4 * CMP_HIDDEN).astype(BF16)
    w2_bd = w2_bd.reshape(4 * CMP_HIDDEN, 256).astype(BF16)
    pos_rows = jnp.concatenate([cmp_pos[0], cmp_pos[0], cmp_pos[1], cmp_pos[1]], axis=-1).astype(F32)
    return pos_rows, w1_bd, w2_bd


TQ = 128
KC = 256
WSPAN = WINDOW + TQ
N_SEL = 32
ROWS = N_HEADS * TQ


def _tile_heads(x):
    return jnp.concatenate([x] * N_HEADS, axis=1)


def _nsa_kernel(qt_ref, ks_ref, kw_ref, vst_ref, vwt_ref, kvc_ref, gt_ref, ovt_ref, eblk_ref, o_ref,
                qaug, m_sc, l_sc, acc_sc):
    t0 = pl.program_id(1) * TQ
    tcol = t0 + lax.broadcasted_iota(jnp.int32, (1, TQ), 1)

    zhalf = jnp.zeros((HEAD_DIM, TQ), BF16)
    for m in range(4):
        blk = qt_ref[m * LANES:(m + 1) * LANES, :]
        qaug[0:LANES, (2 * m) * TQ:(2 * m + 1) * TQ] = jnp.concatenate([blk[0:HEAD_DIM], zhalf], axis=0)
        qaug[0:LANES, (2 * m + 1) * TQ:(2 * m + 2) * TQ] = jnp.concatenate([zhalf, blk[HEAD_DIM:LANES]], axis=0)
    qs = qaug[0:LANES, :]

    nblk = kvc_ref.shape[0] // 2
    kc = kvc_ref[0:nblk, :].astype(BF16)
    vct = kvc_ref[nblk:2 * nblk, :].astype(BF16)
    nrow = lax.broadcasted_iota(jnp.int32, (nblk, 1), 0)
    vbias = jnp.where(nrow * CMP_STRIDE + (CMP_BLOCK - 1) <= tcol, 0.0, NEG_INF)
    s_c = _dot(kc, qs) + _tile_heads(vbias)
    e_c = jnp.exp(s_c - jnp.max(s_c, axis=0, keepdims=True))
    p_c = e_c * (1.0 / jnp.sum(e_c, axis=0, keepdims=True))
    p_c = jnp.where(_tile_heads(tcol >= CMP_BLOCK - 1), p_c, 0.0)
    o_c = _dot(vct, p_c.astype(BF16))

    jrow = lax.broadcasted_iota(jnp.int32, (N_SEL, 1), 0)
    cur = lax.shift_right_logical(tcol, 6)
    forced = (jrow == 0) | (jrow == cur) | (jrow == cur - 1)
    causal = jrow <= cur
    for g in range(N_GROUPS):
        psum = p_c[:, g * TQ:(g + 1) * TQ]
        for m in range(1, 4):
            psum = psum + p_c[:, (2 * m + g) * TQ:(2 * m + g + 1) * TQ]
        imp_t = _dot3(ovt_ref[...], psum)
        score = jnp.where(forced, FORCE_SCORE, jnp.where(causal, imp_t[0:N_SEL, :], NEG_INF))
        cnt = jnp.zeros((N_SEL, TQ), jnp.int32)
        for jp in range(N_SEL):
            rowv = score[jp:jp + 1, :]
            beats = (rowv > score) | ((rowv == score) & (jrow > jp))
            cnt = cnt + jnp.where(beats, 1, 0)
        sel = (cnt < SEL_TOP_N) & causal
        bias = jnp.concatenate([jnp.where(sel, 0.0, MASK_BIAS),
                                jnp.zeros((LANES - N_SEL, TQ), F32)], axis=0).astype(BF16)
        for m in range(4):
            r = 2 * m + g
            qaug[LANES:2 * LANES, r * TQ:(r + 1) * TQ] = bias

    m_sc[...] = jnp.full(m_sc.shape, NEG_INF, F32)
    l_sc[...] = jnp.zeros(l_sc.shape, F32)
    acc_sc[...] = jnp.zeros(acc_sc.shape, F32)

    def sel_step(c, diagonal):
        k0 = pl.multiple_of(c * KC, KC)
        kaug = jnp.concatenate([ks_ref[pl.ds(k0, KC), :], eblk_ref[pl.ds(k0, KC), :]], axis=1)
        s = _dot(kaug, qaug[...])
        if diagonal:
            kpos = k0 + lax.broadcasted_iota(jnp.int32, (KC, 1), 0)
            s = s + _tile_heads(jnp.where(kpos <= tcol, 0.0, NEG_INF))
        m_old = m_sc[...]
        m_new = jnp.maximum(m_old, jnp.max(s, axis=0, keepdims=True))
        alpha = jnp.exp(m_old - m_new)
        p = jnp.exp(s - m_new)
        l_sc[...] = alpha * l_sc[...] + jnp.sum(p, axis=0, keepdims=True)
        m_sc[...] = m_new
        vt = jnp.concatenate([vst_ref[2 * c], vst_ref[2 * c + 1]], axis=1)
        acc_sc[...] = acc_sc[...] * alpha + _dot(vt, p.astype(BF16))

    n_full = t0 // KC

    def body(c, carry):
        sel_step(c, False)
        return carry

    lax.fori_loop(0, n_full, body, 0)
    sel_step(n_full, True)

    w0 = pl.multiple_of(jnp.maximum(t0 - WINDOW, 0), TQ)
    wb = w0 // TQ
    wdiff = tcol - (w0 + lax.broadcasted_iota(jnp.int32, (WSPAN, 1), 0))
    wbias = jnp.where((wdiff >= 0) & (wdiff < WINDOW), 0.0, NEG_INF)
    s_w = _dot(kw_ref[pl.ds(w0, WSPAN), :], qs) + _tile_heads(wbias)
    p_w = jnp.exp(s_w - jnp.max(s_w, axis=0, keepdims=True))
    l_w = jnp.sum(p_w, axis=0, keepdims=True)
    vwt = jnp.concatenate([vwt_ref[wb + j] for j in range(WSPAN // TQ)], axis=1)
    o_w = _dot(vwt, p_w.astype(BF16))

    gates = gt_ref[...]
    o_s = acc_sc[...] * (1.0 / l_sc[...])
    o_w = o_w * (1.0 / l_w)
    for m in range(4):
        halves = []
        for g in range(N_GROUPS):
            r = 2 * m + g
            cols = slice(r * TQ, (r + 1) * TQ)
            o_r = (gates[r:r + 1] * o_c[:, cols] + gates[8 + r:9 + r] * o_s[:, cols]
                   + gates[16 + r:17 + r] * o_w[:, cols])
            halves.append(o_r[g * HEAD_DIM:(g + 1) * HEAD_DIM])
        o_ref[:, m * LANES:(m + 1) * LANES] = jnp.concatenate(halves, axis=0).T.astype(BF16)


def _nsa(qt, ks, kw, vst, vwt, kvc, small_t, ovt, eblk, seq_len):
    n = ks.shape[0]
    nb = n // seq_len
    nq = seq_len // TQ
    nkb = seq_len // LANES
    seq = lambda b, i: (b, 0)
    return pl.pallas_call(
        _nsa_kernel,
        grid=(nb, nq),
        in_specs=[pl.BlockSpec((NSA_WIDTH, TQ), lambda b, i: (0, b * nq + i)),
                  pl.BlockSpec((seq_len, LANES), seq), pl.BlockSpec((seq_len, LANES), seq),
                  pl.BlockSpec((nkb, LANES, LANES), lambda b, i: (b, 0, 0)),
                  pl.BlockSpec((nkb, LANES, LANES), lambda b, i: (b, 0, 0)),
                  pl.BlockSpec((2 * LANES, LANES), seq),
                  pl.BlockSpec((N_SMALL, TQ), lambda b, i: (0, b * nq + i)),
                  pl.BlockSpec((LANES, LANES), lambda b, i: (0, 0)),
                  pl.BlockSpec((seq_len, LANES), lambda b, i: (0, 0))],
        out_specs=pl.BlockSpec((TQ, NSA_WIDTH), lambda b, i: (b * nq + i, 0)),
        out_shape=jax.ShapeDtypeStruct((n, NSA_WIDTH), BF16),
        scratch_shapes=[pltpu.VMEM((2 * LANES, ROWS), BF16),
                        pltpu.VMEM((1, ROWS), F32), pltpu.VMEM((1, ROWS), F32),
                        pltpu.VMEM((LANES, ROWS), F32)],
        compiler_params=pltpu.CompilerParams(dimension_semantics=("arbitrary", "arbitrary"),
                                             vmem_limit_bytes=VMEM_LIMIT),
        name="nsa_attention",
    )(qt, ks, kw, vst, vwt, kvc, small_t, ovt, eblk)


def _nsa_constants(seq_len):
    n_cmp = (seq_len - CMP_BLOCK) // CMP_STRIDE + 1
    s = jnp.arange(LANES)[:, None]
    nn = jnp.arange(LANES)[None, :]
    cs = nn * CMP_STRIDE
    ss = s * SEL_BLOCK
    ovt = (cs < ss + SEL_BLOCK) & (cs + CMP_BLOCK > ss) & (s < seq_len // SEL_BLOCK) & (nn < n_cmp)
    k = jnp.arange(seq_len)[:, None]
    eblk = (k // SEL_BLOCK) == jnp.arange(LANES)[None, :]
    return ovt.astype(BF16), eblk.astype(BF16)


PAIR = 2 * GDN_CHUNK
N_DOUBLINGS = 5
GT_G = 4


def _gdn_kernel(q_ref, k_ref, v_ref, z_ref, small_ref, gt_ref, ng_ref, o_ref, s_sc):
    seq_len = q_ref.shape[0]
    ri = lax.broadcasted_iota(jnp.int32, (PAIR, PAIR), 0)
    ci = lax.broadcasted_iota(jnp.int32, (PAIR, PAIR), 1)
    same = lax.shift_right_logical(ri, 6) == lax.shift_right_logical(ci, 6)
    incl = same & (ri >= ci)
    strict = same & (ri > ci)
    first_cols = ci < GDN_CHUNK
    ltri = jnp.where(incl, 1.0, 0.0).astype(BF16)
    csum = jnp.where(same, 1.0, 0.0).astype(BF16)
    tot = [jnp.where(first_cols, 1.0, 0.0).astype(BF16),
           jnp.where(first_cols, 0.0, 1.0).astype(BF16)]
    s_sc[...] = jnp.zeros(s_sc.shape, F32)

    def body(pi, carry):
        r0 = pl.multiple_of(pi * PAIR, PAIR)
        rows = pl.ds(r0, PAIR)
        sm = small_ref[rows, :]
        cs = _dot3(ltri, sm)
        ctot = _dot3(csum, sm)
        gt = gt_ref[pi]
        csr = _dot3(ltri, gt, nt=True)
        glast = [_dot3(tot[0], gt, nt=True), _dot3(tot[1], gt, nt=True)]
        hs = range(GDN_HEADS)
        cols = [slice(h * GDN_DIM, (h + 1) * GDN_DIM) for h in hs]
        q = [q_ref[rows, c] for c in cols]
        k = [k_ref[rows, c] for c in cols]
        kf = [x.astype(F32) for x in k]
        beta = [sm[:, SMALL_B + h:SMALL_B + h + 1] for h in hs]
        gcc = [cs[:, SMALL_A + h:SMALL_A + h + 1] for h in hs]
        gcr = [csr[GT_G + h:GT_G + h + 1, :] for h in hs]
        decay = [jnp.exp(jnp.where(incl, gcc[h] - gcr[h], NEG_INF)) for h in hs]
        egc = [jnp.exp(g) for g in gcc]
        kb = [kf[h] * beta[h] for h in hs]
        kk = [_dot_nt(kb[h].astype(BF16), k[h]) for h in hs]
        p = [jnp.where(strict, -(kk[h] * decay[h]), 0.0) for h in hs]
        x = [jnp.concatenate([v_ref[rows, cols[h]].astype(F32) * beta[h], kb[h] * egc[h]], axis=1) for h in hs]
        for i in range(N_DOUBLINGS + 1):
            pb = [y.astype(BF16) for y in p]
            x = [x[h] + _dot(pb[h], x[h].astype(BF16)) for h in hs]
            if i < N_DOUBLINGS:
                p = [_dot(y, y) for y in pb]
        uw = [y.astype(BF16) for y in x]
        qk = [(_dot_nt(q[h], k[h]) * decay[h]).astype(BF16) for h in hs]
        ke_t = [(kf[h] * jnp.exp(ctot[:, SMALL_A + h:SMALL_A + h + 1] - gcc[h])).T for h in hs]
        kw = [[_dot(jnp.where(first_cols, y, 0.0).astype(BF16), uw[h]) for h, y in enumerate(ke_t)],
              [_dot(jnp.where(first_cols, 0.0, y).astype(BF16), uw[h]) for h, y in enumerate(ke_t)]]
        qw = [_dot(qk[h], uw[h]) for h in hs]
        qm = [(q[h].astype(F32) * egc[h] - qw[h][:, GDN_DIM:2 * GDN_DIM]).astype(BF16) for h in hs]
        s = [s_sc[h] for h in hs]
        os = []
        for c in range(2):
            half = slice(c * GDN_CHUNK, (c + 1) * GDN_CHUNK)
            sb = [y.astype(BF16) for y in s]
            os.append([_dot(qm[h][half], sb[h]) for h in hs])
            ks = [_dot(kw[c][h][:, GDN_DIM:2 * GDN_DIM].astype(BF16), sb[h]) for h in hs]
            s = [s[h] * jnp.exp(glast[c][GT_G + h:GT_G + h + 1, :]) - ks[h] + kw[c][h][:, 0:GDN_DIM] for h in hs]
        for h in hs:
            s_sc[h] = s[h]
            o = jnp.concatenate([os[0][h], os[1][h]], axis=0) + qw[h][:, 0:GDN_DIM]
            o = o * lax.rsqrt(jnp.mean(o * o, axis=-1, keepdims=True) + RMS_EPS) * ng_ref[...]
            o_ref[rows, cols[h]] = (o * _silu(z_ref[rows, cols[h]].astype(F32))).astype(BF16)
        return carry

    lax.fori_loop(0, seq_len // PAIR, body, 0)


def _gdn(gq, gk, gv, z, small, gt, norm_g, seq_len):
    n = gq.shape[0]
    nb = n // seq_len
    seq = lambda b: (b, 0)
    return pl.pallas_call(
        _gdn_kernel,
        grid=(nb,),
        in_specs=[pl.BlockSpec((seq_len, GDN_WIDTH), seq), pl.BlockSpec((seq_len, GDN_WIDTH), seq),
                  pl.BlockSpec((seq_len, GDN_WIDTH), seq), pl.BlockSpec((seq_len, GDN_WIDTH), seq),
                  pl.BlockSpec((seq_len, LANES), seq),
                  pl.BlockSpec((seq_len // PAIR, SUBLANES, PAIR), lambda b: (b, 0, 0)),
                  pl.BlockSpec((1, GDN_DIM), lambda b: (0, 0))],
        out_specs=pl.BlockSpec((seq_len, GDN_WIDTH), seq),
        out_shape=jax.ShapeDtypeStruct((n, GDN_WIDTH), BF16),
        scratch_shapes=[pltpu.VMEM((GDN_HEADS, GDN_DIM, GDN_DIM), F32)],
        compiler_params=pltpu.CompilerParams(dimension_semantics=("arbitrary",),
                                             vmem_limit_bytes=VMEM_LIMIT),
        name="gated_delta_rule",
    )(gq, gk, gv, z, small, gt, norm_g)


FF_CHUNK = 256


def _rms(x, g):
    return x * lax.rsqrt(jnp.mean(x * x, axis=-1, keepdims=True) + RMS_EPS) * g


def _out_mlp_kernel(x_ref, on_ref, og_ref, gn_ref, wo_ref, g2_ref, wg_ref, wu_ref, wd_ref, gf_ref, out_ref):
    o_nsa = _rms(on_ref[...].astype(F32), gn_ref[...]).astype(BF16)
    mix = jnp.concatenate([o_nsa, og_ref[...]], axis=1)
    h = x_ref[...] + _dot(mix, wo_ref[...])
    hn = _rms(h, g2_ref[...]).astype(BF16)
    y = jnp.zeros_like(h)
    for c in range(D_FF // FF_CHUNK):
        cols = slice(c * FF_CHUNK, (c + 1) * FF_CHUNK)
        act = _silu(_dot(hn, wg_ref[:, cols])) * _dot(hn, wu_ref[:, cols])
        y = y + _dot(act.astype(BF16), wd_ref[cols, :])
    out_ref[...] = _rms(h + y, gf_ref[...])


def _out_mlp(x2, o_nsa, o_gdn, gn, wo, g2, wg, wu, wd, gf):
    n = x2.shape[0]
    row = lambda i: (i, 0)
    const = lambda i: (0, 0)
    resident = lambda shape: pl.BlockSpec(shape, const, pipeline_mode=pl.Buffered(1))
    return pl.pallas_call(
        _out_mlp_kernel,
        grid=(n // TM,),
        in_specs=[pl.BlockSpec((TM, D_MODEL), row), pl.BlockSpec((TM, NSA_WIDTH), row),
                  pl.BlockSpec((TM, GDN_WIDTH), row), pl.BlockSpec((1, NSA_WIDTH), const),
                  resident((D_MODEL, D_MODEL)), pl.BlockSpec((1, D_MODEL), const),
                  resident((D_MODEL, D_FF)), resident((D_MODEL, D_FF)), resident((D_FF, D_MODEL)),
                  pl.BlockSpec((1, D_MODEL), const)],
        out_specs=pl.BlockSpec((TM, D_MODEL), row),
        out_shape=jax.ShapeDtypeStruct((n, D_MODEL), F32),
        compiler_params=pltpu.CompilerParams(dimension_semantics=("arbitrary",),
                                             vmem_limit_bytes=VMEM_LIMIT),
        name="out_mlp",
    )(x2, o_nsa, o_gdn, gn, wo, g2, wg, wu, wd, gf)


def _nsa_head_perm():
    c = jnp.arange(NSA_WIDTH)
    return ((c // LANES) + 4 * ((c % LANES) // HEAD_DIM)) * HEAD_DIM + c % HEAD_DIM


def _layer(x2, pos_row, seq_len, norm1_g, w_in, cmp_pos, cmp_w1, cmp_w2, nsa_norm_g, gdn_conv_w, gdn_a_log,
           gdn_dt_bias, gdn_norm_g, w_out, norm2_g, w_gate, w_up, w_down, out_g):
    w_t, w_r, alog, dtb, alogt, dtbt = _prep_in_proj_weights(w_in, gdn_a_log, gdn_dt_bias)
    qt, cmp_k, ks, kw, vst, vwt, cmp_v, gq, gk, gv, z, small, small_t, gt = _in_proj(
        x2, pos_row, norm1_g[None, :].astype(F32), w_t, w_r, _rope_inv_freq(), gdn_conv_w.astype(F32),
        alog, dtb, alogt, dtbt, seq_len)
    kvc = _compress(cmp_k, cmp_v, *_prep_compress_weights(cmp_pos, cmp_w1, cmp_w2), seq_len)
    ovt, eblk = _nsa_constants(seq_len)
    o_nsa = _nsa(qt, ks, kw, vst, vwt, kvc, small_t, ovt, eblk, seq_len)
    o_gdn = _gdn(gq, gk, gv, z, small, gt, gdn_norm_g[None, :].astype(F32), seq_len)
    perm = _nsa_head_perm()
    wo = jnp.concatenate([w_out[:NSA_WIDTH][perm], w_out[NSA_WIDTH:]], axis=0).astype(BF16)
    return _out_mlp(x2, o_nsa, o_gdn, nsa_norm_g[perm][None, :].astype(F32), wo, norm2_g[None, :].astype(F32),
                    w_gate.astype(BF16), w_up.astype(BF16), w_down.astype(BF16), out_g[None, :].astype(F32))


def kernel(x, positions, norm1_g, w_in, cmp_pos, cmp_w1, cmp_w2, nsa_norm_g, gdn_conv_w, gdn_a_log,
           gdn_dt_bias, gdn_norm_g, w_out, norm2_g, w_gate, w_up, w_down, final_g):
    nb, seq_len, d = x.shape
    depth = w_in.shape[0]
    assert d == D_MODEL and seq_len % TM == 0 and seq_len // SEL_BLOCK == N_SEL and depth == 1
    x2 = x.reshape(nb * seq_len, d)
    pos_row = positions.reshape(1, nb * seq_len)
    out = _layer(x2, pos_row, seq_len, norm1_g[0], w_in[0], cmp_pos[0], cmp_w1[0], cmp_w2[0], nsa_norm_g[0],
                 gdn_conv_w[0], gdn_a_log[0], gdn_dt_bias[0], gdn_norm_g[0], w_out[0], norm2_g[0],
                 w_gate[0], w_up[0], w_down[0], final_g)
    return out.reshape(nb, seq_len, d)
```

```python
import functools
import math

import jax
import jax.numpy as jnp
from jax import lax
from jax.experimental import pallas as pl
from jax.experimental.pallas import tpu as pltpu

F32 = jnp.float32
BF16 = jnp.bfloat16

LANES = 128
SUBLANES = 8

D_MODEL = 1024
N_HEADS = 8
N_GROUPS = 2
HEAD_DIM = 64
CMP_BLOCK = 32
CMP_STRIDE = 16
CMP_HIDDEN = 128
SEL_BLOCK = 64
SEL_TOP_N = 8
WINDOW = 512
ROPE_THETA = 500000.0
ROPE_DIM = 16
GDN_HEADS = 4
GDN_DIM = 128
GDN_CONV = 4
GDN_CHUNK = 64
NSA_WIDTH = 512
GDN_WIDTH = 512
D_FF = 2816
RMS_EPS = 1e-6
NEG_INF = -1e30
MASK_BIAS = -1e9
FORCE_SCORE = 1e9

OFF_KV = 512
OFF_GATE = OFF_KV + 6 * 128
OFF_GQKV = OFF_GATE + 24
OFF_Z = OFF_GQKV + 3 * GDN_WIDTH
OFF_B = OFF_Z + GDN_WIDTH
OFF_A = OFF_B + GDN_HEADS

TR_Q = 0
TR_K = 512
TR_V = 896
TR_SMALL = 1152
TR_ROWS = 1184
TM_VCMP = 0
TM_GDN = 128
TM_Z = 1664
TM_SMALL = 2176
TM_COLS = 2304
SMALL_B = 24
SMALL_A = 28
N_SMALL = 32

Q_SCALE = HEAD_DIM ** -0.5 * math.log2(math.e)
TM = 512
VMEM_LIMIT = 56 * 1024 * 1024


def _dot(a, b):
    return jnp.dot(a, b, preferred_element_type=F32)


def _dot_nt(a, b):
    return lax.dot_general(a, b, (((1,), (1,)), ((), ())), preferred_element_type=F32)


def _sigmoid(x):
    return 1.0 / (1.0 + jnp.exp(-x))


def _silu(x):
    return x * _sigmoid(x)


def _softplus(x):
    return jnp.maximum(x, 0.0) + jnp.log(1.0 + jnp.exp(-jnp.abs(x)))


def _split3(x):
    h1 = x.astype(BF16)
    r1 = x - h1.astype(F32)
    h2 = r1.astype(BF16)
    h3 = (r1 - h2.astype(F32)).astype(BF16)
    return h1, h2, h3


def _dot3(mat01, x, nt=False):
    f = (lambda a: _dot_nt(a, mat01)) if nt else (lambda a: _dot(mat01, a))
    h1, h2, h3 = _split3(x)
    return f(h1) + f(h2) + f(h3)


def _rope_t(blk, cosv, sinv):
    half = ROPE_DIM // 2
    parts = []
    for h in range(blk.shape[0] // HEAD_DIM):
        b = h * HEAD_DIM
        x0 = blk[b:b + half]
        x1 = blk[b + half:b + ROPE_DIM]
        parts += [x0 * cosv - x1 * sinv, x1 * cosv + x0 * sinv, blk[b + ROPE_DIM:b + HEAD_DIM]]
    return jnp.concatenate(parts, axis=0)


def _in_proj_kernel(tiles_per_seq, x_ref, pos_ref, g1_ref, wt_ref, w_ref, invf_ref, convw_ref,
                    alog_ref, dtb_ref, alogt_ref, dtbt_ref,
                    qt_ref, kc_ref, ks_ref, kw_ref, vst_ref, vwt_ref, cmpv_ref,
                    gq_ref, gk_ref, gv_ref, z_ref, small_ref, smallt_ref, gt_ref,
                    cbuf):
    tm = x_ref.shape[0]
    nlb = tm // LANES
    x = x_ref[...]
    hn = x * lax.rsqrt(jnp.mean(x * x, axis=-1, keepdims=True) + RMS_EPS) * g1_ref[...]
    hb = hn.astype(BF16)

    yt = _dot_nt(wt_ref[...], hb)
    ang = invf_ref[...] * pos_ref[...].astype(F32)
    cosv = jnp.cos(ang)
    sinv = jnp.sin(ang)
    for m in range(4):
        blk = _rope_t(yt[TR_Q + m * LANES:TR_Q + (m + 1) * LANES], cosv, sinv)
        qt_ref[m * LANES:(m + 1) * LANES, :] = (blk * Q_SCALE).astype(BF16)
    for j, ref in enumerate((kc_ref, ks_ref, kw_ref)):
        blk = _rope_t(yt[TR_K + j * LANES:TR_K + (j + 1) * LANES], cosv, sinv)
        for c in range(nlb):
            ref[c * LANES:(c + 1) * LANES, :] = blk[:, c * LANES:(c + 1) * LANES].T.astype(ref.dtype)
    for j, ref in enumerate((vst_ref, vwt_ref)):
        blk = yt[TR_V + j * LANES:TR_V + (j + 1) * LANES].astype(BF16)
        for c in range(nlb):
            ref[c] = blk[:, c * LANES:(c + 1) * LANES]
    st = yt[TR_SMALL:TR_SMALL + N_SMALL]
    srow = lax.broadcasted_iota(jnp.int32, (N_SMALL, 1), 0)
    gdec_t = -jnp.exp(alogt_ref[...]) * _softplus(st + dtbt_ref[...])
    small_t = jnp.where(srow < SMALL_A, _sigmoid(st), gdec_t)
    smallt_ref[...] = small_t
    for c in range(nlb):
        gt_ref[c] = small_t[SMALL_B:N_SMALL, c * LANES:(c + 1) * LANES]

    cmpv_ref[...] = _dot(hb, w_ref[:, TM_VCMP:TM_VCMP + LANES])

    first = pl.program_id(0) % tiles_per_seq == 0

    @pl.when(first)
    def _():
        cbuf[0:SUBLANES, :] = jnp.zeros((SUBLANES, 3 * GDN_WIDTH), F32)

    cbuf[SUBLANES:SUBLANES + tm, :] = _dot(hb, w_ref[:, TM_GDN:TM_GDN + 3 * GDN_WIDTH])
    for cb in range(12):
        cols = slice(cb * LANES, (cb + 1) * LANES)
        y = jnp.zeros((tm, LANES), F32)
        for k in range(GDN_CONV):
            r0 = SUBLANES - (GDN_CONV - 1) + k
            y = y + cbuf[r0:r0 + tm, cols] * convw_ref[k:k + 1, cols]
        y = _silu(y)
        if cb < 8:
            y = y * lax.rsqrt(jnp.sum(y * y, axis=-1, keepdims=True) + RMS_EPS)
        if cb < 4:
            gq_ref[:, cols] = (y * (GDN_DIM ** -0.5)).astype(BF16)
        elif cb < 8:
            gk_ref[:, (cb - 4) * LANES:(cb - 3) * LANES] = y.astype(BF16)
        else:
            gv_ref[:, (cb - 8) * LANES:(cb - 7) * LANES] = y.astype(BF16)
    cbuf[0:SUBLANES, :] = cbuf[tm:tm + SUBLANES, :]

    z_ref[...] = _dot(hb, w_ref[:, TM_Z:TM_Z + GDN_WIDTH]).astype(BF16)

    sm = _dot(hb, w_ref[:, TM_SMALL:TM_SMALL + LANES])
    lane1 = lax.broadcasted_iota(jnp.int32, (1, LANES), 1)
    gdec = -jnp.exp(alog_ref[...]) * _softplus(sm + dtb_ref[...])
    small_ref[...] = jnp.where(lane1 < SMALL_A, _sigmoid(sm), gdec)


def _in_proj(x2, pos_row, g1, w_t, w_r, invf, convw, alog, dtb, alogt, dtbt, seq_len):
    n = x2.shape[0]
    row = lambda i: (i, 0)
    colb = lambda i: (0, i)
    lead = lambda i: (i, 0, 0)
    const = lambda i: (0, 0)
    nlb = TM // LANES
    out_shapes = (
        jax.ShapeDtypeStruct((NSA_WIDTH, n), BF16),
        jax.ShapeDtypeStruct((n, LANES), F32),
        jax.ShapeDtypeStruct((n, LANES), BF16),
        jax.ShapeDtypeStruct((n, LANES), BF16),
        jax.ShapeDtypeStruct((n // LANES, LANES, LANES), BF16),
        jax.ShapeDtypeStruct((n // LANES, LANES, LANES), BF16),
        jax.ShapeDtypeStruct((n, LANES), F32),
        jax.ShapeDtypeStruct((n, GDN_WIDTH), BF16),
        jax.ShapeDtypeStruct((n, GDN_WIDTH), BF16),
        jax.ShapeDtypeStruct((n, GDN_WIDTH), BF16),
        jax.ShapeDtypeStruct((n, GDN_WIDTH), BF16),
        jax.ShapeDtypeStruct((n, LANES), F32),
        jax.ShapeDtypeStruct((N_SMALL, n), F32),
        jax.ShapeDtypeStruct((n // LANES, SUBLANES, LANES), F32),
    )
    out_specs = (
        pl.BlockSpec((NSA_WIDTH, TM), colb),
        pl.BlockSpec((TM, LANES), row), pl.BlockSpec((TM, LANES), row), pl.BlockSpec((TM, LANES), row),
        pl.BlockSpec((nlb, LANES, LANES), lead), pl.BlockSpec((nlb, LANES, LANES), lead),
        pl.BlockSpec((TM, LANES), row),
        pl.BlockSpec((TM, GDN_WIDTH), row), pl.BlockSpec((TM, GDN_WIDTH), row),
        pl.BlockSpec((TM, GDN_WIDTH), row), pl.BlockSpec((TM, GDN_WIDTH), row),
        pl.BlockSpec((TM, LANES), row),
        pl.BlockSpec((N_SMALL, TM), colb),
        pl.BlockSpec((nlb, SUBLANES, LANES), lead),
    )
    in_specs = [
        pl.BlockSpec((TM, D_MODEL), row), pl.BlockSpec((1, TM), colb), pl.BlockSpec((1, D_MODEL), const),
        pl.BlockSpec((TR_ROWS, D_MODEL), const), pl.BlockSpec((D_MODEL, TM_COLS), const),
        pl.BlockSpec((ROPE_DIM // 2, 1), const), pl.BlockSpec((GDN_CONV, 3 * GDN_WIDTH), const),
        pl.BlockSpec((1, LANES), const), pl.BlockSpec((1, LANES), const),
        pl.BlockSpec((N_SMALL, 1), const), pl.BlockSpec((N_SMALL, 1), const),
    ]
    return pl.pallas_call(
        functools.partial(_in_proj_kernel, seq_len // TM),
        grid=(n // TM,), in_specs=in_specs, out_specs=out_specs, out_shape=out_shapes,
        scratch_shapes=[pltpu.VMEM((TM + 2 * SUBLANES, 3 * GDN_WIDTH), F32)],
        compiler_params=pltpu.CompilerParams(dimension_semantics=("arbitrary",),
                                             vmem_limit_bytes=VMEM_LIMIT),
        name="in_proj",
    )(x2, pos_row, g1, w_t, w_r, invf, convw, alog, dtb, alogt, dtbt)


def _prep_in_proj_weights(w_in, gdn_a_log, gdn_dt_bias):
    q = w_in[:, :NSA_WIDTH].reshape(D_MODEL, 2, 4, HEAD_DIM)
    q = jnp.transpose(q, (0, 2, 1, 3)).reshape(D_MODEL, NSA_WIDTH)
    kv = w_in[:, OFF_KV:OFF_GATE].reshape(D_MODEL, 6, LANES)
    gate = w_in[:, OFF_GATE:OFF_GQKV].reshape(D_MODEL, 2, 4, 3)
    gate = jnp.transpose(gate, (0, 3, 2, 1)).reshape(D_MODEL, 24)
    small = jnp.concatenate([gate, w_in[:, OFF_B:OFF_A], w_in[:, OFF_A:OFF_A + GDN_HEADS]], axis=1)
    w_t = jnp.concatenate([q, kv[:, 0], kv[:, 2], kv[:, 4], kv[:, 3], kv[:, 5], small], axis=1).T.astype(BF16)
    w_r = jnp.concatenate([kv[:, 1], w_in[:, OFF_GQKV:OFF_Z], w_in[:, OFF_Z:OFF_B], small,
                           jnp.zeros((D_MODEL, LANES - N_SMALL), w_in.dtype)], axis=1).astype(BF16)
    alog = jnp.zeros((LANES,), F32).at[SMALL_A:SMALL_A + GDN_HEADS].set(gdn_a_log.astype(F32))
    dtb = jnp.zeros((LANES,), F32).at[SMALL_A:SMALL_A + GDN_HEADS].set(gdn_dt_bias.astype(F32))
    return w_t, w_r, alog[None, :], dtb[None, :], alog[:N_SMALL, None], dtb[:N_SMALL, None]


def _rope_inv_freq():
    half = ROPE_DIM // 2
    return jnp.power(ROPE_THETA, -jnp.arange(half, dtype=F32) * (2.0 / ROPE_DIM))[:, None]


def _compress_kernel(xk_ref, xv_ref, pos_ref, w1_ref, w2_ref, out_ref):
    nblk = xk_ref.shape[0] // CMP_STRIDE
    acc_lo = jnp.zeros((nblk, 4 * CMP_HIDDEN), F32)
    acc_hi = jnp.zeros((nblk, 4 * CMP_HIDDEN), F32)
    for j in range(CMP_STRIDE):
        xj = jnp.concatenate([xk_ref[pl.ds(j, nblk, stride=CMP_STRIDE), :],
                              xv_ref[pl.ds(j, nblk, stride=CMP_STRIDE), :]], axis=1)
        acc_lo = acc_lo + _dot((xj + pos_ref[j:j + 1, :]).astype(BF16), w1_ref[j])
        acc_hi = acc_hi + _dot((xj + pos_ref[CMP_STRIDE + j:CMP_STRIDE + j + 1, :]).astype(BF16),
                               w1_ref[CMP_STRIDE + j])
    pre = acc_lo + pltpu.roll(acc_hi, nblk - 1, 0)
    kvc = _dot(_silu(pre).astype(BF16), w2_ref[...])
    out_ref[0:nblk, :] = kvc[:, 0:LANES]
    out_ref[nblk:2 * nblk, :] = kvc[:, LANES:2 * LANES].T


def _compress(cmp_k, cmp_v, pos_rows, w1_bd, w2_bd, seq_len):
    n = cmp_k.shape[0]
    nb = n // seq_len
    nblk = seq_len // CMP_STRIDE
    assert nblk == LANES
    return pl.pallas_call(
        _compress_kernel,
        grid=(nb,),
        in_specs=[pl.BlockSpec((seq_len, LANES), lambda b: (b, 0)),
                  pl.BlockSpec((seq_len, LANES), lambda b: (b, 0)),
                  pl.BlockSpec((CMP_BLOCK, 256), lambda b: (0, 0)),
                  pl.BlockSpec((CMP_BLOCK, 256, 4 * CMP_HIDDEN), lambda b: (0, 0, 0)),
                  pl.BlockSpec((4 * CMP_HIDDEN, 256), lambda b: (0, 0))],
        out_specs=pl.BlockSpec((2 * nblk, LANES), lambda b: (b, 0)),
        out_shape=jax.ShapeDtypeStruct((nb * 2 * nblk, LANES), F32),
        compiler_params=pltpu.CompilerParams(dimension_semantics=("arbitrary",),
                                             vmem_limit_bytes=VMEM_LIMIT),
        name="nsa_compress",
    )(cmp_k, cmp_v, pos_rows, w1_bd, w2_bd)


def _prep_compress_weights(cmp_pos, cmp_w1, cmp_w2):
    w1 = cmp_w1.reshape(2, CMP_BLOCK, HEAD_DIM, CMP_HIDDEN)
    w1_bd = jnp.zeros((CMP_BLOCK, 4, HEAD_DIM, 4, CMP_HIDDEN), F32)
    w2_bd = jnp.zeros((4, CMP_HIDDEN, 4, HEAD_DIM), F32)
    for s in range(4):
        w1_bd = w1_bd.at[:, s, :, s, :].set(w1[s // 2])
        w2_bd = w2_bd.at[s, :, s, :].set(cmp_w2[s // 2])
    w1_bd = w1_bd.reshape(CMP_BLOCK, 256, 4 * CMP_HIDDEN).astype(BF16)
    w2_bd = w2_bd.reshape(4 * CMP_HIDDEN, 256).astype(BF16)
    pos_rows = jnp.concatenate([cmp_pos[0], cmp_pos[0], cmp_pos[1], cmp_pos[1]], axis=-1).astype(F32)
    return pos_rows, w1_bd, w2_bd


TQ = 256
KC = 256
N_SEL = 32
ROWS = N_HEADS * TQ
WCHUNKS = WINDOW // KC + 1
ONES_ROWS = 16


def _nsa_kernel(qt_ref, ks_ref, kw_ref, vst_ref, vwt_ref, kvc_ref, gt_ref, ovt_ref, eblk_ref, o_ref,
                qaug, sbuf, mxbuf, ms_sc, mw_sc, accs_sc, accw_sc):
    tile = pl.program_id(1)
    t0 = tile * TQ
    tcol = t0 + lax.broadcasted_iota(jnp.int32, (1, TQ), 1)
    krow = lax.broadcasted_iota(jnp.int32, (KC, 1), 0)
    ones = jnp.ones((ONES_ROWS, KC), BF16)
    groups = [slice(r * TQ, (r + 1) * TQ) for r in range(N_HEADS)]
    blocks_per_chunk = KC // LANES

    zhalf = jnp.zeros((HEAD_DIM, TQ), BF16)
    for m in range(4):
        blk = qt_ref[m * LANES:(m + 1) * LANES, :]
        qaug[0:LANES, groups[2 * m]] = jnp.concatenate([blk[0:HEAD_DIM], zhalf], axis=0)
        qaug[0:LANES, groups[2 * m + 1]] = jnp.concatenate([zhalf, blk[HEAD_DIM:LANES]], axis=0)

    def v_chunk(vt_ref, k0):
        b0 = k0 // LANES
        return jnp.concatenate(
            [jnp.concatenate([vt_ref[b0 + j] for j in range(blocks_per_chunk)], axis=1), ones], axis=0)

    def flash_update(m_ref, acc_ref, s_list, vt, bias):
        if bias is not None:
            s_list = [y + bias for y in s_list]
        m_old = [m_ref[:, cols] for cols in groups]
        m_new = [jnp.maximum(m_old[r], jnp.max(s_list[r], axis=0, keepdims=True)) for r in range(N_HEADS)]
        p = [jnp.exp2((s_list[r] - m_new[r]).astype(BF16)) for r in range(N_HEADS)]
        pv = [_dot(vt, y) for y in p]
        for r, cols in enumerate(groups):
            m_ref[:, cols] = m_new[r]
            acc_ref[:, cols] = acc_ref[:, cols] * jnp.exp2(m_old[r] - m_new[r]) + pv[r]

    mw_sc[...] = jnp.full(mw_sc.shape, NEG_INF, F32)
    accw_sc[...] = jnp.zeros(accw_sc.shape, F32)

    def window_chunk(j):
        start = t0 - WINDOW + j * KC
        k0 = pl.multiple_of(jnp.maximum(start, 0), KC)
        kpos = start + krow
        diff = tcol - kpos
        bias = jnp.where((kpos >= 0) & (diff >= 0) & (diff < WINDOW), 0.0, NEG_INF)
        kwin = kw_ref[pl.ds(k0, KC), :]
        return [_dot(kwin, qaug[0:LANES, cols]) for cols in groups], v_chunk(vwt_ref, k0), bias

    s_w, vt_w, bias_w = window_chunk(WCHUNKS - 1)

    nblk = kvc_ref.shape[0] // 2
    kc = kvc_ref[0:nblk, :].astype(BF16)
    vct = kvc_ref[nblk:2 * nblk, :].astype(BF16)
    nrow = lax.broadcasted_iota(jnp.int32, (nblk, 1), 0)
    vbias = jnp.where(nrow * CMP_STRIDE + (CMP_BLOCK - 1) <= tcol, 0.0, NEG_INF)
    has_any = tcol >= CMP_BLOCK - 1
    s_c = [_dot(kc, qaug[0:LANES, cols]) + vbias for cols in groups]
    e_c = [jnp.exp2(y - jnp.max(y, axis=0, keepdims=True)) for y in s_c]
    p_c = [jnp.where(has_any, y * (1.0 / jnp.sum(y, axis=0, keepdims=True)), 0.0) for y in e_c]
    o_c = [_dot(vct, y.astype(BF16)) for y in p_c]

    flash_update(mw_sc, accw_sc, s_w, vt_w, bias_w)

    jrow = lax.broadcasted_iota(jnp.int32, (N_SEL, 1), 0)
    cur = lax.shift_right_logical(tcol, 6)
    forced = (jrow == 0) | (jrow == cur) | (jrow == cur - 1)
    causal = jrow <= cur
    for g in range(N_GROUPS):
        psum = (p_c[g] + p_c[2 + g]) + (p_c[4 + g] + p_c[6 + g])
        imp_t = _dot3(ovt_ref[...], psum)
        score = jnp.where(forced, FORCE_SCORE, jnp.where(causal, imp_t[0:N_SEL, :], NEG_INF))
        cnt = jnp.zeros((N_SEL, TQ), jnp.int32)
        for jp in range(N_SEL):
            rowv = score[jp:jp + 1, :]
            beats = (rowv > score) | ((rowv == score) & (jrow > jp))
            cnt = cnt + jnp.where(beats, 1, 0)
        sel = (cnt < SEL_TOP_N) & causal
        bias = jnp.concatenate([jnp.where(sel, 0.0, MASK_BIAS),
                                jnp.zeros((LANES - N_SEL, TQ), F32)], axis=0).astype(BF16)
        for m in range(4):
            qaug[LANES:2 * LANES, groups[2 * m + g]] = bias

    for j in range(WCHUNKS - 2, -1, -1):
        s_w, vt_w, bias_w = window_chunk(j)
        flash_update(mw_sc, accw_sc, s_w, vt_w, bias_w)

    ms_sc[...] = jnp.full(ms_sc.shape, NEG_INF, F32)
    accs_sc[...] = jnp.zeros(accs_sc.shape, F32)
    buf_a, buf_b = sbuf.at[0], sbuf.at[1]
    mx_a, mx_b = mxbuf.at[0], mxbuf.at[1]

    def produce(buf, mx, c, bias=None):
        k0 = pl.multiple_of(c * KC, KC)
        kaug = jnp.concatenate([ks_ref[pl.ds(k0, KC), :], eblk_ref[pl.ds(k0, KC), :]], axis=1)
        for cols in groups:
            s = _dot(kaug, qaug[:, cols])
            if bias is not None:
                s = s + bias
            buf[:, cols] = s
            mx[:, cols] = jnp.max(s, axis=0, keepdims=True)

    def consume(buf, mx, c):
        vt = v_chunk(vst_ref, c * KC)
        m_old = [ms_sc[:, cols] for cols in groups]
        m_new = [jnp.maximum(m_old[r], mx[:, cols]) for r, cols in enumerate(groups)]
        p = [jnp.exp2((buf[:, cols] - m_new[r]).astype(BF16)) for r, cols in enumerate(groups)]
        pv = [_dot(vt, y) for y in p]
        for r, cols in enumerate(groups):
            ms_sc[:, cols] = m_new[r]
            accs_sc[:, cols] = accs_sc[:, cols] * jnp.exp2(m_old[r] - m_new[r]) + pv[r]

    n_full = tile
    n_pairs = n_full // 2
    odd = n_full % 2 == 1
    produce(buf_a, mx_a, tile, jnp.where(t0 + krow <= tcol, 0.0, NEG_INF))

    def body(j, carry):
        produce(buf_b, mx_b, 2 * j)
        consume(buf_a, mx_a, jnp.where(j == 0, tile, 2 * j - 1))
        produce(buf_a, mx_a, 2 * j + 1)
        consume(buf_b, mx_b, 2 * j)
        return carry

    lax.fori_loop(0, n_pairs, body, 0)
    last_a = jnp.where(n_pairs == 0, tile, 2 * n_pairs - 1)

    @pl.when(odd)
    def _():
        produce(buf_b, mx_b, 2 * n_pairs)

    consume(buf_a, mx_a, last_a)

    @pl.when(odd)
    def _():
        consume(buf_b, mx_b, 2 * n_pairs)

    gates = gt_ref[...]
    for m in range(4):
        halves = []
        for g in range(N_GROUPS):
            r = 2 * m + g
            acc_s = accs_sc[:, groups[r]]
            acc_w = accw_sc[:, groups[r]]
            o_r = (gates[r:r + 1] * o_c[r]
                   + (gates[8 + r:9 + r] * (1.0 / acc_s[LANES:LANES + 1])) * acc_s[0:LANES]
                   + (gates[16 + r:17 + r] * (1.0 / acc_w[LANES:LANES + 1])) * acc_w[0:LANES])
            halves.append(o_r[g * HEAD_DIM:(g + 1) * HEAD_DIM])
        o_ref[:, m * LANES:(m + 1) * LANES] = jnp.concatenate(halves, axis=0).T.astype(BF16)


def _nsa(qt, ks, kw, vst, vwt, kvc, small_t, ovt, eblk, seq_len):
    n = ks.shape[0]
    nb = n // seq_len
    nq = seq_len // TQ
    nkb = seq_len // LANES
    seq = lambda b, i: (b, 0)
    return pl.pallas_call(
        _nsa_kernel,
        grid=(nb, nq),
        in_specs=[pl.BlockSpec((NSA_WIDTH, TQ), lambda b, i: (0, b * nq + i)),
                  pl.BlockSpec((seq_len, LANES), seq), pl.BlockSpec((seq_len, LANES), seq),
                  pl.BlockSpec((nkb, LANES, LANES), lambda b, i: (b, 0, 0)),
                  pl.BlockSpec((nkb, LANES, LANES), lambda b, i: (b, 0, 0)),
                  pl.BlockSpec((2 * LANES, LANES), seq),
                  pl.BlockSpec((N_SMALL, TQ), lambda b, i: (0, b * nq + i)),
                  pl.BlockSpec((LANES, LANES), lambda b, i: (0, 0)),
                  pl.BlockSpec((seq_len, LANES), lambda b, i: (0, 0))],
        out_specs=pl.BlockSpec((TQ, NSA_WIDTH), lambda b, i: (b * nq + i, 0)),
        out_shape=jax.ShapeDtypeStruct((n, NSA_WIDTH), BF16),
        scratch_shapes=[pltpu.VMEM((2 * LANES, ROWS), BF16),
                        pltpu.VMEM((2, KC, ROWS), F32),
                        pltpu.VMEM((2, 1, ROWS), F32),
                        pltpu.VMEM((1, ROWS), F32), pltpu.VMEM((1, ROWS), F32),
                        pltpu.VMEM((LANES + ONES_ROWS, ROWS), F32),
                        pltpu.VMEM((LANES + ONES_ROWS, ROWS), F32)],
        compiler_params=pltpu.CompilerParams(dimension_semantics=("arbitrary", "arbitrary"),
                                             vmem_limit_bytes=VMEM_LIMIT),
        name="nsa_attention",
    )(qt, ks, kw, vst, vwt, kvc, small_t, ovt, eblk)


def _nsa_constants(seq_len):
    n_cmp = (seq_len - CMP_BLOCK) // CMP_STRIDE + 1
    s = jnp.arange(LANES)[:, None]
    nn = jnp.arange(LANES)[None, :]
    cs = nn * CMP_STRIDE
    ss = s * SEL_BLOCK
    ovt = (cs < ss + SEL_BLOCK) & (cs + CMP_BLOCK > ss) & (s < seq_len // SEL_BLOCK) & (nn < n_cmp)
    k = jnp.arange(seq_len)[:, None]
    eblk = (k // SEL_BLOCK) == jnp.arange(LANES)[None, :]
    return ovt.astype(BF16), eblk.astype(BF16)


PAIR = 2 * GDN_CHUNK
N_DOUBLINGS = 5
GT_G = 4


def _gdn_kernel(q_ref, k_ref, v_ref, z_ref, small_ref, gt_ref, ng_ref, o_ref, s_sc):
    seq_len = q_ref.shape[0]
    ri = lax.broadcasted_iota(jnp.int32, (PAIR, PAIR), 0)
    ci = lax.broadcasted_iota(jnp.int32, (PAIR, PAIR), 1)
    same = lax.shift_right_logical(ri, 6) == lax.shift_right_logical(ci, 6)
    incl = same & (ri >= ci)
    strict = same & (ri > ci)
    first_cols = ci < GDN_CHUNK
    ltri = jnp.where(incl, 1.0, 0.0).astype(BF16)
    csum = jnp.where(same, 1.0, 0.0).astype(BF16)
    tot = [jnp.where(first_cols, 1.0, 0.0).astype(BF16),
           jnp.where(first_cols, 0.0, 1.0).astype(BF16)]
    s_sc[...] = jnp.zeros(s_sc.shape, F32)

    def body(pi, carry):
        r0 = pl.multiple_of(pi * PAIR, PAIR)
        rows = pl.ds(r0, PAIR)
        sm = small_ref[rows, :]
        cs = _dot3(ltri, sm)
        ctot = _dot3(csum, sm)
        gt = gt_ref[pi]
        csr = _dot3(ltri, gt, nt=True)
        glast = [_dot3(tot[0], gt, nt=True), _dot3(tot[1], gt, nt=True)]
        hs = range(GDN_HEADS)
        cols = [slice(h * GDN_DIM, (h + 1) * GDN_DIM) for h in hs]
        q = [q_ref[rows, c] for c in cols]
        k = [k_ref[rows, c] for c in cols]
        kf = [x.astype(F32) for x in k]
        beta = [sm[:, SMALL_B + h:SMALL_B + h + 1] for h in hs]
        gcc = [cs[:, SMALL_A + h:SMALL_A + h + 1] for h in hs]
        gcr = [csr[GT_G + h:GT_G + h + 1, :] for h in hs]
        decay = [jnp.exp(jnp.where(incl, gcc[h] - gcr[h], NEG_INF)) for h in hs]
        egc = [jnp.exp(g) for g in gcc]
        kb = [kf[h] * beta[h] for h in hs]
        kk = [_dot_nt(kb[h].astype(BF16), k[h]) for h in hs]
        p = [jnp.where(strict, -(kk[h] * decay[h]), 0.0) for h in hs]
        x = [jnp.concatenate([v_ref[rows, cols[h]].astype(F32) * beta[h], kb[h] * egc[h]], axis=1) for h in hs]
        for i in range(N_DOUBLINGS + 1):
            pb = [y.astype(BF16) for y in p]
            x = [x[h] + _dot(pb[h], x[h].astype(BF16)) for h in hs]
            if i < N_DOUBLINGS:
                p = [_dot(y, y) for y in pb]
        uw = [y.astype(BF16) for y in x]
        qk = [(_dot_nt(q[h], k[h]) * decay[h]).astype(BF16) for h in hs]
        ke_t = [(kf[h] * jnp.exp(ctot[:, SMALL_A + h:SMALL_A + h + 1] - gcc[h])).T for h in hs]
        kw = [[_dot(jnp.where(first_cols, y, 0.0).astype(BF16), uw[h]) for h, y in enumerate(ke_t)],
              [_dot(jnp.where(first_cols, 0.0, y).astype(BF16), uw[h]) for h, y in enumerate(ke_t)]]
        qw = [_dot(qk[h], uw[h]) for h in hs]
        qm = [(q[h].astype(F32) * egc[h] - qw[h][:, GDN_DIM:2 * GDN_DIM]).astype(BF16) for h in hs]
        s = [s_sc[h] for h in hs]
        os = []
        for c in range(2):
            half = slice(c * GDN_CHUNK, (c + 1) * GDN_CHUNK)
            sb = [y.astype(BF16) for y in s]
            os.append([_dot(qm[h][half], sb[h]) for h in hs])
            ks = [_dot(kw[c][h][:, GDN_DIM:2 * GDN_DIM].astype(BF16), sb[h]) for h in hs]
            s = [s[h] * jnp.exp(glast[c][GT_G + h:GT_G + h + 1, :]) - ks[h] + kw[c][h][:, 0:GDN_DIM] for h in hs]
        for h in hs:
            s_sc[h] = s[h]
            o = jnp.concatenate([os[0][h], os[1][h]], axis=0) + qw[h][:, 0:GDN_DIM]
            o = o * lax.rsqrt(jnp.mean(o * o, axis=-1, keepdims=True) + RMS_EPS) * ng_ref[...]
            o_ref[rows, cols[h]] = (o * _silu(z_ref[rows, cols[h]].astype(F32))).astype(BF16)
        return carry

    lax.fori_loop(0, seq_len // PAIR, body, 0)


def _gdn(gq, gk, gv, z, small, gt, norm_g, seq_len):
    n = gq.shape[0]
    nb = n // seq_len
    seq = lambda b: (b, 0)
    return pl.pallas_call(
        _gdn_kernel,
        grid=(nb,),
        in_specs=[pl.BlockSpec((seq_len, GDN_WIDTH), seq), pl.BlockSpec((seq_len, GDN_WIDTH), seq),
                  pl.BlockSpec((seq_len, GDN_WIDTH), seq), pl.BlockSpec((seq_len, GDN_WIDTH), seq),
                  pl.BlockSpec((seq_len, LANES), seq),
                  pl.BlockSpec((seq_len // PAIR, SUBLANES, PAIR), lambda b: (b, 0, 0)),
                  pl.BlockSpec((1, GDN_DIM), lambda b: (0, 0))],
        out_specs=pl.BlockSpec((seq_len, GDN_WIDTH), seq),
        out_shape=jax.ShapeDtypeStruct((n, GDN_WIDTH), BF16),
        scratch_shapes=[pltpu.VMEM((GDN_HEADS, GDN_DIM, GDN_DIM), F32)],
        compiler_params=pltpu.CompilerParams(dimension_semantics=("arbitrary",),
                                             vmem_limit_bytes=VMEM_LIMIT),
        name="gated_delta_rule",
    )(gq, gk, gv, z, small, gt, norm_g)


FF_CHUNK = 256


def _rms(x, g):
    return x * lax.rsqrt(jnp.mean(x * x, axis=-1, keepdims=True) + RMS_EPS) * g


def _out_mlp_kernel(x_ref, on_ref, og_ref, gn_ref, wo_ref, g2_ref, wg_ref, wu_ref, wd_ref, gf_ref, out_ref):
    o_nsa = _rms(on_ref[...].astype(F32), gn_ref[...]).astype(BF16)
    mix = jnp.concatenate([o_nsa, og_ref[...]], axis=1)
    h = x_ref[...] + _dot(mix, wo_ref[...])
    hn = _rms(h, g2_ref[...]).astype(BF16)
    y = jnp.zeros_like(h)
    for c in range(D_FF // FF_CHUNK):
        cols = slice(c * FF_CHUNK, (c + 1) * FF_CHUNK)
        act = _silu(_dot(hn, wg_ref[:, cols])) * _dot(hn, wu_ref[:, cols])
        y = y + _dot(act.astype(BF16), wd_ref[cols, :])
    out_ref[...] = _rms(h + y, gf_ref[...])


def _out_mlp(x2, o_nsa, o_gdn, gn, wo, g2, wg, wu, wd, gf):
    n = x2.shape[0]
    row = lambda i: (i, 0)
    const = lambda i: (0, 0)
    resident = lambda shape: pl.BlockSpec(shape, const, pipeline_mode=pl.Buffered(1))
    return pl.pallas_call(
        _out_mlp_kernel,
        grid=(n // TM,),
        in_specs=[pl.BlockSpec((TM, D_MODEL), row), pl.BlockSpec((TM, NSA_WIDTH), row),
                  pl.BlockSpec((TM, GDN_WIDTH), row), pl.BlockSpec((1, NSA_WIDTH), const),
                  resident((D_MODEL, D_MODEL)), pl.BlockSpec((1, D_MODEL), const),
                  resident((D_MODEL, D_FF)), resident((D_MODEL, D_FF)), resident((D_FF, D_MODEL)),
                  pl.BlockSpec((1, D_MODEL), const)],
        out_specs=pl.BlockSpec((TM, D_MODEL), row),
        out_shape=jax.ShapeDtypeStruct((n, D_MODEL), F32),
        compiler_params=pltpu.CompilerParams(dimension_semantics=("arbitrary",),
                                             vmem_limit_bytes=VMEM_LIMIT),
        name="out_mlp",
    )(x2, o_nsa, o_gdn, gn, wo, g2, wg, wu, wd, gf)


def _nsa_head_perm():
    c = jnp.arange(NSA_WIDTH)
    return ((c // LANES) + 4 * ((c % LANES) // HEAD_DIM)) * HEAD_DIM + c % HEAD_DIM


def _layer(x2, pos_row, seq_len, norm1_g, w_in, cmp_pos, cmp_w1, cmp_w2, nsa_norm_g, gdn_conv_w, gdn_a_log,
           gdn_dt_bias, gdn_norm_g, w_out, norm2_g, w_gate, w_up, w_down, out_g):
    w_t, w_r, alog, dtb, alogt, dtbt = _prep_in_proj_weights(w_in, gdn_a_log, gdn_dt_bias)
    qt, cmp_k, ks, kw, vst, vwt, cmp_v, gq, gk, gv, z, small, small_t, gt = _in_proj(
        x2, pos_row, norm1_g[None, :].astype(F32), w_t, w_r, _rope_inv_freq(), gdn_conv_w.astype(F32),
        alog, dtb, alogt, dtbt, seq_len)
    kvc = _compress(cmp_k, cmp_v, *_prep_compress_weights(cmp_pos, cmp_w1, cmp_w2), seq_len)
    ovt, eblk = _nsa_constants(seq_len)
    o_nsa = _nsa(qt, ks, kw, vst, vwt, kvc, small_t, ovt, eblk, seq_len)
    o_gdn = _gdn(gq, gk, gv, z, small, gt, gdn_norm_g[None, :].astype(F32), seq_len)
    perm = _nsa_head_perm()
    wo = jnp.concatenate([w_out[:NSA_WIDTH][perm], w_out[NSA_WIDTH:]], axis=0).astype(BF16)
    return _out_mlp(x2, o_nsa, o_gdn, nsa_norm_g[perm][None, :].astype(F32), wo, norm2_g[None, :].astype(F32),
                    w_gate.astype(BF16), w_up.astype(BF16), w_down.astype(BF16), out_g[None, :].astype(F32))


def kernel(x, positions, norm1_g, w_in, cmp_pos, cmp_w1, cmp_w2, nsa_norm_g, gdn_conv_w, gdn_a_log,
           gdn_dt_bias, gdn_norm_g, w_out, norm2_g, w_gate, w_up, w_down, final_g):
    nb, seq_len, d = x.shape
    depth = w_in.shape[0]
    assert d == D_MODEL and seq_len % TM == 0 and seq_len // SEL_BLOCK == N_SEL and depth == 1
    x2 = x.reshape(nb * seq_len, d)
    pos_row = positions.reshape(1, nb * seq_len)
    out = _layer(x2, pos_row, seq_len, norm1_g[0], w_in[0], cmp_pos[0], cmp_w1[0], cmp_w2[0], nsa_norm_g[0],
                 gdn_conv_w[0], gdn_a_log[0], gdn_dt_bias[0], gdn_norm_g[0], w_out[0], norm2_g[0],
                 w_gate[0], w_up[0], w_down[0], final_g)
    return out.reshape(nb, seq_len, d)
```

```python
import functools
import math

import jax
import jax.numpy as jnp
from jax import lax
from jax.experimental import pallas as pl
from jax.experimental.pallas import tpu as pltpu

F32 = jnp.float32
BF16 = jnp.bfloat16

LANES = 128
SUBLANES = 8

D_MODEL = 1024
N_HEADS = 8
N_GROUPS = 2
HEAD_DIM = 64
CMP_BLOCK = 32
CMP_STRIDE = 16
CMP_HIDDEN = 128
SEL_BLOCK = 64
SEL_TOP_N = 8
WINDOW = 512
ROPE_THETA = 500000.0
ROPE_DIM = 16
GDN_HEADS = 4
GDN_DIM = 128
GDN_CONV = 4
GDN_CHUNK = 64
NSA_WIDTH = 512
GDN_WIDTH = 512
D_FF = 2816
RMS_EPS = 1e-6
NEG_INF = -1e30
MASK_BIAS = -1e9
FORCE_SCORE = 1e9

OFF_KV = 512
OFF_GATE = OFF_KV + 6 * 128
OFF_GQKV = OFF_GATE + 24
OFF_Z = OFF_GQKV + 3 * GDN_WIDTH
OFF_B = OFF_Z + GDN_WIDTH
OFF_A = OFF_B + GDN_HEADS

TR_Q = 0
TR_K = 512
TR_V = 896
TR_SMALL = 1152
TR_ROWS = 1184
TM_VCMP = 0
TM_GDN = 128
TM_Z = 1664
TM_SMALL = 2176
TM_COLS = 2304
SMALL_B = 24
SMALL_A = 28
N_SMALL = 32

Q_SCALE = HEAD_DIM ** -0.5 * math.log2(math.e)
TM = 512
VMEM_LIMIT = 56 * 1024 * 1024


def _dot(a, b):
    return jnp.dot(a, b, preferred_element_type=F32)


def _dot_nt(a, b):
    return lax.dot_general(a, b, (((1,), (1,)), ((), ())), preferred_element_type=F32)


def _sigmoid(x):
    return 1.0 / (1.0 + jnp.exp(-x))


def _silu(x):
    return x * _sigmoid(x)


def _softplus(x):
    return jnp.maximum(x, 0.0) + jnp.log(1.0 + jnp.exp(-jnp.abs(x)))


def _split3(x):
    h1 = x.astype(BF16)
    r1 = x - h1.astype(F32)
    h2 = r1.astype(BF16)
    h3 = (r1 - h2.astype(F32)).astype(BF16)
    return h1, h2, h3


def _dot3(mat01, x, nt=False):
    f = (lambda a: _dot_nt(a, mat01)) if nt else (lambda a: _dot(mat01, a))
    h1, h2, h3 = _split3(x)
    return f(h1) + f(h2) + f(h3)


def _rope_t(blk, cosv, sinv):
    half = ROPE_DIM // 2
    parts = []
    for h in range(blk.shape[0] // HEAD_DIM):
        b = h * HEAD_DIM
        x0 = blk[b:b + half]
        x1 = blk[b + half:b + ROPE_DIM]
        parts += [x0 * cosv - x1 * sinv, x1 * cosv + x0 * sinv, blk[b + ROPE_DIM:b + HEAD_DIM]]
    return jnp.concatenate(parts, axis=0)


def _in_proj_kernel(tiles_per_seq, x_ref, pos_ref, g1_ref, wt_ref, w_ref, invf_ref, convw_ref,
                    alog_ref, dtb_ref, alogt_ref, dtbt_ref,
                    qt_ref, kc_ref, ks_ref, kw_ref, vst_ref, vwt_ref, cmpv_ref,
                    gq_ref, gk_ref, gv_ref, z_ref, small_ref, smallt_ref, gt_ref,
                    cbuf):
    tm = x_ref.shape[0]
    nlb = tm // LANES
    x = x_ref[...]
    hn = x * lax.rsqrt(jnp.mean(x * x, axis=-1, keepdims=True) + RMS_EPS) * g1_ref[...]
    hb = hn.astype(BF16)

    first = pl.program_id(0) % tiles_per_seq == 0

    @pl.when(first)
    def _():
        cbuf[0:SUBLANES, :] = jnp.zeros((SUBLANES, 3 * GDN_WIDTH), F32)

    cbuf[SUBLANES:SUBLANES + tm, :] = _dot(hb, w_ref[:, TM_GDN:TM_GDN + 3 * GDN_WIDTH])
    yt = _dot_nt(wt_ref[...], hb)

    ones_sq = jnp.ones((LANES, LANES), BF16)
    for cb in range(12):
        cols = slice(cb * LANES, (cb + 1) * LANES)
        zx = cbuf[0:SUBLANES + tm, cols]
        y = zx * convw_ref[GDN_CONV - 1:GDN_CONV, cols]
        for s in range(1, GDN_CONV):
            y = y + pltpu.roll(zx, s, 0) * convw_ref[GDN_CONV - 1 - s:GDN_CONV - s, cols]
        y = _silu(y[SUBLANES:])
        if cb < 8:
            y = y * lax.rsqrt(_dot((y * y).astype(BF16), ones_sq) + RMS_EPS)
        if cb < 4:
            gq_ref[:, cols] = (y * (GDN_DIM ** -0.5)).astype(BF16)
        elif cb < 8:
            gk_ref[:, (cb - 4) * LANES:(cb - 3) * LANES] = y.astype(BF16)
        else:
            gv_ref[:, (cb - 8) * LANES:(cb - 7) * LANES] = y.astype(BF16)
    cbuf[0:SUBLANES, :] = cbuf[tm:tm + SUBLANES, :]

    ang = invf_ref[...] * pos_ref[...].astype(F32)
    cosv = jnp.cos(ang)
    sinv = jnp.sin(ang)
    for m in range(4):
        blk = _rope_t(yt[TR_Q + m * LANES:TR_Q + (m + 1) * LANES], cosv, sinv)
        qt_ref[m * LANES:(m + 1) * LANES, :] = (blk * Q_SCALE).astype(BF16)
    for j, ref in enumerate((kc_ref, ks_ref, kw_ref)):
        blk = _rope_t(yt[TR_K + j * LANES:TR_K + (j + 1) * LANES], cosv, sinv)
        for c in range(nlb):
            ref[c * LANES:(c + 1) * LANES, :] = blk[:, c * LANES:(c + 1) * LANES].T.astype(ref.dtype)
    for j, ref in enumerate((vst_ref, vwt_ref)):
        blk = yt[TR_V + j * LANES:TR_V + (j + 1) * LANES].astype(BF16)
        for c in range(nlb):
            ref[c] = blk[:, c * LANES:(c + 1) * LANES]
    st = yt[TR_SMALL:TR_SMALL + N_SMALL]
    srow = lax.broadcasted_iota(jnp.int32, (N_SMALL, 1), 0)
    gdec_t = -jnp.exp(alogt_ref[...]) * _softplus(st + dtbt_ref[...])
    small_t = jnp.where(srow < SMALL_A, _sigmoid(st), gdec_t)
    smallt_ref[...] = small_t
    for c in range(nlb):
        gt_ref[c] = small_t[SMALL_B:N_SMALL, c * LANES:(c + 1) * LANES]

    cmpv_ref[...] = _dot(hb, w_ref[:, TM_VCMP:TM_VCMP + LANES])
    z_ref[...] = _dot(hb, w_ref[:, TM_Z:TM_Z + GDN_WIDTH]).astype(BF16)

    sm = _dot(hb, w_ref[:, TM_SMALL:TM_SMALL + LANES])
    lane1 = lax.broadcasted_iota(jnp.int32, (1, LANES), 1)
    gdec = -jnp.exp(alog_ref[...]) * _softplus(sm + dtb_ref[...])
    small_ref[...] = jnp.where(lane1 < SMALL_A, _sigmoid(sm), gdec)


def _in_proj(x2, pos_row, g1, w_t, w_r, invf, convw, alog, dtb, alogt, dtbt, seq_len):
    n = x2.shape[0]
    row = lambda i: (i, 0)
    colb = lambda i: (0, i)
    lead = lambda i: (i, 0, 0)
    const = lambda i: (0, 0)
    nlb = TM // LANES
    out_shapes = (
        jax.ShapeDtypeStruct((NSA_WIDTH, n), BF16),
        jax.ShapeDtypeStruct((n, LANES), F32),
        jax.ShapeDtypeStruct((n, LANES), BF16),
        jax.ShapeDtypeStruct((n, LANES), BF16),
        jax.ShapeDtypeStruct((n // LANES, LANES, LANES), BF16),
        jax.ShapeDtypeStruct((n // LANES, LANES, LANES), BF16),
        jax.ShapeDtypeStruct((n, LANES), F32),
        jax.ShapeDtypeStruct((n, GDN_WIDTH), BF16),
        jax.ShapeDtypeStruct((n, GDN_WIDTH), BF16),
        jax.ShapeDtypeStruct((n, GDN_WIDTH), BF16),
        jax.ShapeDtypeStruct((n, GDN_WIDTH), BF16),
        jax.ShapeDtypeStruct((n, LANES), F32),
        jax.ShapeDtypeStruct((N_SMALL, n), F32),
        jax.ShapeDtypeStruct((n // LANES, SUBLANES, LANES), F32),
    )
    out_specs = (
        pl.BlockSpec((NSA_WIDTH, TM), colb),
        pl.BlockSpec((TM, LANES), row), pl.BlockSpec((TM, LANES), row), pl.BlockSpec((TM, LANES), row),
        pl.BlockSpec((nlb, LANES, LANES), lead), pl.BlockSpec((nlb, LANES, LANES), lead),
        pl.BlockSpec((TM, LANES), row),
        pl.BlockSpec((TM, GDN_WIDTH), row), pl.BlockSpec((TM, GDN_WIDTH), row),
        pl.BlockSpec((TM, GDN_WIDTH), row), pl.BlockSpec((TM, GDN_WIDTH), row),
        pl.BlockSpec((TM, LANES), row),
        pl.BlockSpec((N_SMALL, TM), colb),
        pl.BlockSpec((nlb, SUBLANES, LANES), lead),
    )
    in_specs = [
        pl.BlockSpec((TM, D_MODEL), row), pl.BlockSpec((1, TM), colb), pl.BlockSpec((1, D_MODEL), const),
        pl.BlockSpec((TR_ROWS, D_MODEL), const), pl.BlockSpec((D_MODEL, TM_COLS), const),
        pl.BlockSpec((ROPE_DIM // 2, 1), const), pl.BlockSpec((GDN_CONV, 3 * GDN_WIDTH), const),
        pl.BlockSpec((1, LANES), const), pl.BlockSpec((1, LANES), const),
        pl.BlockSpec((N_SMALL, 1), const), pl.BlockSpec((N_SMALL, 1), const),
    ]
    return pl.pallas_call(
        functools.partial(_in_proj_kernel, seq_len // TM),
        grid=(n // TM,), in_specs=in_specs, out_specs=out_specs, out_shape=out_shapes,
        scratch_shapes=[pltpu.VMEM((TM + 2 * SUBLANES, 3 * GDN_WIDTH), F32)],
        compiler_params=pltpu.CompilerParams(dimension_semantics=("arbitrary",),
                                             vmem_limit_bytes=VMEM_LIMIT),
        name="in_proj",
    )(x2, pos_row, g1, w_t, w_r, invf, convw, alog, dtb, alogt, dtbt)


def _prep_in_proj_weights(w_in, gdn_a_log, gdn_dt_bias):
    q = w_in[:, :NSA_WIDTH].reshape(D_MODEL, 2, 4, HEAD_DIM)
    q = jnp.transpose(q, (0, 2, 1, 3)).reshape(D_MODEL, NSA_WIDTH)
    kv = w_in[:, OFF_KV:OFF_GATE].reshape(D_MODEL, 6, LANES)
    gate = w_in[:, OFF_GATE:OFF_GQKV].reshape(D_MODEL, 2, 4, 3)
    gate = jnp.transpose(gate, (0, 3, 2, 1)).reshape(D_MODEL, 24)
    small = jnp.concatenate([gate, w_in[:, OFF_B:OFF_A], w_in[:, OFF_A:OFF_A + GDN_HEADS]], axis=1)
    w_t = jnp.concatenate([q, kv[:, 0], kv[:, 2], kv[:, 4], kv[:, 3], kv[:, 5], small], axis=1).T.astype(BF16)
    w_r = jnp.concatenate([kv[:, 1], w_in[:, OFF_GQKV:OFF_Z], w_in[:, OFF_Z:OFF_B], small,
                           jnp.zeros((D_MODEL, LANES - N_SMALL), w_in.dtype)], axis=1).astype(BF16)
    alog = jnp.zeros((LANES,), F32).at[SMALL_A:SMALL_A + GDN_HEADS].set(gdn_a_log.astype(F32))
    dtb = jnp.zeros((LANES,), F32).at[SMALL_A:SMALL_A + GDN_HEADS].set(gdn_dt_bias.astype(F32))
    return w_t, w_r, alog[None, :], dtb[None, :], alog[:N_SMALL, None], dtb[:N_SMALL, None]


def _rope_inv_freq():
    half = ROPE_DIM // 2
    return jnp.power(ROPE_THETA, -jnp.arange(half, dtype=F32) * (2.0 / ROPE_DIM))[:, None]


def _compress_kernel(xk_ref, xv_ref, pos_ref, w1_ref, w2_ref, out_ref):
    nblk = xk_ref.shape[0] // CMP_STRIDE
    acc_lo = jnp.zeros((nblk, 4 * CMP_HIDDEN), F32)
    acc_hi = jnp.zeros((nblk, 4 * CMP_HIDDEN), F32)
    for j in range(CMP_STRIDE):
        xj = jnp.concatenate([xk_ref[pl.ds(j, nblk, stride=CMP_STRIDE), :],
                              xv_ref[pl.ds(j, nblk, stride=CMP_STRIDE), :]], axis=1)
        acc_lo = acc_lo + _dot((xj + pos_ref[j:j + 1, :]).astype(BF16), w1_ref[j])
        acc_hi = acc_hi + _dot((xj + pos_ref[CMP_STRIDE + j:CMP_STRIDE + j + 1, :]).astype(BF16),
                               w1_ref[CMP_STRIDE + j])
    pre = acc_lo + pltpu.roll(acc_hi, nblk - 1, 0)
    kvc = _dot(_silu(pre).astype(BF16), w2_ref[...])
    out_ref[0:nblk, :] = kvc[:, 0:LANES]
    out_ref[nblk:2 * nblk, :] = kvc[:, LANES:2 * LANES].T


def _compress(cmp_k, cmp_v, pos_rows, w1_bd, w2_bd, seq_len):
    n = cmp_k.shape[0]
    nb = n // seq_len
    nblk = seq_len // CMP_STRIDE
    assert nblk == LANES
    return pl.pallas_call(
        _compress_kernel,
        grid=(nb,),
        in_specs=[pl.BlockSpec((seq_len, LANES), lambda b: (b, 0)),
                  pl.BlockSpec((seq_len, LANES), lambda b: (b, 0)),
                  pl.BlockSpec((CMP_BLOCK, 256), lambda b: (0, 0)),
                  pl.BlockSpec((CMP_BLOCK, 256, 4 * CMP_HIDDEN), lambda b: (0, 0, 0)),
                  pl.BlockSpec((4 * CMP_HIDDEN, 256), lambda b: (0, 0))],
        out_specs=pl.BlockSpec((2 * nblk, LANES), lambda b: (b, 0)),
        out_shape=jax.ShapeDtypeStruct((nb * 2 * nblk, LANES), F32),
        compiler_params=pltpu.CompilerParams(dimension_semantics=("arbitrary",),
                                             vmem_limit_bytes=VMEM_LIMIT),
        name="nsa_compress",
    )(cmp_k, cmp_v, pos_rows, w1_bd, w2_bd)


def _prep_compress_weights(cmp_pos, cmp_w1, cmp_w2):
    slot_src = jnp.array([0, 0, 1, 1])
    eye = jnp.eye(4, dtype=BF16)
    w1 = cmp_w1.reshape(2, CMP_BLOCK, HEAD_DIM, CMP_HIDDEN).astype(BF16)[slot_src]
    w1_bd = jnp.transpose(w1, (1, 0, 2, 3))[:, :, :, None, :] * eye[None, :, None, :, None]
    w1_bd = w1_bd.reshape(CMP_BLOCK, 256, 4 * CMP_HIDDEN)
    w2_bd = cmp_w2.astype(BF16)[slot_src][:, :, None, :] * eye[:, None, :, None]
    w2_bd = w2_bd.reshape(4 * CMP_HIDDEN, 256)
    pos_rows = jnp.concatenate([cmp_pos[0], cmp_pos[0], cmp_pos[1], cmp_pos[1]], axis=-1).astype(F32)
    return pos_rows, w1_bd, w2_bd


TQ = 256
KC = 256
N_SEL = 32
ROWS = N_HEADS * TQ
WCHUNKS = WINDOW // KC + 1
ONES_ROWS = 16


def _nsa_kernel(qt_ref, ks_ref, kw_ref, vst_ref, vwt_ref, kvc_ref, gt_ref, ovt_ref, eblk_ref, o_ref,
                qaug, sbuf, mxbuf, ms_sc, mw_sc, accs_sc, accw_sc, out_sc):
    assert WCHUNKS == 3
    tile = pl.program_id(1)
    t0 = tile * TQ
    tcol = t0 + lax.broadcasted_iota(jnp.int32, (1, TQ), 1)
    krow = lax.broadcasted_iota(jnp.int32, (KC, 1), 0)
    ones = jnp.ones((ONES_ROWS, KC), BF16)
    groups = [slice(r * TQ, (r + 1) * TQ) for r in range(N_HEADS)]
    blocks_per_chunk = KC // LANES

    zhalf = jnp.zeros((HEAD_DIM, TQ), BF16)
    for m in range(4):
        blk = qt_ref[m * LANES:(m + 1) * LANES, :]
        qaug[0:LANES, groups[2 * m]] = jnp.concatenate([blk[0:HEAD_DIM], zhalf], axis=0)
        qaug[0:LANES, groups[2 * m + 1]] = jnp.concatenate([zhalf, blk[HEAD_DIM:LANES]], axis=0)

    def v_chunk(vt_ref, k0):
        b0 = k0 // LANES
        return jnp.concatenate(
            [jnp.concatenate([vt_ref[b0 + j] for j in range(blocks_per_chunk)], axis=1), ones], axis=0)

    buf_a, buf_b = sbuf.at[0], sbuf.at[1]
    mx_a, mx_b = mxbuf.at[0], mxbuf.at[1]

    def produce(buf, mx, kmat, bias):
        qrows = kmat.shape[1]
        for cols in groups:
            s = _dot(kmat, qaug[0:qrows, cols])
            if bias is not None:
                s = s + bias
            buf[:, cols] = s
            mx[:, cols] = jnp.max(s, axis=0, keepdims=True)

    def consume(buf, mx, vt, m_ref, acc_ref):
        m_old = [m_ref[:, cols] for cols in groups]
        m_new = [jnp.maximum(m_old[r], mx[:, cols]) for r, cols in enumerate(groups)]
        p = [jnp.exp2((buf[:, cols] - m_new[r]).astype(BF16)) for r, cols in enumerate(groups)]
        pv = [_dot(vt, y) for y in p]
        for r, cols in enumerate(groups):
            m_ref[:, cols] = m_new[r]
            acc_ref[:, cols] = acc_ref[:, cols] * jnp.exp2(m_old[r] - m_new[r]) + pv[r]

    mw_sc[...] = jnp.full(mw_sc.shape, NEG_INF, F32)
    accw_sc[...] = jnp.zeros(accw_sc.shape, F32)

    def window_chunk(j):
        start = t0 - WINDOW + j * KC
        k0 = pl.multiple_of(jnp.maximum(start, 0), KC)
        kpos = start + krow
        diff = tcol - kpos
        bias = jnp.where((kpos >= 0) & (diff >= 0) & (diff < WINDOW), 0.0, NEG_INF)
        return k0, kw_ref[pl.ds(k0, KC), :], bias

    w_k0 = [None] * WCHUNKS
    w_k0[2], kwin, wbias = window_chunk(2)
    produce(buf_a, mx_a, kwin, wbias)
    w_k0[1], kwin, wbias = window_chunk(1)
    produce(buf_b, mx_b, kwin, wbias)

    nblk = kvc_ref.shape[0] // 2
    kc = kvc_ref[0:nblk, :].astype(BF16)
    vct = kvc_ref[nblk:2 * nblk, :].astype(BF16)
    nrow = lax.broadcasted_iota(jnp.int32, (nblk, 1), 0)
    vbias = jnp.where(nrow * CMP_STRIDE + (CMP_BLOCK - 1) <= tcol, 0.0, NEG_INF)
    has_any = tcol >= CMP_BLOCK - 1
    s_c = [_dot(kc, qaug[0:LANES, cols]) + vbias for cols in groups]
    e_c = [jnp.exp2(y - jnp.max(y, axis=0, keepdims=True)) for y in s_c]
    p_c = [jnp.where(has_any, y * (1.0 / jnp.sum(y, axis=0, keepdims=True)), 0.0) for y in e_c]
    gates = gt_ref[...]
    for r, cols in enumerate(groups):
        out_sc[:, cols] = gates[r:r + 1] * _dot(vct, p_c[r].astype(BF16))

    consume(buf_a, mx_a, v_chunk(vwt_ref, w_k0[2]), mw_sc, accw_sc)
    w_k0[0], kwin, wbias = window_chunk(0)
    produce(buf_a, mx_a, kwin, wbias)

    jrow = lax.broadcasted_iota(jnp.int32, (N_SEL, 1), 0)
    cur = lax.shift_right_logical(tcol, 6)
    forced = (jrow == 0) | (jrow == cur) | (jrow == cur - 1)
    causal = jrow <= cur
    for g in range(N_GROUPS):
        psum = (p_c[g] + p_c[2 + g]) + (p_c[4 + g] + p_c[6 + g])
        imp_t = _dot3(ovt_ref[...], psum)
        score = jnp.where(forced, FORCE_SCORE, jnp.where(causal, imp_t[0:N_SEL, :], NEG_INF))
        cnt = jnp.zeros((N_SEL, TQ), jnp.int32)
        for jp in range(N_SEL):
            rowv = score[jp:jp + 1, :]
            beats = (rowv > score) | ((rowv == score) & (jrow > jp))
            cnt = cnt + jnp.where(beats, 1, 0)
        sel = (cnt < SEL_TOP_N) & causal
        bias = jnp.concatenate([jnp.where(sel, 0.0, MASK_BIAS),
                                jnp.zeros((LANES - N_SEL, TQ), F32)], axis=0).astype(BF16)
        for m in range(4):
            qaug[LANES:2 * LANES, groups[2 * m + g]] = bias

    ms_sc[...] = jnp.full(ms_sc.shape, NEG_INF, F32)
    accs_sc[...] = jnp.zeros(accs_sc.shape, F32)

    def sel_keys(c):
        k0 = pl.multiple_of(c * KC, KC)
        return jnp.concatenate([ks_ref[pl.ds(k0, KC), :], eblk_ref[pl.ds(k0, KC), :]], axis=1)

    def sel_consume(buf, mx, c):
        consume(buf, mx, v_chunk(vst_ref, c * KC), ms_sc, accs_sc)

    consume(buf_b, mx_b, v_chunk(vwt_ref, w_k0[1]), mw_sc, accw_sc)
    produce(buf_b, mx_b, sel_keys(tile), jnp.where(t0 + krow <= tcol, 0.0, NEG_INF))
    consume(buf_a, mx_a, v_chunk(vwt_ref, w_k0[0]), mw_sc, accw_sc)
    n_full = tile
    n_pairs = n_full // 2
    odd = n_full % 2 == 1

    def body(j, carry):
        produce(buf_a, mx_a, sel_keys(2 * j), None)
        sel_consume(buf_b, mx_b, jnp.where(j == 0, tile, 2 * j - 1))
        produce(buf_b, mx_b, sel_keys(2 * j + 1), None)
        sel_consume(buf_a, mx_a, 2 * j)
        return carry

    lax.fori_loop(0, n_pairs, body, 0)
    last_b = jnp.where(n_pairs == 0, tile, 2 * n_pairs - 1)

    @pl.when(odd)
    def _():
        produce(buf_a, mx_a, sel_keys(2 * n_pairs), None)

    sel_consume(buf_b, mx_b, last_b)

    @pl.when(odd)
    def _():
        sel_consume(buf_a, mx_a, 2 * n_pairs)

    for m in range(4):
        halves = []
        for g in range(N_GROUPS):
            r = 2 * m + g
            acc_s = accs_sc[:, groups[r]]
            acc_w = accw_sc[:, groups[r]]
            o_r = (out_sc[:, groups[r]]
                   + (gates[8 + r:9 + r] * (1.0 / acc_s[LANES:LANES + 1])) * acc_s[0:LANES]
                   + (gates[16 + r:17 + r] * (1.0 / acc_w[LANES:LANES + 1])) * acc_w[0:LANES])
            halves.append(o_r[g * HEAD_DIM:(g + 1) * HEAD_DIM])
        o_ref[:, m * LANES:(m + 1) * LANES] = jnp.concatenate(halves, axis=0).T.astype(BF16)


def _nsa(qt, ks, kw, vst, vwt, kvc, small_t, ovt, eblk, seq_len):
    n = ks.shape[0]
    nb = n // seq_len
    nq = seq_len // TQ
    nkb = seq_len // LANES
    seq = lambda b, i: (b, 0)
    return pl.pallas_call(
        _nsa_kernel,
        grid=(nb, nq),
        in_specs=[pl.BlockSpec((NSA_WIDTH, TQ), lambda b, i: (0, b * nq + i)),
                  pl.BlockSpec((seq_len, LANES), seq), pl.BlockSpec((seq_len, LANES), seq),
                  pl.BlockSpec((nkb, LANES, LANES), lambda b, i: (b, 0, 0)),
                  pl.BlockSpec((nkb, LANES, LANES), lambda b, i: (b, 0, 0)),
                  pl.BlockSpec((2 * LANES, LANES), seq),
                  pl.BlockSpec((N_SMALL, TQ), lambda b, i: (0, b * nq + i)),
                  pl.BlockSpec((LANES, LANES), lambda b, i: (0, 0)),
                  pl.BlockSpec((seq_len, LANES), lambda b, i: (0, 0))],
        out_specs=pl.BlockSpec((TQ, NSA_WIDTH), lambda b, i: (b * nq + i, 0)),
        out_shape=jax.ShapeDtypeStruct((n, NSA_WIDTH), BF16),
        scratch_shapes=[pltpu.VMEM((2 * LANES, ROWS), BF16),
                        pltpu.VMEM((2, KC, ROWS), F32),
                        pltpu.VMEM((2, 1, ROWS), F32),
                        pltpu.VMEM((1, ROWS), F32), pltpu.VMEM((1, ROWS), F32),
                        pltpu.VMEM((LANES + ONES_ROWS, ROWS), F32),
                        pltpu.VMEM((LANES + ONES_ROWS, ROWS), F32),
                        pltpu.VMEM((LANES, ROWS), F32)],
        compiler_params=pltpu.CompilerParams(dimension_semantics=("arbitrary", "arbitrary"),
                                             vmem_limit_bytes=VMEM_LIMIT),
        name="nsa_attention",
    )(qt, ks, kw, vst, vwt, kvc, small_t, ovt, eblk)


def _nsa_constants(seq_len):
    n_cmp = (seq_len - CMP_BLOCK) // CMP_STRIDE + 1
    s = jnp.arange(LANES)[:, None]
    nn = jnp.arange(LANES)[None, :]
    cs = nn * CMP_STRIDE
    ss = s * SEL_BLOCK
    ovt = (cs < ss + SEL_BLOCK) & (cs + CMP_BLOCK > ss) & (s < seq_len // SEL_BLOCK) & (nn < n_cmp)
    k = jnp.arange(seq_len)[:, None]
    eblk = (k // SEL_BLOCK) == jnp.arange(LANES)[None, :]
    return ovt.astype(BF16), eblk.astype(BF16)


PAIR = 2 * GDN_CHUNK
N_DOUBLINGS = 5
GT_G = 4


def _gdn_kernel(nseq, q_ref, k_ref, v_ref, z_ref, small_ref, gt_ref, ng_ref, o_ref, s_sc):
    seq_len = q_ref.shape[0] // nseq
    n_pairs = seq_len // PAIR
    units = [(s, h) for s in range(nseq) for h in range(GDN_HEADS)]
    us = range(len(units))
    cols = [slice(h * GDN_DIM, (h + 1) * GDN_DIM) for _, h in units]
    ri = lax.broadcasted_iota(jnp.int32, (PAIR, PAIR), 0)
    ci = lax.broadcasted_iota(jnp.int32, (PAIR, PAIR), 1)
    same = lax.shift_right_logical(ri, 6) == lax.shift_right_logical(ci, 6)
    incl = same & (ri >= ci)
    strict = same & (ri > ci)
    first_cols = ci < GDN_CHUNK
    first_row = ci[0:1, :] < GDN_CHUNK
    ltri = jnp.where(incl, 1.0, 0.0).astype(BF16)
    tot = [jnp.where(first_cols, 1.0, 0.0).astype(BF16),
           jnp.where(first_cols, 0.0, 1.0).astype(BF16)]
    s_sc[...] = jnp.zeros(s_sc.shape, F32)

    def body(pi, carry):
        rows_s = [pl.ds(pl.multiple_of(s * seq_len + pi * PAIR, PAIR), PAIR) for s in range(nseq)]
        sm_s = [small_ref[r, :] for r in rows_s]
        cs_s = [_dot3(ltri, y) for y in sm_s]
        gt_s = [gt_ref[s * n_pairs + pi] for s in range(nseq)]
        csr_s = [_dot3(ltri, y, nt=True) for y in gt_s]
        glast_s = [[_dot3(tot[c], y, nt=True) for c in range(2)] for y in gt_s]
        rows = [rows_s[s] for s, _ in units]
        q = [q_ref[rows[u], cols[u]] for u in us]
        k = [k_ref[rows[u], cols[u]] for u in us]
        kf = [y.astype(F32) for y in k]
        beta = [sm_s[s][:, SMALL_B + h:SMALL_B + h + 1] for s, h in units]
        gcc = [cs_s[s][:, SMALL_A + h:SMALL_A + h + 1] for s, h in units]
        gcr = [csr_s[s][GT_G + h:GT_G + h + 1, :] for s, h in units]
        glast = [[glast_s[s][c][GT_G + h:GT_G + h + 1, :] for c in range(2)] for s, h in units]
        decay = [jnp.exp(jnp.where(incl, gcc[u] - gcr[u], NEG_INF)) for u in us]
        egc = [jnp.exp(g) for g in gcc]
        kb = [kf[u] * beta[u] for u in us]
        kk = [_dot_nt(kb[u].astype(BF16), k[u]) for u in us]
        p = [jnp.where(strict, -(kk[u] * decay[u]), 0.0) for u in us]
        x = [jnp.concatenate([v_ref[rows[u], cols[u]].astype(F32) * beta[u], kb[u] * egc[u]], axis=1) for u in us]
        for i in range(N_DOUBLINGS + 1):
            pb = [y.astype(BF16) for y in p]
            x = [x[u] + _dot(pb[u], x[u].astype(BF16)) for u in us]
            if i < N_DOUBLINGS:
                p = [_dot(y, y) for y in pb]
        uw = [y.astype(BF16) for y in x]
        qk = [(_dot_nt(q[u], k[u]) * decay[u]).astype(BF16) for u in us]
        ke_t = [kf[u].T * jnp.exp(jnp.where(first_row, glast[u][0], glast[u][1]) - gcr[u]) for u in us]
        kw = [[_dot(jnp.where(first_cols, y, 0.0).astype(BF16), uw[u]) for u, y in enumerate(ke_t)],
              [_dot(jnp.where(first_cols, 0.0, y).astype(BF16), uw[u]) for u, y in enumerate(ke_t)]]
        qw = [_dot(qk[u], uw[u]) for u in us]
        qm = [(q[u].astype(F32) * egc[u] - qw[u][:, GDN_DIM:2 * GDN_DIM]).astype(BF16) for u in us]
        s = [s_sc[u] for u in us]
        os = []
        for c in range(2):
            half = slice(c * GDN_CHUNK, (c + 1) * GDN_CHUNK)
            sb = [y.astype(BF16) for y in s]
            os.append([_dot(qm[u][half], sb[u]) for u in us])
            ks = [_dot(kw[c][u][:, GDN_DIM:2 * GDN_DIM].astype(BF16), sb[u]) for u in us]
            s = [s[u] * jnp.exp(glast[u][c]) - ks[u] + kw[c][u][:, 0:GDN_DIM] for u in us]
        for u in us:
            s_sc[u] = s[u]
            o = jnp.concatenate([os[0][u], os[1][u]], axis=0) + qw[u][:, 0:GDN_DIM]
            o = o * lax.rsqrt(jnp.mean(o * o, axis=-1, keepdims=True) + RMS_EPS) * ng_ref[...]
            o_ref[rows[u], cols[u]] = (o * _silu(z_ref[rows[u], cols[u]].astype(F32))).astype(BF16)
        return carry

    lax.fori_loop(0, n_pairs, body, 0)


GDN_SEQS = 2


def _gdn(gq, gk, gv, z, small, gt, norm_g, seq_len):
    n = gq.shape[0]
    nb = n // seq_len
    nseq = GDN_SEQS if nb % GDN_SEQS == 0 else 1
    rows = nseq * seq_len
    seq = lambda b: (b, 0)
    return pl.pallas_call(
        functools.partial(_gdn_kernel, nseq),
        grid=(nb // nseq,),
        in_specs=[pl.BlockSpec((rows, GDN_WIDTH), seq), pl.BlockSpec((rows, GDN_WIDTH), seq),
                  pl.BlockSpec((rows, GDN_WIDTH), seq), pl.BlockSpec((rows, GDN_WIDTH), seq),
                  pl.BlockSpec((rows, LANES), seq),
                  pl.BlockSpec((rows // PAIR, SUBLANES, PAIR), lambda b: (b, 0, 0)),
                  pl.BlockSpec((1, GDN_DIM), lambda b: (0, 0))],
        out_specs=pl.BlockSpec((rows, GDN_WIDTH), seq),
        out_shape=jax.ShapeDtypeStruct((n, GDN_WIDTH), BF16),
        scratch_shapes=[pltpu.VMEM((nseq * GDN_HEADS, GDN_DIM, GDN_DIM), F32)],
        compiler_params=pltpu.CompilerParams(dimension_semantics=("arbitrary",),
                                             vmem_limit_bytes=VMEM_LIMIT),
        name="gated_delta_rule",
    )(gq, gk, gv, z, small, gt, norm_g)


FF_CHUNK = 256


def _rms(x, g):
    return x * lax.rsqrt(jnp.mean(x * x, axis=-1, keepdims=True) + RMS_EPS) * g


def _out_mlp_kernel(x_ref, on_ref, og_ref, gn_ref, wo_ref, g2_ref, wg_ref, wu_ref, wd_ref, gf_ref, out_ref):
    o_nsa = _rms(on_ref[...].astype(F32), gn_ref[...]).astype(BF16)
    mix = jnp.concatenate([o_nsa, og_ref[...]], axis=1)
    h = x_ref[...] + _dot(mix, wo_ref[...])
    hn = _rms(h, g2_ref[...]).astype(BF16)
    y = jnp.zeros_like(h)
    for c in range(D_FF // FF_CHUNK):
        cols = slice(c * FF_CHUNK, (c + 1) * FF_CHUNK)
        act = _silu(_dot(hn, wg_ref[:, cols])) * _dot(hn, wu_ref[:, cols])
        y = y + _dot(act.astype(BF16), wd_ref[cols, :])
    out_ref[...] = _rms(h + y, gf_ref[...])


def _out_mlp(x2, o_nsa, o_gdn, gn, wo, g2, wg, wu, wd, gf):
    n = x2.shape[0]
    row = lambda i: (i, 0)
    const = lambda i: (0, 0)
    resident = lambda shape: pl.BlockSpec(shape, const, pipeline_mode=pl.Buffered(1))
    return pl.pallas_call(
        _out_mlp_kernel,
        grid=(n // TM,),
        in_specs=[pl.BlockSpec((TM, D_MODEL), row), pl.BlockSpec((TM, NSA_WIDTH), row),
                  pl.BlockSpec((TM, GDN_WIDTH), row), pl.BlockSpec((1, NSA_WIDTH), const),
                  resident((D_MODEL, D_MODEL)), pl.BlockSpec((1, D_MODEL), const),
                  resident((D_MODEL, D_FF)), resident((D_MODEL, D_FF)), resident((D_FF, D_MODEL)),
                  pl.BlockSpec((1, D_MODEL), const)],
        out_specs=pl.BlockSpec((TM, D_MODEL), row),
        out_shape=jax.ShapeDtypeStruct((n, D_MODEL), F32),
        compiler_params=pltpu.CompilerParams(dimension_semantics=("arbitrary",),
                                             vmem_limit_bytes=VMEM_LIMIT),
        name="out_mlp",
    )(x2, o_nsa, o_gdn, gn, wo, g2, wg, wu, wd, gf)


def _nsa_head_perm():
    c = jnp.arange(NSA_WIDTH)
    return ((c // LANES) + 4 * ((c % LANES) // HEAD_DIM)) * HEAD_DIM + c % HEAD_DIM


def _layer(x2, pos_row, seq_len, norm1_g, w_in, cmp_pos, cmp_w1, cmp_w2, nsa_norm_g, gdn_conv_w, gdn_a_log,
           gdn_dt_bias, gdn_norm_g, w_out, norm2_g, w_gate, w_up, w_down, out_g):
    w_t, w_r, alog, dtb, alogt, dtbt = _prep_in_proj_weights(w_in, gdn_a_log, gdn_dt_bias)
    qt, cmp_k, ks, kw, vst, vwt, cmp_v, gq, gk, gv, z, small, small_t, gt = _in_proj(
        x2, pos_row, norm1_g[None, :].astype(F32), w_t, w_r, _rope_inv_freq(), gdn_conv_w.astype(F32),
        alog, dtb, alogt, dtbt, seq_len)
    kvc = _compress(cmp_k, cmp_v, *_prep_compress_weights(cmp_pos, cmp_w1, cmp_w2), seq_len)
    ovt, eblk = _nsa_constants(seq_len)
    o_nsa = _nsa(qt, ks, kw, vst, vwt, kvc, small_t, ovt, eblk, seq_len)
    o_gdn = _gdn(gq, gk, gv, z, small, gt, gdn_norm_g[None, :].astype(F32), seq_len)
    perm = _nsa_head_perm()
    wo = jnp.concatenate([w_out[:NSA_WIDTH][perm], w_out[NSA_WIDTH:]], axis=0).astype(BF16)
    return _out_mlp(x2, o_nsa, o_gdn, nsa_norm_g[perm][None, :].astype(F32), wo, norm2_g[None, :].astype(F32),
                    w_gate.astype(BF16), w_up.astype(BF16), w_down.astype(BF16), out_g[None, :].astype(F32))


def kernel(x, positions, norm1_g, w_in, cmp_pos, cmp_w1, cmp_w2, nsa_norm_g, gdn_conv_w, gdn_a_log,
           gdn_dt_bias, gdn_norm_g, w_out, norm2_g, w_gate, w_up, w_down, final_g):
    nb, seq_len, d = x.shape
    depth = w_in.shape[0]
    assert d == D_MODEL and seq_len % TM == 0 and seq_len // SEL_BLOCK == N_SEL and depth == 1
    x2 = x.reshape(nb * seq_len, d)
    pos_row = positions.reshape(1, nb * seq_len)
    out = _layer(x2, pos_row, seq_len, norm1_g[0], w_in[0], cmp_pos[0], cmp_w1[0], cmp_w2[0], nsa_norm_g[0],
                 gdn_conv_w[0], gdn_a_log[0], gdn_dt_bias[0], gdn_norm_g[0], w_out[0], norm2_g[0],
                 w_gate[0], w_up[0], w_down[0], final_g)
    return out.reshape(nb, seq_len, d)
```

```python
import functools
import math

import jax
import jax.numpy as jnp
from jax import lax
from jax.experimental import pallas as pl
from jax.experimental.pallas import tpu as pltpu

F32 = jnp.float32
BF16 = jnp.bfloat16

LANES = 128
SUBLANES = 8

D_MODEL = 1024
N_HEADS = 8
N_GROUPS = 2
HEAD_DIM = 64
CMP_BLOCK = 32
CMP_STRIDE = 16
CMP_HIDDEN = 128
SEL_BLOCK = 64
SEL_TOP_N = 8
WINDOW = 512
ROPE_THETA = 500000.0
ROPE_DIM = 16
GDN_HEADS = 4
GDN_DIM = 128
GDN_CONV = 4
GDN_CHUNK = 64
NSA_WIDTH = 512
GDN_WIDTH = 512
D_FF = 2816
RMS_EPS = 1e-6
NEG_INF = -1e30
MASK_BIAS = -1e9
FORCE_SCORE = 1e9

OFF_KV = 512
OFF_GATE = OFF_KV + 6 * 128
OFF_GQKV = OFF_GATE + 24
OFF_Z = OFF_GQKV + 3 * GDN_WIDTH
OFF_B = OFF_Z + GDN_WIDTH
OFF_A = OFF_B + GDN_HEADS

TR_Q = 0
TR_K = 512
TR_V = 896
TR_SMALL = 1152
TR_ROWS = 1184
TM_VCMP = 0
TM_GDN = 128
TM_Z = 1664
TM_SMALL = 2176
TM_COLS = 2304
SMALL_B = 24
SMALL_A = 28
N_SMALL = 32

Q_SCALE = HEAD_DIM ** -0.5 * math.log2(math.e)
TM = 512
VMEM_LIMIT = 56 * 1024 * 1024


def _dot(a, b):
    return jnp.dot(a, b, preferred_element_type=F32)


def _dot_nt(a, b):
    return lax.dot_general(a, b, (((1,), (1,)), ((), ())), preferred_element_type=F32)


def _sigmoid(x):
    return 1.0 / (1.0 + jnp.exp(-x))


def _silu(x):
    return x * _sigmoid(x)


def _softplus(x):
    return jnp.maximum(x, 0.0) + jnp.log(1.0 + jnp.exp(-jnp.abs(x)))


def _split3(x):
    h1 = x.astype(BF16)
    r1 = x - h1.astype(F32)
    h2 = r1.astype(BF16)
    h3 = (r1 - h2.astype(F32)).astype(BF16)
    return h1, h2, h3


def _dot3(mat01, x, nt=False):
    f = (lambda a: _dot_nt(a, mat01)) if nt else (lambda a: _dot(mat01, a))
    h1, h2, h3 = _split3(x)
    return f(h1) + f(h2) + f(h3)


def _rope_t(blk, cosv, sinv):
    half = ROPE_DIM // 2
    parts = []
    for h in range(blk.shape[0] // HEAD_DIM):
        b = h * HEAD_DIM
        x0 = blk[b:b + half]
        x1 = blk[b + half:b + ROPE_DIM]
        parts += [x0 * cosv - x1 * sinv, x1 * cosv + x0 * sinv, blk[b + ROPE_DIM:b + HEAD_DIM]]
    return jnp.concatenate(parts, axis=0)


def _in_proj_kernel(tiles_per_seq, x_ref, pos_ref, g1_ref, wt_ref, w_ref, invf_ref, convw_ref,
                    alog_ref, dtb_ref, alogt_ref, dtbt_ref,
                    qt_ref, kc_ref, ks_ref, kw_ref, vst_ref, vwt_ref, cmpv_ref,
                    gq_ref, gk_ref, gv_ref, z_ref, small_ref, smallt_ref, gt_ref,
                    cbuf):
    tm = x_ref.shape[0]
    nlb = tm // LANES
    x = x_ref[...]
    hn = x * lax.rsqrt(jnp.mean(x * x, axis=-1, keepdims=True) + RMS_EPS) * g1_ref[...]
    hb = hn.astype(BF16)

    first = pl.program_id(0) % tiles_per_seq == 0

    @pl.when(first)
    def _():
        cbuf[0:SUBLANES, :] = jnp.zeros((SUBLANES, 3 * GDN_WIDTH), F32)

    proj = {}

    def gdn_piece(c):
        lo = TM_GDN + 2 * c * LANES
        cbuf[SUBLANES:SUBLANES + tm, 2 * c * LANES:2 * (c + 1) * LANES] = _dot(hb, w_ref[:, lo:lo + 2 * LANES])

    def t_piece(name, lo, hi):
        proj[name] = _dot_nt(wt_ref[lo:hi, :], hb)

    def tm_piece(name, lo, hi):
        proj[name] = _dot(hb, w_ref[:, lo:hi])

    later = [lambda: gdn_piece(2), lambda: t_piece("q01", TR_Q, TR_Q + 2 * LANES),
             lambda: gdn_piece(3), lambda: t_piece("q23", TR_Q + 2 * LANES, TR_K),
             lambda: gdn_piece(4), lambda: t_piece("k", TR_K, TR_V),
             lambda: gdn_piece(5), lambda: t_piece("v", TR_V, TR_ROWS),
             lambda: tm_piece("z0", TM_Z, TM_Z + 2 * LANES), lambda: tm_piece("z1", TM_Z + 2 * LANES, TM_SMALL),
             lambda: tm_piece("vcmp", TM_VCMP, TM_VCMP + LANES), lambda: tm_piece("small", TM_SMALL, TM_COLS)]
    gdn_piece(0)
    gdn_piece(1)

    ones_sq = jnp.ones((LANES, LANES), BF16)

    def conv_block(cb):
        cols = slice(cb * LANES, (cb + 1) * LANES)
        zx = cbuf[0:SUBLANES + tm, cols]
        y = zx * convw_ref[GDN_CONV - 1:GDN_CONV, cols]
        for s in range(1, GDN_CONV):
            y = y + pltpu.roll(zx, s, 0) * convw_ref[GDN_CONV - 1 - s:GDN_CONV - s, cols]
        return _silu(y[SUBLANES:])

    def finish_block(cb, y):
        cols = slice((cb % 4) * LANES, (cb % 4 + 1) * LANES)
        if cb < 8:
            y = y * lax.rsqrt(_dot((y * y).astype(BF16), ones_sq) + RMS_EPS)
        if cb < 4:
            gq_ref[:, cols] = (y * (GDN_DIM ** -0.5)).astype(BF16)
        elif cb < 8:
            gk_ref[:, cols] = y.astype(BF16)
        else:
            gv_ref[:, cols] = y.astype(BF16)

    pending = None
    for cb in range(12):
        later[cb]()
        y = conv_block(cb)
        if pending is not None:
            finish_block(*pending)
        pending = (cb, y)
    finish_block(*pending)
    cbuf[0:SUBLANES, :] = cbuf[tm:tm + SUBLANES, :]

    yt = jnp.concatenate([proj["q01"], proj["q23"], proj["k"], proj["v"]], axis=0)
    ang = invf_ref[...] * pos_ref[...].astype(F32)
    cosv = jnp.cos(ang)
    sinv = jnp.sin(ang)
    for m in range(4):
        blk = _rope_t(yt[TR_Q + m * LANES:TR_Q + (m + 1) * LANES], cosv, sinv)
        qt_ref[m * LANES:(m + 1) * LANES, :] = (blk * Q_SCALE).astype(BF16)
    for j, ref in enumerate((kc_ref, ks_ref, kw_ref)):
        blk = _rope_t(yt[TR_K + j * LANES:TR_K + (j + 1) * LANES], cosv, sinv)
        for c in range(nlb):
            ref[c * LANES:(c + 1) * LANES, :] = blk[:, c * LANES:(c + 1) * LANES].T.astype(ref.dtype)
    for j, ref in enumerate((vst_ref, vwt_ref)):
        blk = yt[TR_V + j * LANES:TR_V + (j + 1) * LANES].astype(BF16)
        for c in range(nlb):
            ref[c] = blk[:, c * LANES:(c + 1) * LANES]
    st = yt[TR_SMALL:TR_SMALL + N_SMALL]
    srow = lax.broadcasted_iota(jnp.int32, (N_SMALL, 1), 0)
    gdec_t = -jnp.exp(alogt_ref[...]) * _softplus(st + dtbt_ref[...])
    small_t = jnp.where(srow < SMALL_A, _sigmoid(st), gdec_t)
    smallt_ref[...] = small_t
    for c in range(nlb):
        gt_ref[c] = small_t[SMALL_B:N_SMALL, c * LANES:(c + 1) * LANES]

    cmpv_ref[...] = proj["vcmp"]
    z_ref[:, 0:2 * LANES] = proj["z0"].astype(BF16)
    z_ref[:, 2 * LANES:GDN_WIDTH] = proj["z1"].astype(BF16)

    sm = proj["small"]
    lane1 = lax.broadcasted_iota(jnp.int32, (1, LANES), 1)
    gdec = -jnp.exp(alog_ref[...]) * _softplus(sm + dtb_ref[...])
    small_ref[...] = jnp.where(lane1 < SMALL_A, _sigmoid(sm), gdec)


def _in_proj(x2, pos_row, g1, w_t, w_r, invf, convw, alog, dtb, alogt, dtbt, seq_len):
    n = x2.shape[0]
    row = lambda i: (i, 0)
    colb = lambda i: (0, i)
    lead = lambda i: (i, 0, 0)
    const = lambda i: (0, 0)
    nlb = TM // LANES
    out_shapes = (
        jax.ShapeDtypeStruct((NSA_WIDTH, n), BF16),
        jax.ShapeDtypeStruct((n, LANES), F32),
        jax.ShapeDtypeStruct((n, LANES), BF16),
        jax.ShapeDtypeStruct((n, LANES), BF16),
        jax.ShapeDtypeStruct((n // LANES, LANES, LANES), BF16),
        jax.ShapeDtypeStruct((n // LANES, LANES, LANES), BF16),
        jax.ShapeDtypeStruct((n, LANES), F32),
        jax.ShapeDtypeStruct((n, GDN_WIDTH), BF16),
        jax.ShapeDtypeStruct((n, GDN_WIDTH), BF16),
        jax.ShapeDtypeStruct((n, GDN_WIDTH), BF16),
        jax.ShapeDtypeStruct((n, GDN_WIDTH), BF16),
        jax.ShapeDtypeStruct((n, LANES), F32),
        jax.ShapeDtypeStruct((N_SMALL, n), F32),
        jax.ShapeDtypeStruct((n // LANES, SUBLANES, LANES), F32),
    )
    out_specs = (
        pl.BlockSpec((NSA_WIDTH, TM), colb),
        pl.BlockSpec((TM, LANES), row), pl.BlockSpec((TM, LANES), row), pl.BlockSpec((TM, LANES), row),
        pl.BlockSpec((nlb, LANES, LANES), lead), pl.BlockSpec((nlb, LANES, LANES), lead),
        pl.BlockSpec((TM, LANES), row),
        pl.BlockSpec((TM, GDN_WIDTH), row), pl.BlockSpec((TM, GDN_WIDTH), row),
        pl.BlockSpec((TM, GDN_WIDTH), row), pl.BlockSpec((TM, GDN_WIDTH), row),
        pl.BlockSpec((TM, LANES), row),
        pl.BlockSpec((N_SMALL, TM), colb),
        pl.BlockSpec((nlb, SUBLANES, LANES), lead),
    )
    in_specs = [
        pl.BlockSpec((TM, D_MODEL), row), pl.BlockSpec((1, TM), colb), pl.BlockSpec((1, D_MODEL), const),
        pl.BlockSpec((TR_ROWS, D_MODEL), const), pl.BlockSpec((D_MODEL, TM_COLS), const),
        pl.BlockSpec((ROPE_DIM // 2, 1), const), pl.BlockSpec((GDN_CONV, 3 * GDN_WIDTH), const),
        pl.BlockSpec((1, LANES), const), pl.BlockSpec((1, LANES), const),
        pl.BlockSpec((N_SMALL, 1), const), pl.BlockSpec((N_SMALL, 1), const),
    ]
    return pl.pallas_call(
        functools.partial(_in_proj_kernel, seq_len // TM),
        grid=(n // TM,), in_specs=in_specs, out_specs=out_specs, out_shape=out_shapes,
        scratch_shapes=[pltpu.VMEM((TM + 2 * SUBLANES, 3 * GDN_WIDTH), F32)],
        compiler_params=pltpu.CompilerParams(dimension_semantics=("arbitrary",),
                                             vmem_limit_bytes=VMEM_LIMIT),
        name="in_proj",
    )(x2, pos_row, g1, w_t, w_r, invf, convw, alog, dtb, alogt, dtbt)


def _prep_in_proj_weights(w_in, gdn_a_log, gdn_dt_bias):
    q = w_in[:, :NSA_WIDTH].reshape(D_MODEL, 2, 4, HEAD_DIM)
    q = jnp.transpose(q, (0, 2, 1, 3)).reshape(D_MODEL, NSA_WIDTH)
    kv = w_in[:, OFF_KV:OFF_GATE].reshape(D_MODEL, 6, LANES)
    gate = w_in[:, OFF_GATE:OFF_GQKV].reshape(D_MODEL, 2, 4, 3)
    gate = jnp.transpose(gate, (0, 3, 2, 1)).reshape(D_MODEL, 24)
    small = jnp.concatenate([gate, w_in[:, OFF_B:OFF_A], w_in[:, OFF_A:OFF_A + GDN_HEADS]], axis=1)
    w_t = jnp.concatenate([q, kv[:, 0], kv[:, 2], kv[:, 4], kv[:, 3], kv[:, 5], small], axis=1).T.astype(BF16)
    w_r = jnp.concatenate([kv[:, 1], w_in[:, OFF_GQKV:OFF_Z], w_in[:, OFF_Z:OFF_B], small,
                           jnp.zeros((D_MODEL, LANES - N_SMALL), w_in.dtype)], axis=1).astype(BF16)
    alog = jnp.zeros((LANES,), F32).at[SMALL_A:SMALL_A + GDN_HEADS].set(gdn_a_log.astype(F32))
    dtb = jnp.zeros((LANES,), F32).at[SMALL_A:SMALL_A + GDN_HEADS].set(gdn_dt_bias.astype(F32))
    return w_t, w_r, alog[None, :], dtb[None, :], alog[:N_SMALL, None], dtb[:N_SMALL, None]


def _rope_inv_freq():
    half = ROPE_DIM // 2
    return jnp.power(ROPE_THETA, -jnp.arange(half, dtype=F32) * (2.0 / ROPE_DIM))[:, None]


def _compress_kernel(xk_ref, xv_ref, pos_ref, w1_ref, w2_ref, out_ref):
    nblk = xk_ref.shape[0] // CMP_STRIDE
    acc_lo = jnp.zeros((nblk, 4 * CMP_HIDDEN), F32)
    acc_hi = jnp.zeros((nblk, 4 * CMP_HIDDEN), F32)
    for j in range(CMP_STRIDE):
        xj = jnp.concatenate([xk_ref[pl.ds(j, nblk, stride=CMP_STRIDE), :],
                              xv_ref[pl.ds(j, nblk, stride=CMP_STRIDE), :]], axis=1)
        acc_lo = acc_lo + _dot((xj + pos_ref[j:j + 1, :]).astype(BF16), w1_ref[j])
        acc_hi = acc_hi + _dot((xj + pos_ref[CMP_STRIDE + j:CMP_STRIDE + j + 1, :]).astype(BF16),
                               w1_ref[CMP_STRIDE + j])
    pre = acc_lo + pltpu.roll(acc_hi, nblk - 1, 0)
    kvc = _dot(_silu(pre).astype(BF16), w2_ref[...])
    out_ref[0:nblk, :] = kvc[:, 0:LANES]
    out_ref[nblk:2 * nblk, :] = kvc[:, LANES:2 * LANES].T


def _compress(cmp_k, cmp_v, pos_rows, w1_bd, w2_bd, seq_len):
    n = cmp_k.shape[0]
    nb = n // seq_len
    nblk = seq_len // CMP_STRIDE
    assert nblk == LANES
    return pl.pallas_call(
        _compress_kernel,
        grid=(nb,),
        in_specs=[pl.BlockSpec((seq_len, LANES), lambda b: (b, 0)),
                  pl.BlockSpec((seq_len, LANES), lambda b: (b, 0)),
                  pl.BlockSpec((CMP_BLOCK, 256), lambda b: (0, 0)),
                  pl.BlockSpec((CMP_BLOCK, 256, 4 * CMP_HIDDEN), lambda b: (0, 0, 0)),
                  pl.BlockSpec((4 * CMP_HIDDEN, 256), lambda b: (0, 0))],
        out_specs=pl.BlockSpec((2 * nblk, LANES), lambda b: (b, 0)),
        out_shape=jax.ShapeDtypeStruct((nb * 2 * nblk, LANES), F32),
        compiler_params=pltpu.CompilerParams(dimension_semantics=("arbitrary",),
                                             vmem_limit_bytes=VMEM_LIMIT),
        name="nsa_compress",
    )(cmp_k, cmp_v, pos_rows, w1_bd, w2_bd)


def _prep_compress_weights(cmp_pos, cmp_w1, cmp_w2):
    slot_src = jnp.array([0, 0, 1, 1])
    eye = jnp.eye(4, dtype=BF16)
    w1 = cmp_w1.reshape(2, CMP_BLOCK, HEAD_DIM, CMP_HIDDEN).astype(BF16)[slot_src]
    w1_bd = jnp.transpose(w1, (1, 0, 2, 3))[:, :, :, None, :] * eye[None, :, None, :, None]
    w1_bd = w1_bd.reshape(CMP_BLOCK, 256, 4 * CMP_HIDDEN)
    w2_bd = cmp_w2.astype(BF16)[slot_src][:, :, None, :] * eye[:, None, :, None]
    w2_bd = w2_bd.reshape(4 * CMP_HIDDEN, 256)
    pos_rows = jnp.concatenate([cmp_pos[0], cmp_pos[0], cmp_pos[1], cmp_pos[1]], axis=-1).astype(F32)
    return pos_rows, w1_bd, w2_bd


TQ = 256
KC = 256
N_SEL = 32
ROWS = N_HEADS * TQ
WCHUNKS = WINDOW // KC + 1
ONES_ROWS = 16


def _nsa_kernel(qt_ref, ks_ref, kw_ref, vst_ref, vwt_ref, kvc_ref, gt_ref, ovt_ref, eblk_ref, o_ref,
                qaug, sbuf, mxbuf, ms_sc, mw_sc, accs_sc, accw_sc, out_sc):
    assert WCHUNKS == 3
    tile = pl.program_id(1)
    t0 = tile * TQ
    tcol = t0 + lax.broadcasted_iota(jnp.int32, (1, TQ), 1)
    krow = lax.broadcasted_iota(jnp.int32, (KC, 1), 0)
    ones = jnp.ones((ONES_ROWS, KC), BF16)
    groups = [slice(r * TQ, (r + 1) * TQ) for r in range(N_HEADS)]
    blocks_per_chunk = KC // LANES

    zhalf = jnp.zeros((HEAD_DIM, TQ), BF16)
    for m in range(4):
        blk = qt_ref[m * LANES:(m + 1) * LANES, :]
        qaug[0:LANES, groups[2 * m]] = jnp.concatenate([blk[0:HEAD_DIM], zhalf], axis=0)
        qaug[0:LANES, groups[2 * m + 1]] = jnp.concatenate([zhalf, blk[HEAD_DIM:LANES]], axis=0)

    def v_chunk(vt_ref, k0):
        b0 = k0 // LANES
        return jnp.concatenate(
            [jnp.concatenate([vt_ref[b0 + j] for j in range(blocks_per_chunk)], axis=1), ones], axis=0)

    buf_a, buf_b = sbuf.at[0], sbuf.at[1]
    mx_a, mx_b = mxbuf.at[0], mxbuf.at[1]

    def produce(buf, mx, kmat, bias):
        qrows = kmat.shape[1]
        for cols in groups:
            s = _dot(kmat, qaug[0:qrows, cols])
            if bias is not None:
                s = s + bias
            buf[:, cols] = s
            mx[:, cols] = jnp.max(s, axis=0, keepdims=True)

    def consume(buf, mx, vt, m_ref, acc_ref):
        m_old = [m_ref[:, cols] for cols in groups]
        m_new = [jnp.maximum(m_old[r], mx[:, cols]) for r, cols in enumerate(groups)]
        p = [jnp.exp2((buf[:, cols] - m_new[r]).astype(BF16)) for r, cols in enumerate(groups)]
        pv = [_dot(vt, y) for y in p]
        for r, cols in enumerate(groups):
            m_ref[:, cols] = m_new[r]
            acc_ref[:, cols] = acc_ref[:, cols] * jnp.exp2(m_old[r] - m_new[r]) + pv[r]

    mw_sc[...] = jnp.full(mw_sc.shape, NEG_INF, F32)
    accw_sc[...] = jnp.zeros(accw_sc.shape, F32)

    def window_chunk(j):
        start = t0 - WINDOW + j * KC
        k0 = pl.multiple_of(jnp.maximum(start, 0), KC)
        kpos = start + krow
        diff = tcol - kpos
        bias = jnp.where((kpos >= 0) & (diff >= 0) & (diff < WINDOW), 0.0, NEG_INF)
        return k0, kw_ref[pl.ds(k0, KC), :], bias

    w_k0 = [None] * WCHUNKS
    w_k0[2], kwin, wbias = window_chunk(2)
    produce(buf_a, mx_a, kwin, wbias)
    w_k0[1], kwin, wbias = window_chunk(1)
    produce(buf_b, mx_b, kwin, wbias)

    nblk = kvc_ref.shape[0] // 2
    kc = kvc_ref[0:nblk, :].astype(BF16)
    vct = kvc_ref[nblk:2 * nblk, :].astype(BF16)
    nrow = lax.broadcasted_iota(jnp.int32, (nblk, 1), 0)
    vbias = jnp.where(nrow * CMP_STRIDE + (CMP_BLOCK - 1) <= tcol, 0.0, NEG_INF)
    has_any = tcol >= CMP_BLOCK - 1
    s_c = [_dot(kc, qaug[0:LANES, cols]) + vbias for cols in groups]
    e_c = [jnp.exp2(y - jnp.max(y, axis=0, keepdims=True)) for y in s_c]
    p_c = [jnp.where(has_any, y * (1.0 / jnp.sum(y, axis=0, keepdims=True)), 0.0) for y in e_c]
    gates = gt_ref[...]
    for r, cols in enumerate(groups):
        out_sc[:, cols] = gates[r:r + 1] * _dot(vct, p_c[r].astype(BF16))

    consume(buf_a, mx_a, v_chunk(vwt_ref, w_k0[2]), mw_sc, accw_sc)
    w_k0[0], kwin, wbias = window_chunk(0)
    produce(buf_a, mx_a, kwin, wbias)

    jrow = lax.broadcasted_iota(jnp.int32, (N_SEL, 1), 0)
    cur = lax.shift_right_logical(tcol, 6)
    forced = (jrow == 0) | (jrow == cur) | (jrow == cur - 1)
    causal = jrow <= cur
    for g in range(N_GROUPS):
        psum = (p_c[g] + p_c[2 + g]) + (p_c[4 + g] + p_c[6 + g])
        imp_t = _dot3(ovt_ref[...], psum)
        score = jnp.where(forced, FORCE_SCORE, jnp.where(causal, imp_t[0:N_SEL, :], NEG_INF))
        cnt = jnp.zeros((N_SEL, TQ), jnp.int32)
        for jp in range(N_SEL):
            rowv = score[jp:jp + 1, :]
            beats = (rowv > score) | ((rowv == score) & (jrow > jp))
            cnt = cnt + jnp.where(beats, 1, 0)
        sel = (cnt < SEL_TOP_N) & causal
        bias = jnp.concatenate([jnp.where(sel, 0.0, MASK_BIAS),
                                jnp.zeros((LANES - N_SEL, TQ), F32)], axis=0).astype(BF16)
        for m in range(4):
            qaug[LANES:2 * LANES, groups[2 * m + g]] = bias

    ms_sc[...] = jnp.full(ms_sc.shape, NEG_INF, F32)
    accs_sc[...] = jnp.zeros(accs_sc.shape, F32)

    def sel_keys(c):
        k0 = pl.multiple_of(c * KC, KC)
        return jnp.concatenate([ks_ref[pl.ds(k0, KC), :], eblk_ref[pl.ds(k0, KC), :]], axis=1)

    def sel_consume(buf, mx, c):
        consume(buf, mx, v_chunk(vst_ref, c * KC), ms_sc, accs_sc)

    consume(buf_b, mx_b, v_chunk(vwt_ref, w_k0[1]), mw_sc, accw_sc)
    produce(buf_b, mx_b, sel_keys(tile), jnp.where(t0 + krow <= tcol, 0.0, NEG_INF))
    consume(buf_a, mx_a, v_chunk(vwt_ref, w_k0[0]), mw_sc, accw_sc)
    n_full = tile
    n_pairs = n_full // 2
    odd = n_full % 2 == 1

    def body(j, carry):
        produce(buf_a, mx_a, sel_keys(2 * j), None)
        sel_consume(buf_b, mx_b, jnp.where(j == 0, tile, 2 * j - 1))
        produce(buf_b, mx_b, sel_keys(2 * j + 1), None)
        sel_consume(buf_a, mx_a, 2 * j)
        return carry

    lax.fori_loop(0, n_pairs, body, 0)
    last_b = jnp.where(n_pairs == 0, tile, 2 * n_pairs - 1)

    @pl.when(odd)
    def _():
        produce(buf_a, mx_a, sel_keys(2 * n_pairs), None)

    sel_consume(buf_b, mx_b, last_b)

    @pl.when(odd)
    def _():
        sel_consume(buf_a, mx_a, 2 * n_pairs)

    for m in range(4):
        halves = []
        for g in range(N_GROUPS):
            r = 2 * m + g
            acc_s = accs_sc[:, groups[r]]
            acc_w = accw_sc[:, groups[r]]
            o_r = (out_sc[:, groups[r]]
                   + (gates[8 + r:9 + r] * (1.0 / acc_s[LANES:LANES + 1])) * acc_s[0:LANES]
                   + (gates[16 + r:17 + r] * (1.0 / acc_w[LANES:LANES + 1])) * acc_w[0:LANES])
            halves.append(o_r[g * HEAD_DIM:(g + 1) * HEAD_DIM])
        o_ref[:, m * LANES:(m + 1) * LANES] = jnp.concatenate(halves, axis=0).T.astype(BF16)


def _nsa(qt, ks, kw, vst, vwt, kvc, small_t, ovt, eblk, seq_len):
    n = ks.shape[0]
    nb = n // seq_len
    nq = seq_len // TQ
    nkb = seq_len // LANES
    seq = lambda b, i: (b, 0)
    return pl.pallas_call(
        _nsa_kernel,
        grid=(nb, nq),
        in_specs=[pl.BlockSpec((NSA_WIDTH, TQ), lambda b, i: (0, b * nq + i)),
                  pl.BlockSpec((seq_len, LANES), seq), pl.BlockSpec((seq_len, LANES), seq),
                  pl.BlockSpec((nkb, LANES, LANES), lambda b, i: (b, 0, 0)),
                  pl.BlockSpec((nkb, LANES, LANES), lambda b, i: (b, 0, 0)),
                  pl.BlockSpec((2 * LANES, LANES), seq),
                  pl.BlockSpec((N_SMALL, TQ), lambda b, i: (0, b * nq + i)),
                  pl.BlockSpec((LANES, LANES), lambda b, i: (0, 0)),
                  pl.BlockSpec((seq_len, LANES), lambda b, i: (0, 0))],
        out_specs=pl.BlockSpec((TQ, NSA_WIDTH), lambda b, i: (b * nq + i, 0)),
        out_shape=jax.ShapeDtypeStruct((n, NSA_WIDTH), BF16),
        scratch_shapes=[pltpu.VMEM((2 * LANES, ROWS), BF16),
                        pltpu.VMEM((2, KC, ROWS), F32),
                        pltpu.VMEM((2, 1, ROWS), F32),
                        pltpu.VMEM((1, ROWS), F32), pltpu.VMEM((1, ROWS), F32),
                        pltpu.VMEM((LANES + ONES_ROWS, ROWS), F32),
                        pltpu.VMEM((LANES + ONES_ROWS, ROWS), F32),
                        pltpu.VMEM((LANES, ROWS), F32)],
        compiler_params=pltpu.CompilerParams(dimension_semantics=("arbitrary", "arbitrary"),
                                             vmem_limit_bytes=VMEM_LIMIT),
        name="nsa_attention",
    )(qt, ks, kw, vst, vwt, kvc, small_t, ovt, eblk)


def _nsa_constants(seq_len):
    n_cmp = (seq_len - CMP_BLOCK) // CMP_STRIDE + 1
    s = jnp.arange(LANES)[:, None]
    nn = jnp.arange(LANES)[None, :]
    cs = nn * CMP_STRIDE
    ss = s * SEL_BLOCK
    ovt = (cs < ss + SEL_BLOCK) & (cs + CMP_BLOCK > ss) & (s < seq_len // SEL_BLOCK) & (nn < n_cmp)
    k = jnp.arange(seq_len)[:, None]
    eblk = (k // SEL_BLOCK) == jnp.arange(LANES)[None, :]
    return ovt.astype(BF16), eblk.astype(BF16)


PAIR = 2 * GDN_CHUNK
N_DOUBLINGS = 5
GT_G = 4


def _gdn_kernel(q_ref, k_ref, v_ref, z_ref, small_ref, gt_ref, ng_ref, o_ref, s_sc):
    nseq = q_ref.shape[0]
    units = [(s, h) for s in range(nseq) for h in range(GDN_HEADS)]
    us = range(len(units))
    cols = [slice(h * GDN_DIM, (h + 1) * GDN_DIM) for _, h in units]
    ri = lax.broadcasted_iota(jnp.int32, (PAIR, PAIR), 0)
    ci = lax.broadcasted_iota(jnp.int32, (PAIR, PAIR), 1)
    same = lax.shift_right_logical(ri, 6) == lax.shift_right_logical(ci, 6)
    incl = same & (ri >= ci)
    strict = same & (ri > ci)
    first_cols = ci < GDN_CHUNK
    first_row = ci[0:1, :] < GDN_CHUNK
    ltri = jnp.where(incl, 1.0, 0.0).astype(BF16)
    tot = [jnp.where(first_cols, 1.0, 0.0).astype(BF16),
           jnp.where(first_cols, 0.0, 1.0).astype(BF16)]

    @pl.when(pl.program_id(1) == 0)
    def _():
        s_sc[...] = jnp.zeros(s_sc.shape, F32)

    sm_s = [small_ref[s] for s in range(nseq)]
    cs_s = [_dot3(ltri, y) for y in sm_s]
    gt_s = [gt_ref[s, 0] for s in range(nseq)]
    csr_s = [_dot3(ltri, y, nt=True) for y in gt_s]
    glast_s = [[_dot3(tot[c], y, nt=True) for c in range(2)] for y in gt_s]
    seq = [s for s, _ in units]
    q = [q_ref[seq[u], :, cols[u]] for u in us]
    k = [k_ref[seq[u], :, cols[u]] for u in us]
    kf = [y.astype(F32) for y in k]
    beta = [sm_s[s][:, SMALL_B + h:SMALL_B + h + 1] for s, h in units]
    gcc = [cs_s[s][:, SMALL_A + h:SMALL_A + h + 1] for s, h in units]
    gcr = [csr_s[s][GT_G + h:GT_G + h + 1, :] for s, h in units]
    glast = [[glast_s[s][c][GT_G + h:GT_G + h + 1, :] for c in range(2)] for s, h in units]
    decay = [jnp.exp(jnp.where(incl, gcc[u] - gcr[u], NEG_INF)) for u in us]
    egc = [jnp.exp(g) for g in gcc]
    kb = [kf[u] * beta[u] for u in us]
    kk = [_dot_nt(kb[u].astype(BF16), k[u]) for u in us]
    p = [jnp.where(strict, -(kk[u] * decay[u]), 0.0) for u in us]
    x = [jnp.concatenate([v_ref[seq[u], :, cols[u]].astype(F32) * beta[u], kb[u] * egc[u]], axis=1) for u in us]
    for i in range(N_DOUBLINGS + 1):
        pb = [y.astype(BF16) for y in p]
        x = [x[u] + _dot(pb[u], x[u].astype(BF16)) for u in us]
        if i < N_DOUBLINGS:
            p = [_dot(y, y) for y in pb]
    uw = [y.astype(BF16) for y in x]
    qk = [(_dot_nt(q[u], k[u]) * decay[u]).astype(BF16) for u in us]
    ke_t = [kf[u].T * jnp.exp(jnp.where(first_row, glast[u][0], glast[u][1]) - gcr[u]) for u in us]
    kw = [[_dot(jnp.where(first_cols, y, 0.0).astype(BF16), uw[u]) for u, y in enumerate(ke_t)],
          [_dot(jnp.where(first_cols, 0.0, y).astype(BF16), uw[u]) for u, y in enumerate(ke_t)]]
    qw = [_dot(qk[u], uw[u]) for u in us]
    qm = [(q[u].astype(F32) * egc[u] - qw[u][:, GDN_DIM:2 * GDN_DIM]).astype(BF16) for u in us]
    s = [s_sc[u] for u in us]
    os = []
    for c in range(2):
        half = slice(c * GDN_CHUNK, (c + 1) * GDN_CHUNK)
        sb = [y.astype(BF16) for y in s]
        os.append([_dot(qm[u][half], sb[u]) for u in us])
        ks = [_dot(kw[c][u][:, GDN_DIM:2 * GDN_DIM].astype(BF16), sb[u]) for u in us]
        s = [s[u] * jnp.exp(glast[u][c]) - ks[u] + kw[c][u][:, 0:GDN_DIM] for u in us]
    for u in us:
        s_sc[u] = s[u]
        o = jnp.concatenate([os[0][u], os[1][u]], axis=0) + qw[u][:, 0:GDN_DIM]
        o = o * lax.rsqrt(jnp.mean(o * o, axis=-1, keepdims=True) + RMS_EPS) * ng_ref[...]
        o_ref[seq[u], :, cols[u]] = (o * _silu(z_ref[seq[u], :, cols[u]].astype(F32))).astype(BF16)


GDN_SEQS = 4


def _gdn(gq, gk, gv, z, small, gt, norm_g, seq_len):
    n = gq.shape[0]
    nb = n // seq_len
    n_pairs = seq_len // PAIR
    nseq = max(d for d in range(1, GDN_SEQS + 1) if nb % d == 0)
    by_seq = lambda a: a.reshape(nb, seq_len, a.shape[-1])
    blk = lambda w: pl.BlockSpec((nseq, PAIR, w), lambda b, i: (b, i, 0))
    out = pl.pallas_call(
        _gdn_kernel,
        grid=(nb // nseq, n_pairs),
        in_specs=[blk(GDN_WIDTH), blk(GDN_WIDTH), blk(GDN_WIDTH), blk(GDN_WIDTH), blk(LANES),
                  pl.BlockSpec((nseq, 1, SUBLANES, PAIR), lambda b, i: (b, i, 0, 0)),
                  pl.BlockSpec((1, GDN_DIM), lambda b, i: (0, 0))],
        out_specs=blk(GDN_WIDTH),
        out_shape=jax.ShapeDtypeStruct((nb, seq_len, GDN_WIDTH), BF16),
        scratch_shapes=[pltpu.VMEM((nseq * GDN_HEADS, GDN_DIM, GDN_DIM), F32)],
        compiler_params=pltpu.CompilerParams(dimension_semantics=("arbitrary", "arbitrary"),
                                             vmem_limit_bytes=VMEM_LIMIT),
        name="gated_delta_rule",
    )(by_seq(gq), by_seq(gk), by_seq(gv), by_seq(z), by_seq(small),
      gt.reshape(nb, n_pairs, SUBLANES, PAIR), norm_g)
    return out.reshape(n, GDN_WIDTH)


FF_CHUNK = 256


def _rms(x, g):
    return x * lax.rsqrt(jnp.mean(x * x, axis=-1, keepdims=True) + RMS_EPS) * g


def _out_mlp_kernel(x_ref, on_ref, og_ref, gn_ref, wo_ref, g2_ref, wg_ref, wu_ref, wd_ref, gf_ref, out_ref):
    o_nsa = _rms(on_ref[...].astype(F32), gn_ref[...]).astype(BF16)
    mix = jnp.concatenate([o_nsa, og_ref[...]], axis=1)
    h = x_ref[...] + _dot(mix, wo_ref[...])
    hn = _rms(h, g2_ref[...]).astype(BF16)
    y = jnp.zeros_like(h)
    for c in range(D_FF // FF_CHUNK):
        cols = slice(c * FF_CHUNK, (c + 1) * FF_CHUNK)
        act = _silu(_dot(hn, wg_ref[:, cols])) * _dot(hn, wu_ref[:, cols])
        y = y + _dot(act.astype(BF16), wd_ref[cols, :])
    out_ref[...] = _rms(h + y, gf_ref[...])


def _out_mlp(x2, o_nsa, o_gdn, gn, wo, g2, wg, wu, wd, gf):
    n = x2.shape[0]
    row = lambda i: (i, 0)
    const = lambda i: (0, 0)
    resident = lambda shape: pl.BlockSpec(shape, const, pipeline_mode=pl.Buffered(1))
    return pl.pallas_call(
        _out_mlp_kernel,
        grid=(n // TM,),
        in_specs=[pl.BlockSpec((TM, D_MODEL), row), pl.BlockSpec((TM, NSA_WIDTH), row),
                  pl.BlockSpec((TM, GDN_WIDTH), row), pl.BlockSpec((1, NSA_WIDTH), const),
                  resident((D_MODEL, D_MODEL)), pl.BlockSpec((1, D_MODEL), const),
                  resident((D_MODEL, D_FF)), resident((D_MODEL, D_FF)), resident((D_FF, D_MODEL)),
                  pl.BlockSpec((1, D_MODEL), const)],
        out_specs=pl.BlockSpec((TM, D_MODEL), row),
        out_shape=jax.ShapeDtypeStruct((n, D_MODEL), F32),
        compiler_params=pltpu.CompilerParams(dimension_semantics=("arbitrary",),
                                             vmem_limit_bytes=VMEM_LIMIT),
        name="out_mlp",
    )(x2, o_nsa, o_gdn, gn, wo, g2, wg, wu, wd, gf)


def _nsa_head_perm():
    c = jnp.arange(NSA_WIDTH)
    return ((c // LANES) + 4 * ((c % LANES) // HEAD_DIM)) * HEAD_DIM + c % HEAD_DIM


def _layer(x2, pos_row, seq_len, norm1_g, w_in, cmp_pos, cmp_w1, cmp_w2, nsa_norm_g, gdn_conv_w, gdn_a_log,
           gdn_dt_bias, gdn_norm_g, w_out, norm2_g, w_gate, w_up, w_down, out_g):
    w_t, w_r, alog, dtb, alogt, dtbt = _prep_in_proj_weights(w_in, gdn_a_log, gdn_dt_bias)
    qt, cmp_k, ks, kw, vst, vwt, cmp_v, gq, gk, gv, z, small, small_t, gt = _in_proj(
        x2, pos_row, norm1_g[None, :].astype(F32), w_t, w_r, _rope_inv_freq(), gdn_conv_w.astype(F32),
        alog, dtb, alogt, dtbt, seq_len)
    kvc = _compress(cmp_k, cmp_v, *_prep_compress_weights(cmp_pos, cmp_w1, cmp_w2), seq_len)
    ovt, eblk = _nsa_constants(seq_len)
    o_nsa = _nsa(qt, ks, kw, vst, vwt, kvc, small_t, ovt, eblk, seq_len)
    o_gdn = _gdn(gq, gk, gv, z, small, gt, gdn_norm_g[None, :].astype(F32), seq_len)
    perm = _nsa_head_perm()
    wo = jnp.concatenate([w_out[:NSA_WIDTH][perm], w_out[NSA_WIDTH:]], axis=0).astype(BF16)
    return _out_mlp(x2, o_nsa, o_gdn, nsa_norm_g[perm][None, :].astype(F32), wo, norm2_g[None, :].astype(F32),
                    w_gate.astype(BF16), w_up.astype(BF16), w_down.astype(BF16), out_g[None, :].astype(F32))


def kernel(x, positions, norm1_g, w_in, cmp_pos, cmp_w1, cmp_w2, nsa_norm_g, gdn_conv_w, gdn_a_log,
           gdn_dt_bias, gdn_norm_g, w_out, norm2_g, w_gate, w_up, w_down, final_g):
    nb, seq_len, d = x.shape
    depth = w_in.shape[0]
    assert d == D_MODEL and seq_len % TM == 0 and seq_len // SEL_BLOCK == N_SEL and depth == 1
    x2 = x.reshape(nb * seq_len, d)
    pos_row = positions.reshape(1, nb * seq_len)
    out = _layer(x2, pos_row, seq_len, norm1_g[0], w_in[0], cmp_pos[0], cmp_w1[0], cmp_w2[0], nsa_norm_g[0],
                 gdn_conv_w[0], gdn_a_log[0], gdn_dt_bias[0], gdn_norm_g[0], w_out[0], norm2_g[0],
                 w_gate[0], w_up[0], w_down[0], final_g)
    return out.reshape(nb, seq_len, d)
```

```python
import functools
import math

import jax
import jax.numpy as jnp
from jax import lax
from jax.experimental import pallas as pl
from jax.experimental.pallas import tpu as pltpu

F32 = jnp.float32
BF16 = jnp.bfloat16

LANES = 128
SUBLANES = 8

D_MODEL = 1024
N_HEADS = 8
N_GROUPS = 2
HEAD_DIM = 64
CMP_BLOCK = 32
CMP_STRIDE = 16
CMP_HIDDEN = 128
SEL_BLOCK = 64
SEL_TOP_N = 8
WINDOW = 512
ROPE_THETA = 500000.0
ROPE_DIM = 16
GDN_HEADS = 4
GDN_DIM = 128
GDN_CONV = 4
GDN_CHUNK = 64
NSA_WIDTH = 512
GDN_WIDTH = 512
D_FF = 2816
RMS_EPS = 1e-6
NEG_INF = -1e30
MASK_BIAS = -1e9
FORCE_SCORE = 1e9

OFF_KV = 512
OFF_GATE = OFF_KV + 6 * 128
OFF_GQKV = OFF_GATE + 24
OFF_Z = OFF_GQKV + 3 * GDN_WIDTH
OFF_B = OFF_Z + GDN_WIDTH
OFF_A = OFF_B + GDN_HEADS

TR_Q = 0
TR_K = 512
TR_V = 896
TR_SMALL = 1152
TR_ROWS = 1184
TM_VCMP = 0
TM_GDN = 128
TM_Z = 1664
TM_SMALL = 2176
TM_COLS = 2304
SMALL_B = 24
SMALL_A = 28
N_SMALL = 32

Q_SCALE = HEAD_DIM ** -0.5 * math.log2(math.e)
TM = 512
VMEM_LIMIT = 56 * 1024 * 1024


def _dot(a, b):
    return jnp.dot(a, b, preferred_element_type=F32)


def _dot_nt(a, b):
    return lax.dot_general(a, b, (((1,), (1,)), ((), ())), preferred_element_type=F32)


def _sigmoid(x):
    return 1.0 / (1.0 + jnp.exp(-x))


def _silu(x):
    return x * _sigmoid(x)


def _softplus(x):
    return jnp.maximum(x, 0.0) + jnp.log(1.0 + jnp.exp(-jnp.abs(x)))


def _split3(x):
    h1 = x.astype(BF16)
    r1 = x - h1.astype(F32)
    h2 = r1.astype(BF16)
    h3 = (r1 - h2.astype(F32)).astype(BF16)
    return h1, h2, h3


def _dot3(mat01, x, nt=False):
    f = (lambda a: _dot_nt(a, mat01)) if nt else (lambda a: _dot(mat01, a))
    h1, h2, h3 = _split3(x)
    return f(h1) + f(h2) + f(h3)


def _rope_t(blk, cosv, sinv):
    half = ROPE_DIM // 2
    parts = []
    for h in range(blk.shape[0] // HEAD_DIM):
        b = h * HEAD_DIM
        x0 = blk[b:b + half]
        x1 = blk[b + half:b + ROPE_DIM]
        parts += [x0 * cosv - x1 * sinv, x1 * cosv + x0 * sinv, blk[b + ROPE_DIM:b + HEAD_DIM]]
    return jnp.concatenate(parts, axis=0)


def _in_proj_kernel(tiles_per_seq, x_ref, pos_ref, g1_ref, wt_ref, w_ref, invf_ref, convw_ref,
                    alog_ref, dtb_ref, alogt_ref, dtbt_ref,
                    qt_ref, kc_ref, ks_ref, kw_ref, vst_ref, vwt_ref, cmpv_ref,
                    gq_ref, gk_ref, gv_ref, z_ref, small_ref, smallt_ref, gt_ref,
                    cbuf):
    tm = x_ref.shape[0]
    nlb = tm // LANES
    x = x_ref[...]
    hn = x * lax.rsqrt(jnp.mean(x * x, axis=-1, keepdims=True) + RMS_EPS) * g1_ref[...]
    hb = hn.astype(BF16)

    first = pl.program_id(0) % tiles_per_seq == 0

    @pl.when(first)
    def _():
        cbuf[0:SUBLANES, :] = jnp.zeros((SUBLANES, 3 * GDN_WIDTH), F32)

    proj = {}

    def gdn_piece(c):
        lo = TM_GDN + 2 * c * LANES
        cbuf[SUBLANES:SUBLANES + tm, 2 * c * LANES:2 * (c + 1) * LANES] = _dot(hb, w_ref[:, lo:lo + 2 * LANES])

    def t_piece(name, lo, hi):
        proj[name] = _dot_nt(wt_ref[lo:hi, :], hb)

    def tm_piece(name, lo, hi):
        proj[name] = _dot(hb, w_ref[:, lo:hi])

    later = [lambda: gdn_piece(2), lambda: t_piece("q01", TR_Q, TR_Q + 2 * LANES),
             lambda: gdn_piece(3), lambda: t_piece("q23", TR_Q + 2 * LANES, TR_K),
             lambda: gdn_piece(4), lambda: t_piece("k", TR_K, TR_V),
             lambda: gdn_piece(5), lambda: t_piece("v", TR_V, TR_ROWS),
             lambda: tm_piece("z0", TM_Z, TM_Z + 2 * LANES), lambda: tm_piece("z1", TM_Z + 2 * LANES, TM_SMALL),
             lambda: tm_piece("vcmp", TM_VCMP, TM_VCMP + LANES), lambda: tm_piece("small", TM_SMALL, TM_COLS)]
    gdn_piece(0)
    gdn_piece(1)

    ones_sq = jnp.ones((LANES, LANES), BF16)

    def conv_block(cb):
        cols = slice(cb * LANES, (cb + 1) * LANES)
        zx = cbuf[0:SUBLANES + tm, cols]
        y = zx * convw_ref[GDN_CONV - 1:GDN_CONV, cols]
        for s in range(1, GDN_CONV):
            y = y + pltpu.roll(zx, s, 0) * convw_ref[GDN_CONV - 1 - s:GDN_CONV - s, cols]
        return _silu(y[SUBLANES:])

    def finish_block(cb, y):
        cols = slice((cb % 4) * LANES, (cb % 4 + 1) * LANES)
        if cb < 8:
            y = y * lax.rsqrt(_dot((y * y).astype(BF16), ones_sq) + RMS_EPS)
        if cb < 4:
            gq_ref[:, cols] = (y * (GDN_DIM ** -0.5)).astype(BF16)
        elif cb < 8:
            gk_ref[:, cols] = y.astype(BF16)
        else:
            gv_ref[:, cols] = y.astype(BF16)

    pending = None
    for cb in range(12):
        later[cb]()
        y = conv_block(cb)
        if pending is not None:
            finish_block(*pending)
        pending = (cb, y)
    finish_block(*pending)
    cbuf[0:SUBLANES, :] = cbuf[tm:tm + SUBLANES, :]

    yt = jnp.concatenate([proj["q01"], proj["q23"], proj["k"], proj["v"]], axis=0)
    ang = invf_ref[...] * pos_ref[...].astype(F32)
    cosv = jnp.cos(ang)
    sinv = jnp.sin(ang)
    for m in range(4):
        blk = _rope_t(yt[TR_Q + m * LANES:TR_Q + (m + 1) * LANES], cosv, sinv)
        qt_ref[m * LANES:(m + 1) * LANES, :] = (blk * Q_SCALE).astype(BF16)
    for j, ref in enumerate((kc_ref, ks_ref, kw_ref)):
        blk = _rope_t(yt[TR_K + j * LANES:TR_K + (j + 1) * LANES], cosv, sinv)
        for c in range(nlb):
            ref[c * LANES:(c + 1) * LANES, :] = blk[:, c * LANES:(c + 1) * LANES].T.astype(ref.dtype)
    for j, ref in enumerate((vst_ref, vwt_ref)):
        blk = yt[TR_V + j * LANES:TR_V + (j + 1) * LANES].astype(BF16)
        for c in range(nlb):
            ref[c] = blk[:, c * LANES:(c + 1) * LANES]
    st = yt[TR_SMALL:TR_SMALL + N_SMALL]
    srow = lax.broadcasted_iota(jnp.int32, (N_SMALL, 1), 0)
    gdec_t = -jnp.exp(alogt_ref[...]) * _softplus(st + dtbt_ref[...])
    small_t = jnp.where(srow < SMALL_A, _sigmoid(st), gdec_t)
    smallt_ref[...] = small_t
    for c in range(nlb):
        gt_ref[c] = small_t[SMALL_B:N_SMALL, c * LANES:(c + 1) * LANES]

    cmpv_ref[...] = proj["vcmp"]
    z_ref[:, 0:2 * LANES] = proj["z0"].astype(BF16)
    z_ref[:, 2 * LANES:GDN_WIDTH] = proj["z1"].astype(BF16)

    sm = proj["small"]
    lane1 = lax.broadcasted_iota(jnp.int32, (1, LANES), 1)
    gdec = -jnp.exp(alog_ref[...]) * _softplus(sm + dtb_ref[...])
    small_ref[...] = jnp.where(lane1 < SMALL_A, _sigmoid(sm), gdec)


def _in_proj(x2, pos_row, g1, w_t, w_r, invf, convw, alog, dtb, alogt, dtbt, seq_len):
    n = x2.shape[0]
    row = lambda i: (i, 0)
    colb = lambda i: (0, i)
    lead = lambda i: (i, 0, 0)
    const = lambda i: (0, 0)
    nlb = TM // LANES
    out_shapes = (
        jax.ShapeDtypeStruct((NSA_WIDTH, n), BF16),
        jax.ShapeDtypeStruct((n, LANES), F32),
        jax.ShapeDtypeStruct((n, LANES), BF16),
        jax.ShapeDtypeStruct((n, LANES), BF16),
        jax.ShapeDtypeStruct((n // LANES, LANES, LANES), BF16),
        jax.ShapeDtypeStruct((n // LANES, LANES, LANES), BF16),
        jax.ShapeDtypeStruct((n, LANES), F32),
        jax.ShapeDtypeStruct((n, GDN_WIDTH), BF16),
        jax.ShapeDtypeStruct((n, GDN_WIDTH), BF16),
        jax.ShapeDtypeStruct((n, GDN_WIDTH), BF16),
        jax.ShapeDtypeStruct((n, GDN_WIDTH), BF16),
        jax.ShapeDtypeStruct((n, LANES), F32),
        jax.ShapeDtypeStruct((N_SMALL, n), F32),
        jax.ShapeDtypeStruct((n // LANES, SUBLANES, LANES), F32),
    )
    out_specs = (
        pl.BlockSpec((NSA_WIDTH, TM), colb),
        pl.BlockSpec((TM, LANES), row), pl.BlockSpec((TM, LANES), row), pl.BlockSpec((TM, LANES), row),
        pl.BlockSpec((nlb, LANES, LANES), lead), pl.BlockSpec((nlb, LANES, LANES), lead),
        pl.BlockSpec((TM, LANES), row),
        pl.BlockSpec((TM, GDN_WIDTH), row), pl.BlockSpec((TM, GDN_WIDTH), row),
        pl.BlockSpec((TM, GDN_WIDTH), row), pl.BlockSpec((TM, GDN_WIDTH), row),
        pl.BlockSpec((TM, LANES), row),
        pl.BlockSpec((N_SMALL, TM), colb),
        pl.BlockSpec((nlb, SUBLANES, LANES), lead),
    )
    in_specs = [
        pl.BlockSpec((TM, D_MODEL), row), pl.BlockSpec((1, TM), colb), pl.BlockSpec((1, D_MODEL), const),
        pl.BlockSpec((TR_ROWS, D_MODEL), const), pl.BlockSpec((D_MODEL, TM_COLS), const),
        pl.BlockSpec((ROPE_DIM // 2, 1), const), pl.BlockSpec((GDN_CONV, 3 * GDN_WIDTH), const),
        pl.BlockSpec((1, LANES), const), pl.BlockSpec((1, LANES), const),
        pl.BlockSpec((N_SMALL, 1), const), pl.BlockSpec((N_SMALL, 1), const),
    ]
    return pl.pallas_call(
        functools.partial(_in_proj_kernel, seq_len // TM),
        grid=(n // TM,), in_specs=in_specs, out_specs=out_specs, out_shape=out_shapes,
        scratch_shapes=[pltpu.VMEM((TM + 2 * SUBLANES, 3 * GDN_WIDTH), F32)],
        compiler_params=pltpu.CompilerParams(dimension_semantics=("arbitrary",),
                                             vmem_limit_bytes=VMEM_LIMIT),
        name="in_proj",
    )(x2, pos_row, g1, w_t, w_r, invf, convw, alog, dtb, alogt, dtbt)


def _prep_in_proj_weights(w_in, gdn_a_log, gdn_dt_bias):
    q = w_in[:, :NSA_WIDTH].reshape(D_MODEL, 2, 4, HEAD_DIM)
    q = jnp.transpose(q, (0, 2, 1, 3)).reshape(D_MODEL, NSA_WIDTH)
    kv = w_in[:, OFF_KV:OFF_GATE].reshape(D_MODEL, 6, LANES)
    gate = w_in[:, OFF_GATE:OFF_GQKV].reshape(D_MODEL, 2, 4, 3)
    gate = jnp.transpose(gate, (0, 3, 2, 1)).reshape(D_MODEL, 24)
    small = jnp.concatenate([gate, w_in[:, OFF_B:OFF_A], w_in[:, OFF_A:OFF_A + GDN_HEADS]], axis=1)
    w_t = jnp.concatenate([q, kv[:, 0], kv[:, 2], kv[:, 4], kv[:, 3], kv[:, 5], small], axis=1).T.astype(BF16)
    w_r = jnp.concatenate([kv[:, 1], w_in[:, OFF_GQKV:OFF_Z], w_in[:, OFF_Z:OFF_B], small,
                           jnp.zeros((D_MODEL, LANES - N_SMALL), w_in.dtype)], axis=1).astype(BF16)
    alog = jnp.zeros((LANES,), F32).at[SMALL_A:SMALL_A + GDN_HEADS].set(gdn_a_log.astype(F32))
    dtb = jnp.zeros((LANES,), F32).at[SMALL_A:SMALL_A + GDN_HEADS].set(gdn_dt_bias.astype(F32))
    return w_t, w_r, alog[None, :], dtb[None, :], alog[:N_SMALL, None], dtb[:N_SMALL, None]


def _rope_inv_freq():
    half = ROPE_DIM // 2
    return jnp.power(ROPE_THETA, -jnp.arange(half, dtype=F32) * (2.0 / ROPE_DIM))[:, None]


def _compress_kernel(xk_ref, xv_ref, pos_ref, w1_ref, w2_ref, out_ref):
    nblk = xk_ref.shape[0] // CMP_STRIDE
    acc_lo = jnp.zeros((nblk, 4 * CMP_HIDDEN), F32)
    acc_hi = jnp.zeros((nblk, 4 * CMP_HIDDEN), F32)
    for j in range(CMP_STRIDE):
        xj = jnp.concatenate([xk_ref[pl.ds(j, nblk, stride=CMP_STRIDE), :],
                              xv_ref[pl.ds(j, nblk, stride=CMP_STRIDE), :]], axis=1)
        acc_lo = acc_lo + _dot((xj + pos_ref[j:j + 1, :]).astype(BF16), w1_ref[j])
        acc_hi = acc_hi + _dot((xj + pos_ref[CMP_STRIDE + j:CMP_STRIDE + j + 1, :]).astype(BF16),
                               w1_ref[CMP_STRIDE + j])
    pre = acc_lo + pltpu.roll(acc_hi, nblk - 1, 0)
    kvc = _dot(_silu(pre).astype(BF16), w2_ref[...])
    out_ref[0:nblk, :] = kvc[:, 0:LANES]
    out_ref[nblk:2 * nblk, :] = kvc[:, LANES:2 * LANES].T


def _compress(cmp_k, cmp_v, pos_rows, w1_bd, w2_bd, seq_len):
    n = cmp_k.shape[0]
    nb = n // seq_len
    nblk = seq_len // CMP_STRIDE
    assert nblk == LANES
    return pl.pallas_call(
        _compress_kernel,
        grid=(nb,),
        in_specs=[pl.BlockSpec((seq_len, LANES), lambda b: (b, 0)),
                  pl.BlockSpec((seq_len, LANES), lambda b: (b, 0)),
                  pl.BlockSpec((CMP_BLOCK, 256), lambda b: (0, 0)),
                  pl.BlockSpec((CMP_BLOCK, 256, 4 * CMP_HIDDEN), lambda b: (0, 0, 0)),
                  pl.BlockSpec((4 * CMP_HIDDEN, 256), lambda b: (0, 0))],
        out_specs=pl.BlockSpec((2 * nblk, LANES), lambda b: (b, 0)),
        out_shape=jax.ShapeDtypeStruct((nb * 2 * nblk, LANES), F32),
        compiler_params=pltpu.CompilerParams(dimension_semantics=("arbitrary",),
                                             vmem_limit_bytes=VMEM_LIMIT),
        name="nsa_compress",
    )(cmp_k, cmp_v, pos_rows, w1_bd, w2_bd)


def _prep_compress_weights(cmp_pos, cmp_w1, cmp_w2):
    slot_src = jnp.array([0, 0, 1, 1])
    eye = jnp.eye(4, dtype=BF16)
    w1 = cmp_w1.reshape(2, CMP_BLOCK, HEAD_DIM, CMP_HIDDEN).astype(BF16)[slot_src]
    w1 = jnp.transpose(w1, (1, 0, 2, 3)).reshape(CMP_BLOCK, 4 * HEAD_DIM, CMP_HIDDEN)
    diag = jnp.repeat(jnp.repeat(eye, HEAD_DIM, axis=0), CMP_HIDDEN, axis=1)
    w1_bd = jnp.tile(w1, (1, 1, 4)) * diag[None]
    w2_bd = cmp_w2.astype(BF16)[slot_src][:, :, None, :] * eye[:, None, :, None]
    w2_bd = w2_bd.reshape(4 * CMP_HIDDEN, 256)
    pos_rows = jnp.concatenate([cmp_pos[0], cmp_pos[0], cmp_pos[1], cmp_pos[1]], axis=-1).astype(F32)
    return pos_rows, w1_bd, w2_bd


TQ = 256
KC = 256
N_SEL = 32
ROWS = N_HEADS * TQ
WCHUNKS = WINDOW // KC + 1
ONES_ROWS = 16


def _nsa_kernel(qt_ref, ks_ref, kw_ref, vst_ref, vwt_ref, kvc_ref, gt_ref, ovt_ref, eblk_ref, o_ref,
                qaug, sbuf, mxbuf, ms_sc, mw_sc, accs_sc, accw_sc, out_sc):
    assert WCHUNKS == 3
    tile = pl.program_id(1)
    t0 = tile * TQ
    tcol = t0 + lax.broadcasted_iota(jnp.int32, (1, TQ), 1)
    krow = lax.broadcasted_iota(jnp.int32, (KC, 1), 0)
    ones = jnp.ones((ONES_ROWS, KC), BF16)
    groups = [slice(r * TQ, (r + 1) * TQ) for r in range(N_HEADS)]
    blocks_per_chunk = KC // LANES

    zhalf = jnp.zeros((HEAD_DIM, TQ), BF16)
    for m in range(4):
        blk = qt_ref[m * LANES:(m + 1) * LANES, :]
        qaug[0:LANES, groups[2 * m]] = jnp.concatenate([blk[0:HEAD_DIM], zhalf], axis=0)
        qaug[0:LANES, groups[2 * m + 1]] = jnp.concatenate([zhalf, blk[HEAD_DIM:LANES]], axis=0)

    def v_chunk(vt_ref, k0):
        b0 = k0 // LANES
        vt = jnp.concatenate([vt_ref[b0 + j] for j in range(blocks_per_chunk)], axis=1)
        return [jnp.concatenate([vt[g * HEAD_DIM:(g + 1) * HEAD_DIM], ones], axis=0) for g in range(N_GROUPS)]

    buf_a, buf_b = sbuf.at[0], sbuf.at[1]
    mx_a, mx_b = mxbuf.at[0], mxbuf.at[1]

    def produce(buf, mx, kmat, bias):
        qrows = kmat.shape[1]
        for cols in groups:
            s = _dot(kmat, qaug[0:qrows, cols])
            if bias is not None:
                s = s + bias
            buf[:, cols] = s
            mx[:, cols] = jnp.max(s, axis=0, keepdims=True)

    def consume(buf, mx, vt, m_ref, acc_ref):
        for r, cols in enumerate(groups):
            m_old = m_ref[:, cols]
            m_new = jnp.maximum(m_old, mx[:, cols])
            p = jnp.exp2((buf[:, cols] - m_new).astype(BF16))
            m_ref[:, cols] = m_new
            acc_ref[:, cols] = acc_ref[:, cols] * jnp.exp2(m_old - m_new) + _dot(vt[r % N_GROUPS], p)

    mw_sc[...] = jnp.full(mw_sc.shape, NEG_INF, F32)
    accw_sc[...] = jnp.zeros(accw_sc.shape, F32)

    def window_chunk(j):
        start = t0 - WINDOW + j * KC
        k0 = pl.multiple_of(jnp.maximum(start, 0), KC)
        kpos = start + krow
        diff = tcol - kpos
        bias = jnp.where((kpos >= 0) & (diff >= 0) & (diff < WINDOW), 0.0, NEG_INF)
        return k0, kw_ref[pl.ds(k0, KC), :], bias

    w_k0 = [None] * WCHUNKS
    w_k0[2], kwin, wbias = window_chunk(2)
    produce(buf_a, mx_a, kwin, wbias)
    w_k0[1], kwin, wbias = window_chunk(1)
    produce(buf_b, mx_b, kwin, wbias)

    nblk = kvc_ref.shape[0] // 2
    kc = kvc_ref[0:nblk, :].astype(BF16)
    vct = kvc_ref[nblk:2 * nblk, :].astype(BF16)
    nrow = lax.broadcasted_iota(jnp.int32, (nblk, 1), 0)
    vbias = jnp.where(nrow * CMP_STRIDE + (CMP_BLOCK - 1) <= tcol, 0.0, NEG_INF)
    has_any = tcol >= CMP_BLOCK - 1
    s_c = [_dot(kc, qaug[0:LANES, cols]) + vbias for cols in groups]
    e_c = [jnp.exp2(y - jnp.max(y, axis=0, keepdims=True)) for y in s_c]
    p_c = [jnp.where(has_any, y * (1.0 / jnp.sum(y, axis=0, keepdims=True)), 0.0) for y in e_c]
    gates = gt_ref[...]
    for r, cols in enumerate(groups):
        g = r % N_GROUPS
        out_sc[:, cols] = gates[r:r + 1] * _dot(vct[g * HEAD_DIM:(g + 1) * HEAD_DIM], p_c[r].astype(BF16))

    consume(buf_a, mx_a, v_chunk(vwt_ref, w_k0[2]), mw_sc, accw_sc)
    w_k0[0], kwin, wbias = window_chunk(0)
    produce(buf_a, mx_a, kwin, wbias)

    jrow = lax.broadcasted_iota(jnp.int32, (N_SEL, 1), 0)
    cur = lax.shift_right_logical(tcol, 6)
    forced = (jrow == 0) | (jrow == cur) | (jrow == cur - 1)
    causal = jrow <= cur
    for g in range(N_GROUPS):
        psum = (p_c[g] + p_c[2 + g]) + (p_c[4 + g] + p_c[6 + g])
        imp_t = _dot3(ovt_ref[...], psum)
        score = jnp.where(forced, FORCE_SCORE, jnp.where(causal, imp_t[0:N_SEL, :], NEG_INF))
        cnt = jnp.zeros((N_SEL, TQ), jnp.int32)
        for jp in range(N_SEL):
            rowv = score[jp:jp + 1, :]
            beats = (rowv > score) | ((rowv == score) & (jrow > jp))
            cnt = cnt + jnp.where(beats, 1, 0)
        sel = (cnt < SEL_TOP_N) & causal
        bias = jnp.concatenate([jnp.where(sel, 0.0, MASK_BIAS),
                                jnp.zeros((LANES - N_SEL, TQ), F32)], axis=0).astype(BF16)
        for m in range(4):
            qaug[LANES:2 * LANES, groups[2 * m + g]] = bias

    ms_sc[...] = jnp.full(ms_sc.shape, NEG_INF, F32)
    accs_sc[...] = jnp.zeros(accs_sc.shape, F32)

    def sel_keys(c):
        k0 = pl.multiple_of(c * KC, KC)
        return jnp.concatenate([ks_ref[pl.ds(k0, KC), :], eblk_ref[pl.ds(k0, KC), :]], axis=1)

    def sel_consume(buf, mx, c):
        consume(buf, mx, v_chunk(vst_ref, c * KC), ms_sc, accs_sc)

    consume(buf_b, mx_b, v_chunk(vwt_ref, w_k0[1]), mw_sc, accw_sc)
    produce(buf_b, mx_b, sel_keys(tile), jnp.where(t0 + krow <= tcol, 0.0, NEG_INF))
    consume(buf_a, mx_a, v_chunk(vwt_ref, w_k0[0]), mw_sc, accw_sc)
    n_full = tile
    n_pairs = n_full // 2
    odd = n_full % 2 == 1

    def body(j, carry):
        produce(buf_a, mx_a, sel_keys(2 * j), None)
        sel_consume(buf_b, mx_b, jnp.where(j == 0, tile, 2 * j - 1))
        produce(buf_b, mx_b, sel_keys(2 * j + 1), None)
        sel_consume(buf_a, mx_a, 2 * j)
        return carry

    lax.fori_loop(0, n_pairs, body, 0)
    last_b = jnp.where(n_pairs == 0, tile, 2 * n_pairs - 1)

    @pl.when(odd)
    def _():
        produce(buf_a, mx_a, sel_keys(2 * n_pairs), None)

    sel_consume(buf_b, mx_b, last_b)

    @pl.when(odd)
    def _():
        sel_consume(buf_a, mx_a, 2 * n_pairs)

    for m in range(4):
        halves = []
        for g in range(N_GROUPS):
            r = 2 * m + g
            acc_s = accs_sc[:, groups[r]]
            acc_w = accw_sc[:, groups[r]]
            halves.append(out_sc[:, groups[r]]
                          + (gates[8 + r:9 + r] * (1.0 / acc_s[HEAD_DIM:HEAD_DIM + 1])) * acc_s[0:HEAD_DIM]
                          + (gates[16 + r:17 + r] * (1.0 / acc_w[HEAD_DIM:HEAD_DIM + 1])) * acc_w[0:HEAD_DIM])
        o_ref[:, m * LANES:(m + 1) * LANES] = jnp.concatenate(halves, axis=0).T.astype(BF16)


def _nsa(qt, ks, kw, vst, vwt, kvc, small_t, ovt, eblk, seq_len):
    n = ks.shape[0]
    nb = n // seq_len
    nq = seq_len // TQ
    nkb = seq_len // LANES
    seq = lambda b, i: (b, 0)
    return pl.pallas_call(
        _nsa_kernel,
        grid=(nb, nq),
        in_specs=[pl.BlockSpec((NSA_WIDTH, TQ), lambda b, i: (0, b * nq + i)),
                  pl.BlockSpec((seq_len, LANES), seq), pl.BlockSpec((seq_len, LANES), seq),
                  pl.BlockSpec((nkb, LANES, LANES), lambda b, i: (b, 0, 0)),
                  pl.BlockSpec((nkb, LANES, LANES), lambda b, i: (b, 0, 0)),
                  pl.BlockSpec((2 * LANES, LANES), seq),
                  pl.BlockSpec((N_SMALL, TQ), lambda b, i: (0, b * nq + i)),
                  pl.BlockSpec((LANES, LANES), lambda b, i: (0, 0)),
                  pl.BlockSpec((seq_len, LANES), lambda b, i: (0, 0))],
        out_specs=pl.BlockSpec((TQ, NSA_WIDTH), lambda b, i: (b * nq + i, 0)),
        out_shape=jax.ShapeDtypeStruct((n, NSA_WIDTH), BF16),
        scratch_shapes=[pltpu.VMEM((2 * LANES, ROWS), BF16),
                        pltpu.VMEM((2, KC, ROWS), F32),
                        pltpu.VMEM((2, 1, ROWS), F32),
                        pltpu.VMEM((1, ROWS), F32), pltpu.VMEM((1, ROWS), F32),
                        pltpu.VMEM((HEAD_DIM + ONES_ROWS, ROWS), F32),
                        pltpu.VMEM((HEAD_DIM + ONES_ROWS, ROWS), F32),
                        pltpu.VMEM((HEAD_DIM, ROWS), F32)],
        compiler_params=pltpu.CompilerParams(dimension_semantics=("arbitrary", "arbitrary"),
                                             vmem_limit_bytes=VMEM_LIMIT),
        name="nsa_attention",
    )(qt, ks, kw, vst, vwt, kvc, small_t, ovt, eblk)


def _nsa_constants(seq_len):
    n_cmp = (seq_len - CMP_BLOCK) // CMP_STRIDE + 1
    s = jnp.arange(LANES)[:, None]
    nn = jnp.arange(LANES)[None, :]
    cs = nn * CMP_STRIDE
    ss = s * SEL_BLOCK
    ovt = (cs < ss + SEL_BLOCK) & (cs + CMP_BLOCK > ss) & (s < seq_len // SEL_BLOCK) & (nn < n_cmp)
    k = jnp.arange(seq_len)[:, None]
    eblk = (k // SEL_BLOCK) == jnp.arange(LANES)[None, :]
    return ovt.astype(BF16), eblk.astype(BF16)


PAIR = 2 * GDN_CHUNK
N_DOUBLINGS = 5
GT_G = 4


def _gdn_kernel(q_ref, k_ref, v_ref, z_ref, small_ref, gt_ref, ng_ref, o_ref, s_sc):
    nseq = q_ref.shape[0]
    units = [(s, h) for s in range(nseq) for h in range(GDN_HEADS)]
    us = range(len(units))
    cols = [slice(h * GDN_DIM, (h + 1) * GDN_DIM) for _, h in units]
    ri = lax.broadcasted_iota(jnp.int32, (PAIR, PAIR), 0)
    ci = lax.broadcasted_iota(jnp.int32, (PAIR, PAIR), 1)
    same = lax.shift_right_logical(ri, 6) == lax.shift_right_logical(ci, 6)
    incl = same & (ri >= ci)
    strict = same & (ri > ci)
    first_cols = ci < GDN_CHUNK
    first_row = ci[0:1, :] < GDN_CHUNK
    ltri = jnp.where(incl, 1.0, 0.0).astype(BF16)
    tot = [jnp.where(first_cols, 1.0, 0.0).astype(BF16),
           jnp.where(first_cols, 0.0, 1.0).astype(BF16)]

    @pl.when(pl.program_id(1) == 0)
    def _():
        s_sc[...] = jnp.zeros(s_sc.shape, F32)

    sm_s = [small_ref[s] for s in range(nseq)]
    cs_s = [_dot3(ltri, y) for y in sm_s]
    gt_s = [gt_ref[s, 0] for s in range(nseq)]
    csr_s = [_dot3(ltri, y, nt=True) for y in gt_s]
    glast_s = [[_dot3(tot[c], y, nt=True) for c in range(2)] for y in gt_s]
    seq = [s for s, _ in units]
    q = [q_ref[seq[u], :, cols[u]] for u in us]
    k = [k_ref[seq[u], :, cols[u]] for u in us]
    kf = [y.astype(F32) for y in k]
    beta = [sm_s[s][:, SMALL_B + h:SMALL_B + h + 1] for s, h in units]
    gcc = [cs_s[s][:, SMALL_A + h:SMALL_A + h + 1] for s, h in units]
    gcr = [csr_s[s][GT_G + h:GT_G + h + 1, :] for s, h in units]
    glast = [[glast_s[s][c][GT_G + h:GT_G + h + 1, :] for c in range(2)] for s, h in units]
    decay = [jnp.exp(jnp.where(incl, gcc[u] - gcr[u], NEG_INF)) for u in us]
    egc = [jnp.exp(g) for g in gcc]
    kb = [kf[u] * beta[u] for u in us]
    kk = [_dot_nt(kb[u].astype(BF16), k[u]) for u in us]
    p = [jnp.where(strict, -(kk[u] * decay[u]), 0.0) for u in us]
    x = [jnp.concatenate([v_ref[seq[u], :, cols[u]].astype(F32) * beta[u], kb[u] * egc[u]], axis=1) for u in us]
    for i in range(N_DOUBLINGS + 1):
        pb = [y.astype(BF16) for y in p]
        x = [x[u] + _dot(pb[u], x[u].astype(BF16)) for u in us]
        if i < N_DOUBLINGS:
            p = [_dot(y, y) for y in pb]
    uw = [y.astype(BF16) for y in x]
    qk = [(_dot_nt(q[u], k[u]) * decay[u]).astype(BF16) for u in us]
    ke_t = [kf[u].T * jnp.exp(jnp.where(first_row, glast[u][0], glast[u][1]) - gcr[u]) for u in us]
    kw = [[_dot(jnp.where(first_cols, y, 0.0).astype(BF16), uw[u]) for u, y in enumerate(ke_t)],
          [_dot(jnp.where(first_cols, 0.0, y).astype(BF16), uw[u]) for u, y in enumerate(ke_t)]]
    qw = [_dot(qk[u], uw[u]) for u in us]
    qm = [(q[u].astype(F32) * egc[u] - qw[u][:, GDN_DIM:2 * GDN_DIM]).astype(BF16) for u in us]
    s = [s_sc[u] for u in us]
    os = []
    for c in range(2):
        half = slice(c * GDN_CHUNK, (c + 1) * GDN_CHUNK)
        sb = [y.astype(BF16) for y in s]
        os.append([_dot(qm[u][half], sb[u]) for u in us])
        ks = [_dot(kw[c][u][:, GDN_DIM:2 * GDN_DIM].astype(BF16), sb[u]) for u in us]
        s = [s[u] * jnp.exp(glast[u][c]) - ks[u] + kw[c][u][:, 0:GDN_DIM] for u in us]
    for u in us:
        s_sc[u] = s[u]
        o = jnp.concatenate([os[0][u], os[1][u]], axis=0) + qw[u][:, 0:GDN_DIM]
        o = o * lax.rsqrt(jnp.mean(o * o, axis=-1, keepdims=True) + RMS_EPS) * ng_ref[...]
        o_ref[seq[u], :, cols[u]] = (o * _silu(z_ref[seq[u], :, cols[u]].astype(F32))).astype(BF16)


GDN_SEQS = 4


def _gdn(gq, gk, gv, z, small, gt, norm_g, seq_len):
    n = gq.shape[0]
    nb = n // seq_len
    n_pairs = seq_len // PAIR
    nseq = max(d for d in range(1, GDN_SEQS + 1) if nb % d == 0)
    by_seq = lambda a: a.reshape(nb, seq_len, a.shape[-1])
    blk = lambda w: pl.BlockSpec((nseq, PAIR, w), lambda b, i: (b, i, 0))
    out = pl.pallas_call(
        _gdn_kernel,
        grid=(nb // nseq, n_pairs),
        in_specs=[blk(GDN_WIDTH), blk(GDN_WIDTH), blk(GDN_WIDTH), blk(GDN_WIDTH), blk(LANES),
                  pl.BlockSpec((nseq, 1, SUBLANES, PAIR), lambda b, i: (b, i, 0, 0)),
                  pl.BlockSpec((1, GDN_DIM), lambda b, i: (0, 0))],
        out_specs=blk(GDN_WIDTH),
        out_shape=jax.ShapeDtypeStruct((nb, seq_len, GDN_WIDTH), BF16),
        scratch_shapes=[pltpu.VMEM((nseq * GDN_HEADS, GDN_DIM, GDN_DIM), F32)],
        compiler_params=pltpu.CompilerParams(dimension_semantics=("arbitrary", "arbitrary"),
                                             vmem_limit_bytes=VMEM_LIMIT),
        name="gated_delta_rule",
    )(by_seq(gq), by_seq(gk), by_seq(gv), by_seq(z), by_seq(small),
      gt.reshape(nb, n_pairs, SUBLANES, PAIR), norm_g)
    return out.reshape(n, GDN_WIDTH)


FF_CHUNK = 256


def _rms(x, g):
    return x * lax.rsqrt(jnp.mean(x * x, axis=-1, keepdims=True) + RMS_EPS) * g


def _out_mlp_kernel(x_ref, on_ref, og_ref, gn_ref, wo_ref, g2_ref, wg_ref, wu_ref, wd_ref, gf_ref, out_ref,
                    act_sc):
    o_nsa = _rms(on_ref[...].astype(F32), gn_ref[...]).astype(BF16)
    mix = jnp.concatenate([o_nsa, og_ref[...]], axis=1)
    h = x_ref[...] + _dot(mix, wo_ref[...])
    hn = _rms(h, g2_ref[...]).astype(BF16)
    for c in range(D_FF // FF_CHUNK):
        cols = slice(c * FF_CHUNK, (c + 1) * FF_CHUNK)
        act_sc[:, cols] = (_silu(_dot(hn, wg_ref[:, cols])) * _dot(hn, wu_ref[:, cols])).astype(BF16)
    out_ref[...] = _rms(h + _dot(act_sc[...], wd_ref[...]), gf_ref[...])


def _out_mlp(x2, o_nsa, o_gdn, gn, wo, g2, wg, wu, wd, gf):
    n = x2.shape[0]
    row = lambda i: (i, 0)
    const = lambda i: (0, 0)
    resident = lambda shape: pl.BlockSpec(shape, const, pipeline_mode=pl.Buffered(1))
    return pl.pallas_call(
        _out_mlp_kernel,
        grid=(n // TM,),
        in_specs=[pl.BlockSpec((TM, D_MODEL), row), pl.BlockSpec((TM, NSA_WIDTH), row),
                  pl.BlockSpec((TM, GDN_WIDTH), row), pl.BlockSpec((1, NSA_WIDTH), const),
                  resident((D_MODEL, D_MODEL)), pl.BlockSpec((1, D_MODEL), const),
                  resident((D_MODEL, D_FF)), resident((D_MODEL, D_FF)), resident((D_FF, D_MODEL)),
                  pl.BlockSpec((1, D_MODEL), const)],
        out_specs=pl.BlockSpec((TM, D_MODEL), row),
        out_shape=jax.ShapeDtypeStruct((n, D_MODEL), F32),
        scratch_shapes=[pltpu.VMEM((TM, D_FF), BF16)],
        compiler_params=pltpu.CompilerParams(dimension_semantics=("arbitrary",),
                                             vmem_limit_bytes=VMEM_LIMIT),
        name="out_mlp",
    )(x2, o_nsa, o_gdn, gn, wo, g2, wg, wu, wd, gf)


def _nsa_head_perm():
    c = jnp.arange(NSA_WIDTH)
    return ((c // LANES) + 4 * ((c % LANES) // HEAD_DIM)) * HEAD_DIM + c % HEAD_DIM


def _layer(x2, pos_row, seq_len, norm1_g, w_in, cmp_pos, cmp_w1, cmp_w2, nsa_norm_g, gdn_conv_w, gdn_a_log,
           gdn_dt_bias, gdn_norm_g, w_out, norm2_g, w_gate, w_up, w_down, out_g):
    w_t, w_r, alog, dtb, alogt, dtbt = _prep_in_proj_weights(w_in, gdn_a_log, gdn_dt_bias)
    qt, cmp_k, ks, kw, vst, vwt, cmp_v, gq, gk, gv, z, small, small_t, gt = _in_proj(
        x2, pos_row, norm1_g[None, :].astype(F32), w_t, w_r, _rope_inv_freq(), gdn_conv_w.astype(F32),
        alog, dtb, alogt, dtbt, seq_len)
    kvc = _compress(cmp_k, cmp_v, *_prep_compress_weights(cmp_pos, cmp_w1, cmp_w2), seq_len)
    ovt, eblk = _nsa_constants(seq_len)
    o_nsa = _nsa(qt, ks, kw, vst, vwt, kvc, small_t, ovt, eblk, seq_len)
    o_gdn = _gdn(gq, gk, gv, z, small, gt, gdn_norm_g[None, :].astype(F32), seq_len)
    perm = _nsa_head_perm()
    wo = jnp.concatenate([w_out[:NSA_WIDTH][perm], w_out[NSA_WIDTH:]], axis=0).astype(BF16)
    return _out_mlp(x2, o_nsa, o_gdn, nsa_norm_g[perm][None, :].astype(F32), wo, norm2_g[None, :].astype(F32),
                    w_gate.astype(BF16), w_up.astype(BF16), w_down.astype(BF16), out_g[None, :].astype(F32))


def kernel(x, positions, norm1_g, w_in, cmp_pos, cmp_w1, cmp_w2, nsa_norm_g, gdn_conv_w, gdn_a_log,
           gdn_dt_bias, gdn_norm_g, w_out, norm2_g, w_gate, w_up, w_down, final_g):
    nb, seq_len, d = x.shape
    depth = w_in.shape[0]
    assert d == D_MODEL and seq_len % TM == 0 and seq_len // SEL_BLOCK == N_SEL and depth == 1
    x2 = x.reshape(nb * seq_len, d)
    pos_row = positions.reshape(1, nb * seq_len)
    out = _layer(x2, pos_row, seq_len, norm1_g[0], w_in[0], cmp_pos[0], cmp_w1[0], cmp_w2[0], nsa_norm_g[0],
                 gdn_conv_w[0], gdn_a_log[0], gdn_dt_bias[0], gdn_norm_g[0], w_out[0], norm2_g[0],
                 w_gate[0], w_up[0], w_down[0], final_g)
    return out.reshape(nb, seq_len, d)
```

```python
import functools
import math

import jax
import jax.numpy as jnp
from jax import lax
from jax.experimental import pallas as pl
from jax.experimental.pallas import tpu as pltpu

F32 = jnp.float32
BF16 = jnp.bfloat16

LANES = 128
SUBLANES = 8

D_MODEL = 1024
N_HEADS = 8
N_GROUPS = 2
HEAD_DIM = 64
CMP_BLOCK = 32
CMP_STRIDE = 16
CMP_HIDDEN = 128
SEL_BLOCK = 64
SEL_TOP_N = 8
WINDOW = 512
ROPE_THETA = 500000.0
ROPE_DIM = 16
GDN_HEADS = 4
GDN_DIM = 128
GDN_CONV = 4
GDN_CHUNK = 64
NSA_WIDTH = 512
GDN_WIDTH = 512
D_FF = 2816
RMS_EPS = 1e-6
NEG_INF = -1e30
MASK_BIAS = -1e9
FORCE_SCORE = 1e9

OFF_KV = 512
OFF_GATE = OFF_KV + 6 * 128
OFF_GQKV = OFF_GATE + 24
OFF_Z = OFF_GQKV + 3 * GDN_WIDTH
OFF_B = OFF_Z + GDN_WIDTH
OFF_A = OFF_B + GDN_HEADS

TR_Q = 0
TR_K = 512
TR_V = 896
TR_SMALL = 1152
TR_ROWS = 1184
TM_VCMP = 0
TM_GDN = 128
TM_Z = 1664
TM_SMALL = 2176
TM_COLS = 2304
SMALL_B = 24
SMALL_A = 28
N_SMALL = 32

Q_SCALE = HEAD_DIM ** -0.5 * math.log2(math.e)
TM = 512
IN_HALVES = 2
VMEM_LIMIT = 56 * 1024 * 1024


def _dot(a, b):
    return jnp.dot(a, b, preferred_element_type=F32)


def _dot_nt(a, b):
    return lax.dot_general(a, b, (((1,), (1,)), ((), ())), preferred_element_type=F32)


def _sigmoid(x):
    return 1.0 / (1.0 + jnp.exp(-x))


def _silu(x):
    return x * _sigmoid(x)


def _softplus(x):
    return jnp.maximum(x, 0.0) + jnp.log(1.0 + jnp.exp(-jnp.abs(x)))


def _split3(x):
    h1 = x.astype(BF16)
    r1 = x - h1.astype(F32)
    h2 = r1.astype(BF16)
    h3 = (r1 - h2.astype(F32)).astype(BF16)
    return h1, h2, h3


def _dot3(mat01, x, nt=False):
    f = (lambda a: _dot_nt(a, mat01)) if nt else (lambda a: _dot(mat01, a))
    h1, h2, h3 = _split3(x)
    return f(h1) + f(h2) + f(h3)


def _rope_t(blk, cosv, sinv):
    half = ROPE_DIM // 2
    parts = []
    for h in range(blk.shape[0] // HEAD_DIM):
        b = h * HEAD_DIM
        x0 = blk[b:b + half]
        x1 = blk[b + half:b + ROPE_DIM]
        parts += [x0 * cosv - x1 * sinv, x1 * cosv + x0 * sinv, blk[b + ROPE_DIM:b + HEAD_DIM]]
    return jnp.concatenate(parts, axis=0)


def _in_proj_kernel(tiles_per_seq, x_ref, pos_ref, g1_ref, wt_ref, w_ref, invf_ref, convw_ref,
                    alog_ref, dtb_ref, alogt_ref, dtbt_ref,
                    qt_ref, kc_ref, ks_ref, kw_ref, vst_ref, vwt_ref, cmpv_ref,
                    gq_ref, gk_ref, gv_ref, z_ref, small_ref, smallt_ref, gt_ref,
                    cbuf):
    tm = x_ref.shape[0] // IN_HALVES
    nlb = tm // LANES
    first = pl.program_id(0) % tiles_per_seq == 0

    @pl.when(first)
    def _():
        cbuf[0:SUBLANES, :] = jnp.zeros((SUBLANES, 3 * GDN_WIDTH), F32)

    def normed(h):
        x = x_ref[h * tm:(h + 1) * tm, :]
        hn = x * lax.rsqrt(jnp.mean(x * x, axis=-1, keepdims=True) + RMS_EPS) * g1_ref[...]
        return hn.astype(BF16)

    def pieces(h, hb, proj):
        r0 = SUBLANES + h * tm

        def gdn_piece(c):
            lo = TM_GDN + 2 * c * LANES
            cbuf[r0:r0 + tm, 2 * c * LANES:2 * (c + 1) * LANES] = _dot(hb, w_ref[:, lo:lo + 2 * LANES])

        def t_piece(name, lo, hi):
            proj[name] = _dot_nt(wt_ref[lo:hi, :], hb)

        def tm_piece(name, lo, hi):
            proj[name] = _dot(hb, w_ref[:, lo:hi])

        return [lambda: gdn_piece(0), lambda: gdn_piece(1),
                lambda: gdn_piece(2), lambda: t_piece("q01", TR_Q, TR_Q + 2 * LANES),
                lambda: gdn_piece(3), lambda: t_piece("q23", TR_Q + 2 * LANES, TR_K),
                lambda: gdn_piece(4), lambda: t_piece("k", TR_K, TR_V),
                lambda: gdn_piece(5), lambda: t_piece("v", TR_V, TR_ROWS),
                lambda: tm_piece("z0", TM_Z, TM_Z + 2 * LANES), lambda: tm_piece("z1", TM_Z + 2 * LANES, TM_SMALL),
                lambda: tm_piece("vcmp", TM_VCMP, TM_VCMP + LANES), lambda: tm_piece("small", TM_SMALL, TM_COLS)]

    ones_sq = jnp.ones((LANES, LANES), BF16)

    def conv_block(h, cb):
        cols = slice(cb * LANES, (cb + 1) * LANES)
        zx = cbuf[h * tm:h * tm + SUBLANES + tm, cols]
        y = zx * convw_ref[GDN_CONV - 1:GDN_CONV, cols]
        for s in range(1, GDN_CONV):
            y = y + pltpu.roll(zx, s, 0) * convw_ref[GDN_CONV - 1 - s:GDN_CONV - s, cols]
        return _silu(y[SUBLANES:])

    def finish_block(h, cb, y):
        rows = slice(h * tm, (h + 1) * tm)
        cols = slice((cb % 4) * LANES, (cb % 4 + 1) * LANES)
        if cb < 8:
            y = y * lax.rsqrt(_dot((y * y).astype(BF16), ones_sq) + RMS_EPS)
        if cb < 4:
            gq_ref[rows, cols] = (y * (GDN_DIM ** -0.5)).astype(BF16)
        elif cb < 8:
            gk_ref[rows, cols] = y.astype(BF16)
        else:
            gv_ref[rows, cols] = y.astype(BF16)

    def post(h, proj):
        rows = slice(h * tm, (h + 1) * tm)
        yt = jnp.concatenate([proj["q01"], proj["q23"], proj["k"], proj["v"]], axis=0)
        ang = invf_ref[...] * pos_ref[:, rows].astype(F32)
        cosv = jnp.cos(ang)
        sinv = jnp.sin(ang)
        for m in range(4):
            blk = _rope_t(yt[TR_Q + m * LANES:TR_Q + (m + 1) * LANES], cosv, sinv)
            qt_ref[m * LANES:(m + 1) * LANES, rows] = (blk * Q_SCALE).astype(BF16)
        for j, ref in enumerate((kc_ref, ks_ref, kw_ref)):
            blk = _rope_t(yt[TR_K + j * LANES:TR_K + (j + 1) * LANES], cosv, sinv)
            for c in range(nlb):
                r0 = h * tm + c * LANES
                ref[r0:r0 + LANES, :] = blk[:, c * LANES:(c + 1) * LANES].T.astype(ref.dtype)
        for j, ref in enumerate((vst_ref, vwt_ref)):
            blk = yt[TR_V + j * LANES:TR_V + (j + 1) * LANES].astype(BF16)
            for c in range(nlb):
                ref[h * nlb + c] = blk[:, c * LANES:(c + 1) * LANES]
        st = yt[TR_SMALL:TR_SMALL + N_SMALL]
        srow = lax.broadcasted_iota(jnp.int32, (N_SMALL, 1), 0)
        gdec_t = -jnp.exp(alogt_ref[...]) * _softplus(st + dtbt_ref[...])
        small_t = jnp.where(srow < SMALL_A, _sigmoid(st), gdec_t)
        smallt_ref[:, rows] = small_t
        for c in range(nlb):
            gt_ref[h * nlb + c] = small_t[SMALL_B:N_SMALL, c * LANES:(c + 1) * LANES]
        cmpv_ref[rows, :] = proj["vcmp"]
        z_ref[rows, 0:2 * LANES] = proj["z0"].astype(BF16)
        z_ref[rows, 2 * LANES:GDN_WIDTH] = proj["z1"].astype(BF16)
        sm = proj["small"]
        lane1 = lax.broadcasted_iota(jnp.int32, (1, LANES), 1)
        gdec = -jnp.exp(alog_ref[...]) * _softplus(sm + dtb_ref[...])
        small_ref[rows, :] = jnp.where(lane1 < SMALL_A, _sigmoid(sm), gdec)

    projs = [{} for _ in range(IN_HALVES)]
    issue = [None] * IN_HALVES
    issue[0] = pieces(0, normed(0), projs[0])
    issue[0][0]()
    issue[0][1]()
    for h in range(IN_HALVES):
        if h + 1 < IN_HALVES:
            issue[h + 1] = pieces(h + 1, normed(h + 1), projs[h + 1])
        pending = None
        for cb in range(12):
            issue[h][cb + 2]()
            y = conv_block(h, cb)
            if pending is not None:
                finish_block(h, *pending)
            pending = (cb, y)
        finish_block(h, *pending)
        if h + 1 < IN_HALVES:
            issue[h + 1][0]()
            issue[h + 1][1]()
        post(h, projs[h])
    cbuf[0:SUBLANES, :] = cbuf[IN_HALVES * tm:IN_HALVES * tm + SUBLANES, :]


def _in_proj(x2, pos_row, g1, w_t, w_r, invf, convw, alog, dtb, alogt, dtbt, seq_len):
    n = x2.shape[0]
    row = lambda i: (i, 0)
    colb = lambda i: (0, i)
    lead = lambda i: (i, 0, 0)
    const = lambda i: (0, 0)
    tmi = IN_HALVES * TM
    nlb = tmi // LANES
    out_shapes = (
        jax.ShapeDtypeStruct((NSA_WIDTH, n), BF16),
        jax.ShapeDtypeStruct((n, LANES), F32),
        jax.ShapeDtypeStruct((n, LANES), BF16),
        jax.ShapeDtypeStruct((n, LANES), BF16),
        jax.ShapeDtypeStruct((n // LANES, LANES, LANES), BF16),
        jax.ShapeDtypeStruct((n // LANES, LANES, LANES), BF16),
        jax.ShapeDtypeStruct((n, LANES), F32),
        jax.ShapeDtypeStruct((n, GDN_WIDTH), BF16),
        jax.ShapeDtypeStruct((n, GDN_WIDTH), BF16),
        jax.ShapeDtypeStruct((n, GDN_WIDTH), BF16),
        jax.ShapeDtypeStruct((n, GDN_WIDTH), BF16),
        jax.ShapeDtypeStruct((n, LANES), F32),
        jax.ShapeDtypeStruct((N_SMALL, n), F32),
        jax.ShapeDtypeStruct((n // LANES, SUBLANES, LANES), F32),
    )
    out_specs = (
        pl.BlockSpec((NSA_WIDTH, tmi), colb),
        pl.BlockSpec((tmi, LANES), row), pl.BlockSpec((tmi, LANES), row), pl.BlockSpec((tmi, LANES), row),
        pl.BlockSpec((nlb, LANES, LANES), lead), pl.BlockSpec((nlb, LANES, LANES), lead),
        pl.BlockSpec((tmi, LANES), row),
        pl.BlockSpec((tmi, GDN_WIDTH), row), pl.BlockSpec((tmi, GDN_WIDTH), row),
        pl.BlockSpec((tmi, GDN_WIDTH), row), pl.BlockSpec((tmi, GDN_WIDTH), row),
        pl.BlockSpec((tmi, LANES), row),
        pl.BlockSpec((N_SMALL, tmi), colb),
        pl.BlockSpec((nlb, SUBLANES, LANES), lead),
    )
    in_specs = [
        pl.BlockSpec((tmi, D_MODEL), row), pl.BlockSpec((1, tmi), colb), pl.BlockSpec((1, D_MODEL), const),
        pl.BlockSpec((TR_ROWS, D_MODEL), const), pl.BlockSpec((D_MODEL, TM_COLS), const),
        pl.BlockSpec((ROPE_DIM // 2, 1), const), pl.BlockSpec((GDN_CONV, 3 * GDN_WIDTH), const),
        pl.BlockSpec((1, LANES), const), pl.BlockSpec((1, LANES), const),
        pl.BlockSpec((N_SMALL, 1), const), pl.BlockSpec((N_SMALL, 1), const),
    ]
    return pl.pallas_call(
        functools.partial(_in_proj_kernel, seq_len // tmi),
        grid=(n // tmi,), in_specs=in_specs, out_specs=out_specs, out_shape=out_shapes,
        scratch_shapes=[pltpu.VMEM((tmi + 2 * SUBLANES, 3 * GDN_WIDTH), F32)],
        compiler_params=pltpu.CompilerParams(dimension_semantics=("arbitrary",),
                                             vmem_limit_bytes=VMEM_LIMIT),
        name="in_proj",
    )(x2, pos_row, g1, w_t, w_r, invf, convw, alog, dtb, alogt, dtbt)


def _prep_in_proj_weights(w_in, gdn_a_log, gdn_dt_bias):
    q = w_in[:, :NSA_WIDTH].reshape(D_MODEL, 2, 4, HEAD_DIM)
    q = jnp.transpose(q, (0, 2, 1, 3)).reshape(D_MODEL, NSA_WIDTH)
    kv = w_in[:, OFF_KV:OFF_GATE].reshape(D_MODEL, 6, LANES)
    gate = w_in[:, OFF_GATE:OFF_GQKV].reshape(D_MODEL, 2, 4, 3)
    gate = jnp.transpose(gate, (0, 3, 2, 1)).reshape(D_MODEL, 24)
    small = jnp.concatenate([gate, w_in[:, OFF_B:OFF_A], w_in[:, OFF_A:OFF_A + GDN_HEADS]], axis=1)
    w_t = jnp.concatenate([q, kv[:, 0], kv[:, 2], kv[:, 4], kv[:, 3], kv[:, 5], small], axis=1).T.astype(BF16)
    w_r = jnp.concatenate([kv[:, 1], w_in[:, OFF_GQKV:OFF_Z], w_in[:, OFF_Z:OFF_B], small,
                           jnp.zeros((D_MODEL, LANES - N_SMALL), w_in.dtype)], axis=1).astype(BF16)
    alog = jnp.zeros((LANES,), F32).at[SMALL_A:SMALL_A + GDN_HEADS].set(gdn_a_log.astype(F32))
    dtb = jnp.zeros((LANES,), F32).at[SMALL_A:SMALL_A + GDN_HEADS].set(gdn_dt_bias.astype(F32))
    return w_t, w_r, alog[None, :], dtb[None, :], alog[:N_SMALL, None], dtb[:N_SMALL, None]


def _rope_inv_freq():
    half = ROPE_DIM // 2
    return jnp.power(ROPE_THETA, -jnp.arange(half, dtype=F32) * (2.0 / ROPE_DIM))[:, None]


def _compress_kernel(xk_ref, xv_ref, pos_ref, w1_ref, w2_ref, out_ref):
    nblk = xk_ref.shape[0] // CMP_STRIDE
    acc_lo = jnp.zeros((nblk, 4 * CMP_HIDDEN), F32)
    acc_hi = jnp.zeros((nblk, 4 * CMP_HIDDEN), F32)
    for j in range(CMP_STRIDE):
        xj = jnp.concatenate([xk_ref[pl.ds(j, nblk, stride=CMP_STRIDE), :],
                              xv_ref[pl.ds(j, nblk, stride=CMP_STRIDE), :]], axis=1)
        acc_lo = acc_lo + _dot((xj + pos_ref[j:j + 1, :]).astype(BF16), w1_ref[j])
        acc_hi = acc_hi + _dot((xj + pos_ref[CMP_STRIDE + j:CMP_STRIDE + j + 1, :]).astype(BF16),
                               w1_ref[CMP_STRIDE + j])
    pre = acc_lo + pltpu.roll(acc_hi, nblk - 1, 0)
    kvc = _dot(_silu(pre).astype(BF16), w2_ref[...])
    out_ref[0:nblk, :] = kvc[:, 0:LANES]
    out_ref[nblk:2 * nblk, :] = kvc[:, LANES:2 * LANES].T


def _compress(cmp_k, cmp_v, pos_rows, w1_bd, w2_bd, seq_len):
    n = cmp_k.shape[0]
    nb = n // seq_len
    nblk = seq_len // CMP_STRIDE
    assert nblk == LANES
    return pl.pallas_call(
        _compress_kernel,
        grid=(nb,),
        in_specs=[pl.BlockSpec((seq_len, LANES), lambda b: (b, 0)),
                  pl.BlockSpec((seq_len, LANES), lambda b: (b, 0)),
                  pl.BlockSpec((CMP_BLOCK, 256), lambda b: (0, 0)),
                  pl.BlockSpec((CMP_BLOCK, 256, 4 * CMP_HIDDEN), lambda b: (0, 0, 0)),
                  pl.BlockSpec((4 * CMP_HIDDEN, 256), lambda b: (0, 0))],
        out_specs=pl.BlockSpec((2 * nblk, LANES), lambda b: (b, 0)),
        out_shape=jax.ShapeDtypeStruct((nb * 2 * nblk, LANES), F32),
        compiler_params=pltpu.CompilerParams(dimension_semantics=("arbitrary",),
                                             vmem_limit_bytes=VMEM_LIMIT),
        name="nsa_compress",
    )(cmp_k, cmp_v, pos_rows, w1_bd, w2_bd)


def _prep_compress_weights(cmp_pos, cmp_w1, cmp_w2):
    slot_src = jnp.array([0, 0, 1, 1])
    eye = jnp.eye(4, dtype=BF16)
    w1 = cmp_w1.reshape(2, CMP_BLOCK, HEAD_DIM, CMP_HIDDEN).astype(BF16)[slot_src]
    w1 = jnp.transpose(w1, (1, 0, 2, 3)).reshape(CMP_BLOCK, 4 * HEAD_DIM, CMP_HIDDEN)
    diag = jnp.repeat(jnp.repeat(eye, HEAD_DIM, axis=0), CMP_HIDDEN, axis=1)
    w1_bd = jnp.tile(w1, (1, 1, 4)) * diag[None]
    w2_bd = cmp_w2.astype(BF16)[slot_src][:, :, None, :] * eye[:, None, :, None]
    w2_bd = w2_bd.reshape(4 * CMP_HIDDEN, 256)
    pos_rows = jnp.concatenate([cmp_pos[0], cmp_pos[0], cmp_pos[1], cmp_pos[1]], axis=-1).astype(F32)
    return pos_rows, w1_bd, w2_bd


TQ = 256
KC = 256
N_SEL = 32
ROWS = N_HEADS * TQ
WCHUNKS = WINDOW // KC + 1
ONES_ROWS = 16


def _nsa_kernel(qt_ref, ks_ref, kw_ref, vst_ref, vwt_ref, kvc_ref, gt_ref, ovt_ref, eblk_ref, o_ref,
                qaug, sbuf, mxbuf, ms_sc, mw_sc, accs_sc, accw_sc, out_sc):
    assert WCHUNKS == 3
    tile = pl.program_id(1)
    t0 = tile * TQ
    tcol = t0 + lax.broadcasted_iota(jnp.int32, (1, TQ), 1)
    krow = lax.broadcasted_iota(jnp.int32, (KC, 1), 0)
    ones = jnp.ones((ONES_ROWS, KC), BF16)
    groups = [slice(r * TQ, (r + 1) * TQ) for r in range(N_HEADS)]
    blocks_per_chunk = KC // LANES

    zhalf = jnp.zeros((HEAD_DIM, TQ), BF16)
    for m in range(4):
        blk = qt_ref[m * LANES:(m + 1) * LANES, :]
        qaug[0:LANES, groups[2 * m]] = jnp.concatenate([blk[0:HEAD_DIM], zhalf], axis=0)
        qaug[0:LANES, groups[2 * m + 1]] = jnp.concatenate([zhalf, blk[HEAD_DIM:LANES]], axis=0)

    def v_chunk(vt_ref, k0):
        b0 = k0 // LANES
        vt = jnp.concatenate([vt_ref[b0 + j] for j in range(blocks_per_chunk)], axis=1)
        return [jnp.concatenate([vt[g * HEAD_DIM:(g + 1) * HEAD_DIM], ones], axis=0) for g in range(N_GROUPS)]

    buf_a, buf_b = sbuf.at[0], sbuf.at[1]
    mx_a, mx_b = mxbuf.at[0], mxbuf.at[1]

    def produce(buf, mx, kmat, bias):
        qrows = kmat.shape[1]
        for cols in groups:
            s = _dot(kmat, qaug[0:qrows, cols])
            if bias is not None:
                s = s + bias
            buf[:, cols] = s
            mx[:, cols] = jnp.max(s, axis=0, keepdims=True)

    def consume(buf, mx, vt, m_ref, acc_ref):
        for r, cols in enumerate(groups):
            m_old = m_ref[:, cols]
            m_new = jnp.maximum(m_old, mx[:, cols])
            p = jnp.exp2((buf[:, cols] - m_new).astype(BF16))
            m_ref[:, cols] = m_new
            acc_ref[:, cols] = acc_ref[:, cols] * jnp.exp2(m_old - m_new) + _dot(vt[r % N_GROUPS], p)

    mw_sc[...] = jnp.full(mw_sc.shape, NEG_INF, F32)
    accw_sc[...] = jnp.zeros(accw_sc.shape, F32)

    def window_chunk(j):
        start = t0 - WINDOW + j * KC
        k0 = pl.multiple_of(jnp.maximum(start, 0), KC)
        kpos = start + krow
        diff = tcol - kpos
        bias = jnp.where((kpos >= 0) & (diff >= 0) & (diff < WINDOW), 0.0, NEG_INF)
        return k0, kw_ref[pl.ds(k0, KC), :], bias

    w_k0 = [None] * WCHUNKS
    w_k0[2], kwin, wbias = window_chunk(2)
    produce(buf_a, mx_a, kwin, wbias)
    w_k0[1], kwin, wbias = window_chunk(1)
    produce(buf_b, mx_b, kwin, wbias)

    nblk = kvc_ref.shape[0] // 2
    kc = kvc_ref[0:nblk, :].astype(BF16)
    vct = kvc_ref[nblk:2 * nblk, :].astype(BF16)
    nrow = lax.broadcasted_iota(jnp.int32, (nblk, 1), 0)
    vbias = jnp.where(nrow * CMP_STRIDE + (CMP_BLOCK - 1) <= tcol, 0.0, NEG_INF)
    has_any = tcol >= CMP_BLOCK - 1
    s_c = [_dot(kc, qaug[0:LANES, cols]) + vbias for cols in groups]
    e_c = [jnp.exp2(y - jnp.max(y, axis=0, keepdims=True)) for y in s_c]
    p_c = [jnp.where(has_any, y * (1.0 / jnp.sum(y, axis=0, keepdims=True)), 0.0) for y in e_c]
    gates = gt_ref[...]
    for r, cols in enumerate(groups):
        g = r % N_GROUPS
        out_sc[:, cols] = gates[r:r + 1] * _dot(vct[g * HEAD_DIM:(g + 1) * HEAD_DIM], p_c[r].astype(BF16))

    consume(buf_a, mx_a, v_chunk(vwt_ref, w_k0[2]), mw_sc, accw_sc)
    w_k0[0], kwin, wbias = window_chunk(0)
    produce(buf_a, mx_a, kwin, wbias)

    jrow = lax.broadcasted_iota(jnp.int32, (N_SEL, 1), 0)
    cur = lax.shift_right_logical(tcol, 6)
    forced = (jrow == 0) | (jrow == cur) | (jrow == cur - 1)
    causal = jrow <= cur
    for g in range(N_GROUPS):
        psum = (p_c[g] + p_c[2 + g]) + (p_c[4 + g] + p_c[6 + g])
        imp_t = _dot3(ovt_ref[...], psum)
        score = jnp.where(forced, FORCE_SCORE, jnp.where(causal, imp_t[0:N_SEL, :], NEG_INF))
        cnt = jnp.zeros((N_SEL, TQ), jnp.int32)
        for jp in range(N_SEL):
            rowv = score[jp:jp + 1, :]
            beats = (rowv > score) | ((rowv == score) & (jrow > jp))
            cnt = cnt + jnp.where(beats, 1, 0)
        sel = (cnt < SEL_TOP_N) & causal
        bias = jnp.concatenate([jnp.where(sel, 0.0, MASK_BIAS),
                                jnp.zeros((LANES - N_SEL, TQ), F32)], axis=0).astype(BF16)
        for m in range(4):
            qaug[LANES:2 * LANES, groups[2 * m + g]] = bias

    ms_sc[...] = jnp.full(ms_sc.shape, NEG_INF, F32)
    accs_sc[...] = jnp.zeros(accs_sc.shape, F32)

    def sel_keys(c):
        k0 = pl.multiple_of(c * KC, KC)
        return jnp.concatenate([ks_ref[pl.ds(k0, KC), :], eblk_ref[pl.ds(k0, KC), :]], axis=1)

    def sel_consume(buf, mx, c):
        consume(buf, mx, v_chunk(vst_ref, c * KC), ms_sc, accs_sc)

    consume(buf_b, mx_b, v_chunk(vwt_ref, w_k0[1]), mw_sc, accw_sc)
    produce(buf_b, mx_b, sel_keys(tile), jnp.where(t0 + krow <= tcol, 0.0, NEG_INF))
    consume(buf_a, mx_a, v_chunk(vwt_ref, w_k0[0]), mw_sc, accw_sc)
    n_full = tile
    n_pairs = n_full // 2
    odd = n_full % 2 == 1

    def body(j, carry):
        produce(buf_a, mx_a, sel_keys(2 * j), None)
        sel_consume(buf_b, mx_b, jnp.where(j == 0, tile, 2 * j - 1))
        produce(buf_b, mx_b, sel_keys(2 * j + 1), None)
        sel_consume(buf_a, mx_a, 2 * j)
        return carry

    lax.fori_loop(0, n_pairs, body, 0)
    last_b = jnp.where(n_pairs == 0, tile, 2 * n_pairs - 1)

    @pl.when(odd)
    def _():
        produce(buf_a, mx_a, sel_keys(2 * n_pairs), None)

    sel_consume(buf_b, mx_b, last_b)

    @pl.when(odd)
    def _():
        sel_consume(buf_a, mx_a, 2 * n_pairs)

    for m in range(4):
        halves = []
        for g in range(N_GROUPS):
            r = 2 * m + g
            acc_s = accs_sc[:, groups[r]]
            acc_w = accw_sc[:, groups[r]]
            halves.append(out_sc[:, groups[r]]
                          + (gates[8 + r:9 + r] * (1.0 / acc_s[HEAD_DIM:HEAD_DIM + 1])) * acc_s[0:HEAD_DIM]
                          + (gates[16 + r:17 + r] * (1.0 / acc_w[HEAD_DIM:HEAD_DIM + 1])) * acc_w[0:HEAD_DIM])
        o_ref[:, m * LANES:(m + 1) * LANES] = jnp.concatenate(halves, axis=0).T.astype(BF16)


def _nsa(qt, ks, kw, vst, vwt, kvc, small_t, ovt, eblk, seq_len):
    n = ks.shape[0]
    nb = n // seq_len
    nq = seq_len // TQ
    nkb = seq_len // LANES
    seq = lambda b, i: (b, 0)
    return pl.pallas_call(
        _nsa_kernel,
        grid=(nb, nq),
        in_specs=[pl.BlockSpec((NSA_WIDTH, TQ), lambda b, i: (0, b * nq + i)),
                  pl.BlockSpec((seq_len, LANES), seq), pl.BlockSpec((seq_len, LANES), seq),
                  pl.BlockSpec((nkb, LANES, LANES), lambda b, i: (b, 0, 0)),
                  pl.BlockSpec((nkb, LANES, LANES), lambda b, i: (b, 0, 0)),
                  pl.BlockSpec((2 * LANES, LANES), seq),
                  pl.BlockSpec((N_SMALL, TQ), lambda b, i: (0, b * nq + i)),
                  pl.BlockSpec((LANES, LANES), lambda b, i: (0, 0)),
                  pl.BlockSpec((seq_len, LANES), lambda b, i: (0, 0))],
        out_specs=pl.BlockSpec((TQ, NSA_WIDTH), lambda b, i: (b * nq + i, 0)),
        out_shape=jax.ShapeDtypeStruct((n, NSA_WIDTH), BF16),
        scratch_shapes=[pltpu.VMEM((2 * LANES, ROWS), BF16),
                        pltpu.VMEM((2, KC, ROWS), F32),
                        pltpu.VMEM((2, 1, ROWS), F32),
                        pltpu.VMEM((1, ROWS), F32), pltpu.VMEM((1, ROWS), F32),
                        pltpu.VMEM((HEAD_DIM + ONES_ROWS, ROWS), F32),
                        pltpu.VMEM((HEAD_DIM + ONES_ROWS, ROWS), F32),
                        pltpu.VMEM((HEAD_DIM, ROWS), F32)],
        compiler_params=pltpu.CompilerParams(dimension_semantics=("arbitrary", "arbitrary"),
                                             vmem_limit_bytes=VMEM_LIMIT),
        name="nsa_attention",
    )(qt, ks, kw, vst, vwt, kvc, small_t, ovt, eblk)


def _nsa_constants(seq_len):
    n_cmp = (seq_len - CMP_BLOCK) // CMP_STRIDE + 1
    s = jnp.arange(LANES)[:, None]
    nn = jnp.arange(LANES)[None, :]
    cs = nn * CMP_STRIDE
    ss = s * SEL_BLOCK
    ovt = (cs < ss + SEL_BLOCK) & (cs + CMP_BLOCK > ss) & (s < seq_len // SEL_BLOCK) & (nn < n_cmp)
    k = jnp.arange(seq_len)[:, None]
    eblk = (k // SEL_BLOCK) == jnp.arange(LANES)[None, :]
    return ovt.astype(BF16), eblk.astype(BF16)


PAIR = 2 * GDN_CHUNK
N_DOUBLINGS = 5
GT_G = 4


def _gdn_kernel(q_ref, k_ref, v_ref, z_ref, small_ref, gt_ref, ng_ref, o_ref, s_sc):
    nseq = q_ref.shape[0]
    units = [(s, h) for s in range(nseq) for h in range(GDN_HEADS)]
    us = range(len(units))
    cols = [slice(h * GDN_DIM, (h + 1) * GDN_DIM) for _, h in units]
    ri = lax.broadcasted_iota(jnp.int32, (PAIR, PAIR), 0)
    ci = lax.broadcasted_iota(jnp.int32, (PAIR, PAIR), 1)
    same = lax.shift_right_logical(ri, 6) == lax.shift_right_logical(ci, 6)
    incl = same & (ri >= ci)
    strict = same & (ri > ci)
    first_cols = ci < GDN_CHUNK
    first_row = ci[0:1, :] < GDN_CHUNK
    ltri = jnp.where(incl, 1.0, 0.0).astype(BF16)
    tot = [jnp.where(first_cols, 1.0, 0.0).astype(BF16),
           jnp.where(first_cols, 0.0, 1.0).astype(BF16)]

    @pl.when(pl.program_id(1) == 0)
    def _():
        s_sc[...] = jnp.zeros(s_sc.shape, F32)

    sm_s = [small_ref[s] for s in range(nseq)]
    cs_s = [_dot3(ltri, y) for y in sm_s]
    gt_s = [gt_ref[s, 0] for s in range(nseq)]
    csr_s = [_dot3(ltri, y, nt=True) for y in gt_s]
    glast_s = [[_dot3(tot[c], y, nt=True) for c in range(2)] for y in gt_s]
    seq = [s for s, _ in units]
    q = [q_ref[seq[u], :, cols[u]] for u in us]
    k = [k_ref[seq[u], :, cols[u]] for u in us]
    kf = [y.astype(F32) for y in k]
    beta = [sm_s[s][:, SMALL_B + h:SMALL_B + h + 1] for s, h in units]
    gcc = [cs_s[s][:, SMALL_A + h:SMALL_A + h + 1] for s, h in units]
    gcr = [csr_s[s][GT_G + h:GT_G + h + 1, :] for s, h in units]
    glast = [[glast_s[s][c][GT_G + h:GT_G + h + 1, :] for c in range(2)] for s, h in units]
    decay = [jnp.exp(jnp.where(incl, gcc[u] - gcr[u], NEG_INF)) for u in us]
    egc = [jnp.exp(g) for g in gcc]
    kb = [kf[u] * beta[u] for u in us]
    kk = [_dot_nt(kb[u].astype(BF16), k[u]) for u in us]
    p = [jnp.where(strict, -(kk[u] * decay[u]), 0.0) for u in us]
    x = [jnp.concatenate([v_ref[seq[u], :, cols[u]].astype(F32) * beta[u], kb[u] * egc[u]], axis=1) for u in us]
    for i in range(N_DOUBLINGS + 1):
        pb = [y.astype(BF16) for y in p]
        x = [x[u] + _dot(pb[u], x[u].astype(BF16)) for u in us]
        if i < N_DOUBLINGS:
            p = [_dot(y, y) for y in pb]
    uw = [y.astype(BF16) for y in x]
    qk = [(_dot_nt(q[u], k[u]) * decay[u]).astype(BF16) for u in us]
    ke_t = [kf[u].T * jnp.exp(jnp.where(first_row, glast[u][0], glast[u][1]) - gcr[u]) for u in us]
    kw = [[_dot(jnp.where(first_cols, y, 0.0).astype(BF16), uw[u]) for u, y in enumerate(ke_t)],
          [_dot(jnp.where(first_cols, 0.0, y).astype(BF16), uw[u]) for u, y in enumerate(ke_t)]]
    qw = [_dot(qk[u], uw[u]) for u in us]
    qm = [(q[u].astype(F32) * egc[u] - qw[u][:, GDN_DIM:2 * GDN_DIM]).astype(BF16) for u in us]
    s = [s_sc[u] for u in us]
    os = []
    for c in range(2):
        half = slice(c * GDN_CHUNK, (c + 1) * GDN_CHUNK)
        sb = [y.astype(BF16) for y in s]
        os.append([_dot(qm[u][half], sb[u]) for u in us])
        ks = [_dot(kw[c][u][:, GDN_DIM:2 * GDN_DIM].astype(BF16), sb[u]) for u in us]
        s = [s[u] * jnp.exp(glast[u][c]) - ks[u] + kw[c][u][:, 0:GDN_DIM] for u in us]
    for u in us:
        s_sc[u] = s[u]
        o = jnp.concatenate([os[0][u], os[1][u]], axis=0) + qw[u][:, 0:GDN_DIM]
        o = o * lax.rsqrt(jnp.mean(o * o, axis=-1, keepdims=True) + RMS_EPS) * ng_ref[...]
        o_ref[seq[u], :, cols[u]] = (o * _silu(z_ref[seq[u], :, cols[u]].astype(F32))).astype(BF16)


GDN_SEQS = 4


def _gdn(gq, gk, gv, z, small, gt, norm_g, seq_len):
    n = gq.shape[0]
    nb = n // seq_len
    n_pairs = seq_len // PAIR
    nseq = max(d for d in range(1, GDN_SEQS + 1) if nb % d == 0)
    by_seq = lambda a: a.reshape(nb, seq_len, a.shape[-1])
    blk = lambda w: pl.BlockSpec((nseq, PAIR, w), lambda b, i: (b, i, 0))
    out = pl.pallas_call(
        _gdn_kernel,
        grid=(nb // nseq, n_pairs),
        in_specs=[blk(GDN_WIDTH), blk(GDN_WIDTH), blk(GDN_WIDTH), blk(GDN_WIDTH), blk(LANES),
                  pl.BlockSpec((nseq, 1, SUBLANES, PAIR), lambda b, i: (b, i, 0, 0)),
                  pl.BlockSpec((1, GDN_DIM), lambda b, i: (0, 0))],
        out_specs=blk(GDN_WIDTH),
        out_shape=jax.ShapeDtypeStruct((nb, seq_len, GDN_WIDTH), BF16),
        scratch_shapes=[pltpu.VMEM((nseq * GDN_HEADS, GDN_DIM, GDN_DIM), F32)],
        compiler_params=pltpu.CompilerParams(dimension_semantics=("arbitrary", "arbitrary"),
                                             vmem_limit_bytes=VMEM_LIMIT),
        name="gated_delta_rule",
    )(by_seq(gq), by_seq(gk), by_seq(gv), by_seq(z), by_seq(small),
      gt.reshape(nb, n_pairs, SUBLANES, PAIR), norm_g)
    return out.reshape(n, GDN_WIDTH)


FF_CHUNK = 256


def _rms(x, g):
    return x * lax.rsqrt(jnp.mean(x * x, axis=-1, keepdims=True) + RMS_EPS) * g


def _out_mlp_kernel(x_ref, on_ref, og_ref, gn_ref, wo_ref, g2_ref, wg_ref, wu_ref, wd_ref, gf_ref, out_ref,
                    act_sc):
    o_nsa = _rms(on_ref[...].astype(F32), gn_ref[...]).astype(BF16)
    mix = jnp.concatenate([o_nsa, og_ref[...]], axis=1)
    h = x_ref[...] + _dot(mix, wo_ref[...])
    hn = _rms(h, g2_ref[...]).astype(BF16)
    for c in range(D_FF // FF_CHUNK):
        cols = slice(c * FF_CHUNK, (c + 1) * FF_CHUNK)
        act_sc[:, cols] = (_silu(_dot(hn, wg_ref[:, cols])) * _dot(hn, wu_ref[:, cols])).astype(BF16)
    out_ref[...] = _rms(h + _dot(act_sc[...], wd_ref[...]), gf_ref[...])


def _out_mlp(x2, o_nsa, o_gdn, gn, wo, g2, wg, wu, wd, gf):
    n = x2.shape[0]
    row = lambda i: (i, 0)
    const = lambda i: (0, 0)
    resident = lambda shape: pl.BlockSpec(shape, const, pipeline_mode=pl.Buffered(1))
    return pl.pallas_call(
        _out_mlp_kernel,
        grid=(n // TM,),
        in_specs=[pl.BlockSpec((TM, D_MODEL), row), pl.BlockSpec((TM, NSA_WIDTH), row),
                  pl.BlockSpec((TM, GDN_WIDTH), row), pl.BlockSpec((1, NSA_WIDTH), const),
                  resident((D_MODEL, D_MODEL)), pl.BlockSpec((1, D_MODEL), const),
                  resident((D_MODEL, D_FF)), resident((D_MODEL, D_FF)), resident((D_FF, D_MODEL)),
                  pl.BlockSpec((1, D_MODEL), const)],
        out_specs=pl.BlockSpec((TM, D_MODEL), row),
        out_shape=jax.ShapeDtypeStruct((n, D_MODEL), F32),
        scratch_shapes=[pltpu.VMEM((TM, D_FF), BF16)],
        compiler_params=pltpu.CompilerParams(dimension_semantics=("arbitrary",),
                                             vmem_limit_bytes=VMEM_LIMIT),
        name="out_mlp",
    )(x2, o_nsa, o_gdn, gn, wo, g2, wg, wu, wd, gf)


def _nsa_head_perm():
    c = jnp.arange(NSA_WIDTH)
    return ((c // LANES) + 4 * ((c % LANES) // HEAD_DIM)) * HEAD_DIM + c % HEAD_DIM


def _layer(x2, pos_row, seq_len, norm1_g, w_in, cmp_pos, cmp_w1, cmp_w2, nsa_norm_g, gdn_conv_w, gdn_a_log,
           gdn_dt_bias, gdn_norm_g, w_out, norm2_g, w_gate, w_up, w_down, out_g):
    w_t, w_r, alog, dtb, alogt, dtbt = _prep_in_proj_weights(w_in, gdn_a_log, gdn_dt_bias)
    qt, cmp_k, ks, kw, vst, vwt, cmp_v, gq, gk, gv, z, small, small_t, gt = _in_proj(
        x2, pos_row, norm1_g[None, :].astype(F32), w_t, w_r, _rope_inv_freq(), gdn_conv_w.astype(F32),
        alog, dtb, alogt, dtbt, seq_len)
    kvc = _compress(cmp_k, cmp_v, *_prep_compress_weights(cmp_pos, cmp_w1, cmp_w2), seq_len)
    ovt, eblk = _nsa_constants(seq_len)
    o_nsa = _nsa(qt, ks, kw, vst, vwt, kvc, small_t, ovt, eblk, seq_len)
    o_gdn = _gdn(gq, gk, gv, z, small, gt, gdn_norm_g[None, :].astype(F32), seq_len)
    perm = _nsa_head_perm()
    wo = jnp.concatenate([w_out[:NSA_WIDTH][perm], w_out[NSA_WIDTH:]], axis=0).astype(BF16)
    return _out_mlp(x2, o_nsa, o_gdn, nsa_norm_g[perm][None, :].astype(F32), wo, norm2_g[None, :].astype(F32),
                    w_gate.astype(BF16), w_up.astype(BF16), w_down.astype(BF16), out_g[None, :].astype(F32))


def kernel(x, positions, norm1_g, w_in, cmp_pos, cmp_w1, cmp_w2, nsa_norm_g, gdn_conv_w, gdn_a_log,
           gdn_dt_bias, gdn_norm_g, w_out, norm2_g, w_gate, w_up, w_down, final_g):
    nb, seq_len, d = x.shape
    depth = w_in.shape[0]
    assert d == D_MODEL and seq_len % (IN_HALVES * TM) == 0 and seq_len // SEL_BLOCK == N_SEL and depth == 1
    x2 = x.reshape(nb * seq_len, d)
    pos_row = positions.reshape(1, nb * seq_len)
    out = _layer(x2, pos_row, seq_len, norm1_g[0], w_in[0], cmp_pos[0], cmp_w1[0], cmp_w2[0], nsa_norm_g[0],
                 gdn_conv_w[0], gdn_a_log[0], gdn_dt_bias[0], gdn_norm_g[0], w_out[0], norm2_g[0],
                 w_gate[0], w_up[0], w_down[0], final_g)
    return out.reshape(nb, seq_len, d)
```

```python
import functools
import math

import jax
import jax.numpy as jnp
from jax import lax
from jax.experimental import pallas as pl
from jax.experimental.pallas import tpu as pltpu

F32 = jnp.float32
BF16 = jnp.bfloat16

LANES = 128
SUBLANES = 8

D_MODEL = 1024
N_HEADS = 8
N_GROUPS = 2
HEAD_DIM = 64
CMP_BLOCK = 32
CMP_STRIDE = 16
CMP_HIDDEN = 128
SEL_BLOCK = 64
SEL_TOP_N = 8
WINDOW = 512
ROPE_THETA = 500000.0
ROPE_DIM = 16
GDN_HEADS = 4
GDN_DIM = 128
GDN_CONV = 4
GDN_CHUNK = 64
NSA_WIDTH = 512
GDN_WIDTH = 512
D_FF = 2816
RMS_EPS = 1e-6
NEG_INF = -1e30
MASK_BIAS = -1e9
FORCE_SCORE = 1e9

OFF_KV = 512
OFF_GATE = OFF_KV + 6 * 128
OFF_GQKV = OFF_GATE + 24
OFF_Z = OFF_GQKV + 3 * GDN_WIDTH
OFF_B = OFF_Z + GDN_WIDTH
OFF_A = OFF_B + GDN_HEADS

TR_Q = 0
TR_K = 512
TR_V = 896
TR_SMALL = 1152
TR_ROWS = 1184
TM_VCMP = 0
TM_GDN = 128
TM_Z = 1664
TM_SMALL = 2176
TM_COLS = 2304
SMALL_B = 24
SMALL_A = 28
N_SMALL = 32

Q_SCALE = HEAD_DIM ** -0.5 * math.log2(math.e)
TM = 512
VMEM_LIMIT = 56 * 1024 * 1024


def _dot(a, b):
    return jnp.dot(a, b, preferred_element_type=F32)


def _dot_nt(a, b):
    return lax.dot_general(a, b, (((1,), (1,)), ((), ())), preferred_element_type=F32)


def _sigmoid(x):
    return 1.0 / (1.0 + jnp.exp(-x))


def _silu(x):
    return x * _sigmoid(x)


def _softplus(x):
    return jnp.maximum(x, 0.0) + jnp.log(1.0 + jnp.exp(-jnp.abs(x)))


def _split3(x):
    h1 = x.astype(BF16)
    r1 = x - h1.astype(F32)
    h2 = r1.astype(BF16)
    h3 = (r1 - h2.astype(F32)).astype(BF16)
    return h1, h2, h3


def _dot3(mat01, x, nt=False):
    f = (lambda a: _dot_nt(a, mat01)) if nt else (lambda a: _dot(mat01, a))
    h1, h2, h3 = _split3(x)
    return f(h1) + f(h2) + f(h3)


def _rope_t(blk, cosv, sinv):
    half = ROPE_DIM // 2
    parts = []
    for h in range(blk.shape[0] // HEAD_DIM):
        b = h * HEAD_DIM
        x0 = blk[b:b + half]
        x1 = blk[b + half:b + ROPE_DIM]
        parts += [x0 * cosv - x1 * sinv, x1 * cosv + x0 * sinv, blk[b + ROPE_DIM:b + HEAD_DIM]]
    return jnp.concatenate(parts, axis=0)


def _in_proj_kernel(tiles_per_seq, x_ref, pos_ref, g1_ref, wt_ref, w_ref, invf_ref, convw_ref,
                    alog_ref, dtb_ref, alogt_ref, dtbt_ref,
                    qt_ref, kc_ref, ks_ref, kw_ref, vst_ref, vwt_ref, cmpv_ref,
                    gq_ref, gk_ref, gv_ref, z_ref, small_ref, smallt_ref, gt_ref,
                    cbuf):
    tm = x_ref.shape[0]
    nlb = tm // LANES
    x = x_ref[...]
    hn = x * lax.rsqrt(jnp.mean(x * x, axis=-1, keepdims=True) + RMS_EPS) * g1_ref[...]
    hb = hn.astype(BF16)

    first = pl.program_id(0) % tiles_per_seq == 0

    @pl.when(first)
    def _():
        cbuf[0:SUBLANES, :] = jnp.zeros((SUBLANES, 3 * GDN_WIDTH), F32)

    proj = {}

    def gdn_piece(c):
        lo = TM_GDN + 2 * c * LANES
        cbuf[SUBLANES:SUBLANES + tm, 2 * c * LANES:2 * (c + 1) * LANES] = _dot(hb, w_ref[:, lo:lo + 2 * LANES])

    def t_piece(name, lo, hi):
        proj[name] = _dot_nt(wt_ref[lo:hi, :], hb)

    def tm_piece(name, lo, hi):
        proj[name] = _dot(hb, w_ref[:, lo:hi])

    later = [lambda: gdn_piece(2), lambda: t_piece("q01", TR_Q, TR_Q + 2 * LANES),
             lambda: gdn_piece(3), lambda: t_piece("q23", TR_Q + 2 * LANES, TR_K),
             lambda: gdn_piece(4), lambda: t_piece("k", TR_K, TR_V),
             lambda: gdn_piece(5), lambda: t_piece("v", TR_V, TR_ROWS),
             lambda: tm_piece("z0", TM_Z, TM_Z + 2 * LANES), lambda: tm_piece("z1", TM_Z + 2 * LANES, TM_SMALL),
             lambda: tm_piece("vcmp", TM_VCMP, TM_VCMP + LANES), lambda: tm_piece("small", TM_SMALL, TM_COLS)]
    gdn_piece(0)
    gdn_piece(1)

    ones_sq = jnp.ones((LANES, LANES), BF16)

    def conv_block(cb):
        cols = slice(cb * LANES, (cb + 1) * LANES)
        zx = cbuf[0:SUBLANES + tm, cols]
        y = zx * convw_ref[GDN_CONV - 1:GDN_CONV, cols]
        for s in range(1, GDN_CONV):
            y = y + pltpu.roll(zx, s, 0) * convw_ref[GDN_CONV - 1 - s:GDN_CONV - s, cols]
        return _silu(y[SUBLANES:])

    def finish_block(cb, y):
        cols = slice((cb % 4) * LANES, (cb % 4 + 1) * LANES)
        if cb < 8:
            y = y * lax.rsqrt(_dot((y * y).astype(BF16), ones_sq) + RMS_EPS)
        if cb < 4:
            gq_ref[:, cols] = (y * (GDN_DIM ** -0.5)).astype(BF16)
        elif cb < 8:
            gk_ref[:, cols] = y.astype(BF16)
        else:
            gv_ref[:, cols] = y.astype(BF16)

    pending = None
    for cb in range(12):
        later[cb]()
        y = conv_block(cb)
        if pending is not None:
            finish_block(*pending)
        pending = (cb, y)
    finish_block(*pending)
    cbuf[0:SUBLANES, :] = cbuf[tm:tm + SUBLANES, :]

    yt = jnp.concatenate([proj["q01"], proj["q23"], proj["k"], proj["v"]], axis=0)
    ang = invf_ref[...] * pos_ref[...].astype(F32)
    cosv = jnp.cos(ang)
    sinv = jnp.sin(ang)
    for m in range(4):
        blk = _rope_t(yt[TR_Q + m * LANES:TR_Q + (m + 1) * LANES], cosv, sinv)
        qt_ref[m * LANES:(m + 1) * LANES, :] = (blk * Q_SCALE).astype(BF16)
    for j, ref in enumerate((kc_ref, ks_ref, kw_ref)):
        blk = _rope_t(yt[TR_K + j * LANES:TR_K + (j + 1) * LANES], cosv, sinv)
        for c in range(nlb):
            ref[c * LANES:(c + 1) * LANES, :] = blk[:, c * LANES:(c + 1) * LANES].T.astype(ref.dtype)
    for j, ref in enumerate((vst_ref, vwt_ref)):
        blk = yt[TR_V + j * LANES:TR_V + (j + 1) * LANES].astype(BF16)
        for c in range(nlb):
            ref[c] = blk[:, c * LANES:(c + 1) * LANES]
    st = yt[TR_SMALL:TR_SMALL + N_SMALL]
    srow = lax.broadcasted_iota(jnp.int32, (N_SMALL, 1), 0)
    gdec_t = -jnp.exp(alogt_ref[...]) * _softplus(st + dtbt_ref[...])
    small_t = jnp.where(srow < SMALL_A, _sigmoid(st), gdec_t)
    smallt_ref[...] = small_t
    for c in range(nlb):
        gt_ref[c] = small_t[SMALL_B:N_SMALL, c * LANES:(c + 1) * LANES]

    cmpv_ref[...] = proj["vcmp"]
    z_ref[:, 0:2 * LANES] = proj["z0"].astype(BF16)
    z_ref[:, 2 * LANES:GDN_WIDTH] = proj["z1"].astype(BF16)

    sm = proj["small"]
    lane1 = lax.broadcasted_iota(jnp.int32, (1, LANES), 1)
    gdec = -jnp.exp(alog_ref[...]) * _softplus(sm + dtb_ref[...])
    small_ref[...] = jnp.where(lane1 < SMALL_A, _sigmoid(sm), gdec)


def _in_proj(x2, pos_row, g1, w_t, w_r, invf, convw, alog, dtb, alogt, dtbt, seq_len):
    n = x2.shape[0]
    row = lambda i: (i, 0)
    colb = lambda i: (0, i)
    lead = lambda i: (i, 0, 0)
    const = lambda i: (0, 0)
    nlb = TM // LANES
    out_shapes = (
        jax.ShapeDtypeStruct((NSA_WIDTH, n), BF16),
        jax.ShapeDtypeStruct((n, LANES), F32),
        jax.ShapeDtypeStruct((n, LANES), BF16),
        jax.ShapeDtypeStruct((n, LANES), BF16),
        jax.ShapeDtypeStruct((n // LANES, LANES, LANES), BF16),
        jax.ShapeDtypeStruct((n // LANES, LANES, LANES), BF16),
        jax.ShapeDtypeStruct((n, LANES), F32),
        jax.ShapeDtypeStruct((n, GDN_WIDTH), BF16),
        jax.ShapeDtypeStruct((n, GDN_WIDTH), BF16),
        jax.ShapeDtypeStruct((n, GDN_WIDTH), BF16),
        jax.ShapeDtypeStruct((n, GDN_WIDTH), BF16),
        jax.ShapeDtypeStruct((n, LANES), F32),
        jax.ShapeDtypeStruct((N_SMALL, n), F32),
        jax.ShapeDtypeStruct((n // LANES, SUBLANES, LANES), F32),
    )
    out_specs = (
        pl.BlockSpec((NSA_WIDTH, TM), colb),
        pl.BlockSpec((TM, LANES), row), pl.BlockSpec((TM, LANES), row), pl.BlockSpec((TM, LANES), row),
        pl.BlockSpec((nlb, LANES, LANES), lead), pl.BlockSpec((nlb, LANES, LANES), lead),
        pl.BlockSpec((TM, LANES), row),
        pl.BlockSpec((TM, GDN_WIDTH), row), pl.BlockSpec((TM, GDN_WIDTH), row),
        pl.BlockSpec((TM, GDN_WIDTH), row), pl.BlockSpec((TM, GDN_WIDTH), row),
        pl.BlockSpec((TM, LANES), row),
        pl.BlockSpec((N_SMALL, TM), colb),
        pl.BlockSpec((nlb, SUBLANES, LANES), lead),
    )
    in_specs = [
        pl.BlockSpec((TM, D_MODEL), row), pl.BlockSpec((1, TM), colb), pl.BlockSpec((1, D_MODEL), const),
        pl.BlockSpec((TR_ROWS, D_MODEL), const), pl.BlockSpec((D_MODEL, TM_COLS), const),
        pl.BlockSpec((ROPE_DIM // 2, 1), const), pl.BlockSpec((GDN_CONV, 3 * GDN_WIDTH), const),
        pl.BlockSpec((1, LANES), const), pl.BlockSpec((1, LANES), const),
        pl.BlockSpec((N_SMALL, 1), const), pl.BlockSpec((N_SMALL, 1), const),
    ]
    return pl.pallas_call(
        functools.partial(_in_proj_kernel, seq_len // TM),
        grid=(n // TM,), in_specs=in_specs, out_specs=out_specs, out_shape=out_shapes,
        scratch_shapes=[pltpu.VMEM((TM + 2 * SUBLANES, 3 * GDN_WIDTH), F32)],
        compiler_params=pltpu.CompilerParams(dimension_semantics=("arbitrary",),
                                             vmem_limit_bytes=VMEM_LIMIT),
        name="in_proj",
    )(x2, pos_row, g1, w_t, w_r, invf, convw, alog, dtb, alogt, dtbt)


def _prep_in_proj_weights(w_in, gdn_a_log, gdn_dt_bias):
    q = w_in[:, :NSA_WIDTH].reshape(D_MODEL, 2, 4, HEAD_DIM)
    q = jnp.transpose(q, (0, 2, 1, 3)).reshape(D_MODEL, NSA_WIDTH)
    kv = w_in[:, OFF_KV:OFF_GATE].reshape(D_MODEL, 6, LANES)
    gate = w_in[:, OFF_GATE:OFF_GQKV].reshape(D_MODEL, 2, 4, 3)
    gate = jnp.transpose(gate, (0, 3, 2, 1)).reshape(D_MODEL, 24)
    small = jnp.concatenate([gate, w_in[:, OFF_B:OFF_A], w_in[:, OFF_A:OFF_A + GDN_HEADS]], axis=1)
    w_t = jnp.concatenate([q, kv[:, 0], kv[:, 2], kv[:, 4], kv[:, 3], kv[:, 5], small], axis=1).T.astype(BF16)
    w_r = jnp.concatenate([kv[:, 1], w_in[:, OFF_GQKV:OFF_Z], w_in[:, OFF_Z:OFF_B], small,
                           jnp.zeros((D_MODEL, LANES - N_SMALL), w_in.dtype)], axis=1).astype(BF16)
    alog = jnp.zeros((LANES,), F32).at[SMALL_A:SMALL_A + GDN_HEADS].set(gdn_a_log.astype(F32))
    dtb = jnp.zeros((LANES,), F32).at[SMALL_A:SMALL_A + GDN_HEADS].set(gdn_dt_bias.astype(F32))
    return w_t, w_r, alog[None, :], dtb[None, :], alog[:N_SMALL, None], dtb[:N_SMALL, None]


def _rope_inv_freq():
    half = ROPE_DIM // 2
    return jnp.power(ROPE_THETA, -jnp.arange(half, dtype=F32) * (2.0 / ROPE_DIM))[:, None]


def _compress_kernel(xk_ref, xv_ref, pos_ref, w1_ref, w2_ref, out_ref):
    nblk = xk_ref.shape[0] // CMP_STRIDE
    acc_lo = jnp.zeros((nblk, 4 * CMP_HIDDEN), F32)
    acc_hi = jnp.zeros((nblk, 4 * CMP_HIDDEN), F32)
    for j in range(CMP_STRIDE):
        xj = jnp.concatenate([xk_ref[pl.ds(j, nblk, stride=CMP_STRIDE), :],
                              xv_ref[pl.ds(j, nblk, stride=CMP_STRIDE), :]], axis=1)
        acc_lo = acc_lo + _dot((xj + pos_ref[j:j + 1, :]).astype(BF16), w1_ref[j])
        acc_hi = acc_hi + _dot((xj + pos_ref[CMP_STRIDE + j:CMP_STRIDE + j + 1, :]).astype(BF16),
                               w1_ref[CMP_STRIDE + j])
    pre = acc_lo + pltpu.roll(acc_hi, nblk - 1, 0)
    kvc = _dot(_silu(pre).astype(BF16), w2_ref[...])
    out_ref[0:nblk, :] = kvc[:, 0:LANES]
    out_ref[nblk:2 * nblk, :] = kvc[:, LANES:2 * LANES].T


def _compress(cmp_k, cmp_v, pos_rows, w1_bd, w2_bd, seq_len):
    n = cmp_k.shape[0]
    nb = n // seq_len
    nblk = seq_len // CMP_STRIDE
    assert nblk == LANES
    return pl.pallas_call(
        _compress_kernel,
        grid=(nb,),
        in_specs=[pl.BlockSpec((seq_len, LANES), lambda b: (b, 0)),
                  pl.BlockSpec((seq_len, LANES), lambda b: (b, 0)),
                  pl.BlockSpec((CMP_BLOCK, 256), lambda b: (0, 0)),
                  pl.BlockSpec((CMP_BLOCK, 256, 4 * CMP_HIDDEN), lambda b: (0, 0, 0)),
                  pl.BlockSpec((4 * CMP_HIDDEN, 256), lambda b: (0, 0))],
        out_specs=pl.BlockSpec((2 * nblk, LANES), lambda b: (b, 0)),
        out_shape=jax.ShapeDtypeStruct((nb * 2 * nblk, LANES), F32),
        compiler_params=pltpu.CompilerParams(dimension_semantics=("arbitrary",),
                                             vmem_limit_bytes=VMEM_LIMIT),
        name="nsa_compress",
    )(cmp_k, cmp_v, pos_rows, w1_bd, w2_bd)


def _prep_compress_weights(cmp_pos, cmp_w1, cmp_w2):
    slot_src = jnp.array([0, 0, 1, 1])
    eye = jnp.eye(4, dtype=BF16)
    w1 = cmp_w1.reshape(2, CMP_BLOCK, HEAD_DIM, CMP_HIDDEN).astype(BF16)[slot_src]
    w1 = jnp.transpose(w1, (1, 0, 2, 3)).reshape(CMP_BLOCK, 4 * HEAD_DIM, CMP_HIDDEN)
    diag = jnp.repeat(jnp.repeat(eye, HEAD_DIM, axis=0), CMP_HIDDEN, axis=1)
    w1_bd = jnp.tile(w1, (1, 1, 4)) * diag[None]
    w2_bd = cmp_w2.astype(BF16)[slot_src][:, :, None, :] * eye[:, None, :, None]
    w2_bd = w2_bd.reshape(4 * CMP_HIDDEN, 256)
    pos_rows = jnp.concatenate([cmp_pos[0], cmp_pos[0], cmp_pos[1], cmp_pos[1]], axis=-1).astype(F32)
    return pos_rows, w1_bd, w2_bd


TQ = 256
KC = 256
N_SEL = 32
ROWS = N_HEADS * TQ
WCHUNKS = WINDOW // KC + 1
ONES_ROWS = 16


def _nsa_kernel(qt_ref, ks_ref, kw_ref, vst_ref, vwt_ref, kvc_ref, gt_ref, ovt_ref, eblk_ref, o_ref,
                qaug, sbuf, mxbuf, ms_sc, mw_sc, accs_sc, accw_sc, out_sc):
    assert WCHUNKS == 3
    tile = pl.program_id(1)
    t0 = tile * TQ
    tcol = t0 + lax.broadcasted_iota(jnp.int32, (1, TQ), 1)
    krow = lax.broadcasted_iota(jnp.int32, (KC, 1), 0)
    ones = jnp.ones((ONES_ROWS, KC), BF16)
    groups = [slice(r * TQ, (r + 1) * TQ) for r in range(N_HEADS)]
    blocks_per_chunk = KC // LANES

    zhalf = jnp.zeros((HEAD_DIM, TQ), BF16)
    for m in range(4):
        blk = qt_ref[m * LANES:(m + 1) * LANES, :]
        qaug[0:LANES, groups[2 * m]] = jnp.concatenate([blk[0:HEAD_DIM], zhalf], axis=0)
        qaug[0:LANES, groups[2 * m + 1]] = jnp.concatenate([zhalf, blk[HEAD_DIM:LANES]], axis=0)

    def v_chunk(vt_ref, k0):
        b0 = k0 // LANES
        vt = jnp.concatenate([vt_ref[b0 + j] for j in range(blocks_per_chunk)], axis=1)
        return [jnp.concatenate([vt[g * HEAD_DIM:(g + 1) * HEAD_DIM], ones], axis=0) for g in range(N_GROUPS)]

    buf_a, buf_b = sbuf.at[0], sbuf.at[1]
    mx_a, mx_b = mxbuf.at[0], mxbuf.at[1]

    def produce(buf, mx, kmat, bias):
        qrows = kmat.shape[1]
        for cols in groups:
            s = _dot(kmat, qaug[0:qrows, cols])
            if bias is not None:
                s = s + bias
            buf[:, cols] = s
            mx[:, cols] = jnp.max(s, axis=0, keepdims=True)

    def consume(buf, mx, vt, m_ref, acc_ref):
        for r, cols in enumerate(groups):
            m_old = m_ref[:, cols]
            m_new = jnp.maximum(m_old, mx[:, cols])
            p = jnp.exp2((buf[:, cols] - m_new).astype(BF16))
            m_ref[:, cols] = m_new
            acc_ref[:, cols] = acc_ref[:, cols] * jnp.exp2(m_old - m_new) + _dot(vt[r % N_GROUPS], p)

    def produce_consume(pbuf, pmx, kmat, bias, cbuf, cmx, vt, m_ref, acc_ref):
        qrows = kmat.shape[1]
        for r, cols in enumerate(groups):
            s = _dot(kmat, qaug[0:qrows, cols])
            if bias is not None:
                s = s + bias
            pbuf[:, cols] = s
            pmx[:, cols] = jnp.max(s, axis=0, keepdims=True)
            m_old = m_ref[:, cols]
            m_new = jnp.maximum(m_old, cmx[:, cols])
            p = jnp.exp2((cbuf[:, cols] - m_new).astype(BF16))
            m_ref[:, cols] = m_new
            acc_ref[:, cols] = acc_ref[:, cols] * jnp.exp2(m_old - m_new) + _dot(vt[r % N_GROUPS], p)

    mw_sc[...] = jnp.full(mw_sc.shape, NEG_INF, F32)
    accw_sc[...] = jnp.zeros(accw_sc.shape, F32)

    def window_chunk(j):
        start = t0 - WINDOW + j * KC
        k0 = pl.multiple_of(jnp.maximum(start, 0), KC)
        kpos = start + krow
        diff = tcol - kpos
        bias = jnp.where((kpos >= 0) & (diff >= 0) & (diff < WINDOW), 0.0, NEG_INF)
        return k0, kw_ref[pl.ds(k0, KC), :], bias

    w_k0 = [None] * WCHUNKS
    w_k0[2], kwin, wbias = window_chunk(2)
    produce(buf_a, mx_a, kwin, wbias)
    w_k0[1], kwin, wbias = window_chunk(1)
    produce_consume(buf_b, mx_b, kwin, wbias, buf_a, mx_a, v_chunk(vwt_ref, w_k0[2]), mw_sc, accw_sc)

    nblk = kvc_ref.shape[0] // 2
    kc = kvc_ref[0:nblk, :].astype(BF16)
    vct = kvc_ref[nblk:2 * nblk, :].astype(BF16)
    nrow = lax.broadcasted_iota(jnp.int32, (nblk, 1), 0)
    vbias = jnp.where(nrow * CMP_STRIDE + (CMP_BLOCK - 1) <= tcol, 0.0, NEG_INF)
    has_any = tcol >= CMP_BLOCK - 1
    s_c = [_dot(kc, qaug[0:LANES, cols]) + vbias for cols in groups]
    e_c = [jnp.exp2(y - jnp.max(y, axis=0, keepdims=True)) for y in s_c]
    p_c = [jnp.where(has_any, y * (1.0 / jnp.sum(y, axis=0, keepdims=True)), 0.0) for y in e_c]
    gates = gt_ref[...]
    for r, cols in enumerate(groups):
        g = r % N_GROUPS
        out_sc[:, cols] = gates[r:r + 1] * _dot(vct[g * HEAD_DIM:(g + 1) * HEAD_DIM], p_c[r].astype(BF16))

    w_k0[0], kwin, wbias = window_chunk(0)
    produce_consume(buf_a, mx_a, kwin, wbias, buf_b, mx_b, v_chunk(vwt_ref, w_k0[1]), mw_sc, accw_sc)

    jrow = lax.broadcasted_iota(jnp.int32, (N_SEL, 1), 0)
    cur = lax.shift_right_logical(tcol, 6)
    forced = (jrow == 0) | (jrow == cur) | (jrow == cur - 1)
    causal = jrow <= cur
    for g in range(N_GROUPS):
        psum = (p_c[g] + p_c[2 + g]) + (p_c[4 + g] + p_c[6 + g])
        imp_t = _dot3(ovt_ref[...], psum)
        score = jnp.where(forced, FORCE_SCORE, jnp.where(causal, imp_t[0:N_SEL, :], NEG_INF))
        cnt = jnp.zeros((N_SEL, TQ), jnp.int32)
        for jp in range(N_SEL):
            rowv = score[jp:jp + 1, :]
            beats = (rowv > score) | ((rowv == score) & (jrow > jp))
            cnt = cnt + jnp.where(beats, 1, 0)
        sel = (cnt < SEL_TOP_N) & causal
        bias = jnp.concatenate([jnp.where(sel, 0.0, MASK_BIAS),
                                jnp.zeros((LANES - N_SEL, TQ), F32)], axis=0).astype(BF16)
        for m in range(4):
            qaug[LANES:2 * LANES, groups[2 * m + g]] = bias

    ms_sc[...] = jnp.full(ms_sc.shape, NEG_INF, F32)
    accs_sc[...] = jnp.zeros(accs_sc.shape, F32)

    def sel_keys(c):
        k0 = pl.multiple_of(c * KC, KC)
        return jnp.concatenate([ks_ref[pl.ds(k0, KC), :], eblk_ref[pl.ds(k0, KC), :]], axis=1)

    def sel_consume(buf, mx, c):
        consume(buf, mx, v_chunk(vst_ref, c * KC), ms_sc, accs_sc)

    produce_consume(buf_b, mx_b, sel_keys(tile), jnp.where(t0 + krow <= tcol, 0.0, NEG_INF),
                    buf_a, mx_a, v_chunk(vwt_ref, w_k0[0]), mw_sc, accw_sc)
    n_full = tile
    n_pairs = n_full // 2
    odd = n_full % 2 == 1

    def sel_produce_consume(pbuf, pmx, c_new, cbuf, cmx, c_old):
        produce_consume(pbuf, pmx, sel_keys(c_new), None, cbuf, cmx, v_chunk(vst_ref, c_old * KC), ms_sc, accs_sc)

    def body(j, carry):
        sel_produce_consume(buf_a, mx_a, 2 * j, buf_b, mx_b, jnp.where(j == 0, tile, 2 * j - 1))
        sel_produce_consume(buf_b, mx_b, 2 * j + 1, buf_a, mx_a, 2 * j)
        return carry

    lax.fori_loop(0, n_pairs, body, 0)
    last_b = jnp.where(n_pairs == 0, tile, 2 * n_pairs - 1)

    @pl.when(odd)
    def _():
        sel_produce_consume(buf_a, mx_a, 2 * n_pairs, buf_b, mx_b, last_b)
        sel_consume(buf_a, mx_a, 2 * n_pairs)

    @pl.when(jnp.logical_not(odd))
    def _():
        sel_consume(buf_b, mx_b, last_b)

    for m in range(4):
        halves = []
        for g in range(N_GROUPS):
            r = 2 * m + g
            acc_s = accs_sc[:, groups[r]]
            acc_w = accw_sc[:, groups[r]]
            halves.append(out_sc[:, groups[r]]
                          + (gates[8 + r:9 + r] * (1.0 / acc_s[HEAD_DIM:HEAD_DIM + 1])) * acc_s[0:HEAD_DIM]
                          + (gates[16 + r:17 + r] * (1.0 / acc_w[HEAD_DIM:HEAD_DIM + 1])) * acc_w[0:HEAD_DIM])
        o_ref[:, m * LANES:(m + 1) * LANES] = jnp.concatenate(halves, axis=0).T.astype(BF16)


def _nsa(qt, ks, kw, vst, vwt, kvc, small_t, ovt, eblk, seq_len):
    n = ks.shape[0]
    nb = n // seq_len
    nq = seq_len // TQ
    nkb = seq_len // LANES
    seq = lambda b, i: (b, 0)
    return pl.pallas_call(
        _nsa_kernel,
        grid=(nb, nq),
        in_specs=[pl.BlockSpec((NSA_WIDTH, TQ), lambda b, i: (0, b * nq + i)),
                  pl.BlockSpec((seq_len, LANES), seq), pl.BlockSpec((seq_len, LANES), seq),
                  pl.BlockSpec((nkb, LANES, LANES), lambda b, i: (b, 0, 0)),
                  pl.BlockSpec((nkb, LANES, LANES), lambda b, i: (b, 0, 0)),
                  pl.BlockSpec((2 * LANES, LANES), seq),
                  pl.BlockSpec((N_SMALL, TQ), lambda b, i: (0, b * nq + i)),
                  pl.BlockSpec((LANES, LANES), lambda b, i: (0, 0)),
                  pl.BlockSpec((seq_len, LANES), lambda b, i: (0, 0))],
        out_specs=pl.BlockSpec((TQ, NSA_WIDTH), lambda b, i: (b * nq + i, 0)),
        out_shape=jax.ShapeDtypeStruct((n, NSA_WIDTH), BF16),
        scratch_shapes=[pltpu.VMEM((2 * LANES, ROWS), BF16),
                        pltpu.VMEM((2, KC, ROWS), F32),
                        pltpu.VMEM((2, 1, ROWS), F32),
                        pltpu.VMEM((1, ROWS), F32), pltpu.VMEM((1, ROWS), F32),
                        pltpu.VMEM((HEAD_DIM + ONES_ROWS, ROWS), F32),
                        pltpu.VMEM((HEAD_DIM + ONES_ROWS, ROWS), F32),
                        pltpu.VMEM((HEAD_DIM, ROWS), F32)],
        compiler_params=pltpu.CompilerParams(dimension_semantics=("arbitrary", "arbitrary"),
                                             vmem_limit_bytes=VMEM_LIMIT),
        name="nsa_attention",
    )(qt, ks, kw, vst, vwt, kvc, small_t, ovt, eblk)


def _nsa_constants(seq_len):
    n_cmp = (seq_len - CMP_BLOCK) // CMP_STRIDE + 1
    s = jnp.arange(LANES)[:, None]
    nn = jnp.arange(LANES)[None, :]
    cs = nn * CMP_STRIDE
    ss = s * SEL_BLOCK
    ovt = (cs < ss + SEL_BLOCK) & (cs + CMP_BLOCK > ss) & (s < seq_len // SEL_BLOCK) & (nn < n_cmp)
    k = jnp.arange(seq_len)[:, None]
    eblk = (k // SEL_BLOCK) == jnp.arange(LANES)[None, :]
    return ovt.astype(BF16), eblk.astype(BF16)


PAIR = 2 * GDN_CHUNK
N_DOUBLINGS = 5
GT_G = 4


def _gdn_kernel(q_ref, k_ref, v_ref, z_ref, small_ref, gt_ref, ng_ref, o_ref, s_sc):
    nseq = q_ref.shape[0]
    units = [(s, h) for s in range(nseq) for h in range(GDN_HEADS)]
    us = range(len(units))
    cols = [slice(h * GDN_DIM, (h + 1) * GDN_DIM) for _, h in units]
    ri = lax.broadcasted_iota(jnp.int32, (PAIR, PAIR), 0)
    ci = lax.broadcasted_iota(jnp.int32, (PAIR, PAIR), 1)
    same = lax.shift_right_logical(ri, 6) == lax.shift_right_logical(ci, 6)
    incl = same & (ri >= ci)
    strict = same & (ri > ci)
    first_cols = ci < GDN_CHUNK
    first_row = ci[0:1, :] < GDN_CHUNK
    ltri = jnp.where(incl, 1.0, 0.0).astype(BF16)
    tot = [jnp.where(first_cols, 1.0, 0.0).astype(BF16),
           jnp.where(first_cols, 0.0, 1.0).astype(BF16)]

    @pl.when(pl.program_id(1) == 0)
    def _():
        s_sc[...] = jnp.zeros(s_sc.shape, F32)

    sm_s = [small_ref[s] for s in range(nseq)]
    cs_s = [_dot3(ltri, y) for y in sm_s]
    gt_s = [gt_ref[s, 0] for s in range(nseq)]
    csr_s = [_dot3(ltri, y, nt=True) for y in gt_s]
    glast_s = [[_dot3(tot[c], y, nt=True) for c in range(2)] for y in gt_s]
    seq = [s for s, _ in units]
    q = [q_ref[seq[u], :, cols[u]] for u in us]
    k = [k_ref[seq[u], :, cols[u]] for u in us]
    kf = [y.astype(F32) for y in k]
    beta = [sm_s[s][:, SMALL_B + h:SMALL_B + h + 1] for s, h in units]
    gcc = [cs_s[s][:, SMALL_A + h:SMALL_A + h + 1] for s, h in units]
    gcr = [csr_s[s][GT_G + h:GT_G + h + 1, :] for s, h in units]
    glast = [[glast_s[s][c][GT_G + h:GT_G + h + 1, :] for c in range(2)] for s, h in units]
    decay = [jnp.exp(jnp.where(incl, gcc[u] - gcr[u], NEG_INF)) for u in us]
    egc = [jnp.exp(g) for g in gcc]
    kb = [kf[u] * beta[u] for u in us]
    kk = [_dot_nt(kb[u].astype(BF16), k[u]) for u in us]
    p = [jnp.where(strict, -(kk[u] * decay[u]), 0.0) for u in us]
    x = [jnp.concatenate([v_ref[seq[u], :, cols[u]].astype(F32) * beta[u], kb[u] * egc[u]], axis=1) for u in us]
    for i in range(N_DOUBLINGS + 1):
        pb = [y.astype(BF16) for y in p]
        x = [x[u] + _dot(pb[u], x[u].astype(BF16)) for u in us]
        if i < N_DOUBLINGS:
            p = [_dot(y, y) for y in pb]
    uw = [y.astype(BF16) for y in x]
    qk = [(_dot_nt(q[u], k[u]) * decay[u]).astype(BF16) for u in us]
    ke_t = [kf[u].T * jnp.exp(jnp.where(first_row, glast[u][0], glast[u][1]) - gcr[u]) for u in us]
    kw = [[_dot(jnp.where(first_cols, y, 0.0).astype(BF16), uw[u]) for u, y in enumerate(ke_t)],
          [_dot(jnp.where(first_cols, 0.0, y).astype(BF16), uw[u]) for u, y in enumerate(ke_t)]]
    qw = [_dot(qk[u], uw[u]) for u in us]
    qm = [(q[u].astype(F32) * egc[u] - qw[u][:, GDN_DIM:2 * GDN_DIM]).astype(BF16) for u in us]
    s = [s_sc[u] for u in us]
    os = []
    for c in range(2):
        half = slice(c * GDN_CHUNK, (c + 1) * GDN_CHUNK)
        sb = [y.astype(BF16) for y in s]
        os.append([_dot(qm[u][half], sb[u]) for u in us])
        ks = [_dot(kw[c][u][:, GDN_DIM:2 * GDN_DIM].astype(BF16), sb[u]) for u in us]
        s = [s[u] * jnp.exp(glast[u][c]) - ks[u] + kw[c][u][:, 0:GDN_DIM] for u in us]
    for u in us:
        s_sc[u] = s[u]
        o = jnp.concatenate([os[0][u], os[1][u]], axis=0) + qw[u][:, 0:GDN_DIM]
        o = o * lax.rsqrt(jnp.mean(o * o, axis=-1, keepdims=True) + RMS_EPS) * ng_ref[...]
        o_ref[seq[u], :, cols[u]] = (o * _silu(z_ref[seq[u], :, cols[u]].astype(F32))).astype(BF16)


GDN_SEQS = 4


def _gdn(gq, gk, gv, z, small, gt, norm_g, seq_len):
    n = gq.shape[0]
    nb = n // seq_len
    n_pairs = seq_len // PAIR
    nseq = max(d for d in range(1, GDN_SEQS + 1) if nb % d == 0)
    by_seq = lambda a: a.reshape(nb, seq_len, a.shape[-1])
    blk = lambda w: pl.BlockSpec((nseq, PAIR, w), lambda b, i: (b, i, 0))
    out = pl.pallas_call(
        _gdn_kernel,
        grid=(nb // nseq, n_pairs),
        in_specs=[blk(GDN_WIDTH), blk(GDN_WIDTH), blk(GDN_WIDTH), blk(GDN_WIDTH), blk(LANES),
                  pl.BlockSpec((nseq, 1, SUBLANES, PAIR), lambda b, i: (b, i, 0, 0)),
                  pl.BlockSpec((1, GDN_DIM), lambda b, i: (0, 0))],
        out_specs=blk(GDN_WIDTH),
        out_shape=jax.ShapeDtypeStruct((nb, seq_len, GDN_WIDTH), BF16),
        scratch_shapes=[pltpu.VMEM((nseq * GDN_HEADS, GDN_DIM, GDN_DIM), F32)],
        compiler_params=pltpu.CompilerParams(dimension_semantics=("arbitrary", "arbitrary"),
                                             vmem_limit_bytes=VMEM_LIMIT),
        name="gated_delta_rule",
    )(by_seq(gq), by_seq(gk), by_seq(gv), by_seq(z), by_seq(small),
      gt.reshape(nb, n_pairs, SUBLANES, PAIR), norm_g)
    return out.reshape(n, GDN_WIDTH)


FF_CHUNK = 256


def _rms(x, g):
    return x * lax.rsqrt(jnp.mean(x * x, axis=-1, keepdims=True) + RMS_EPS) * g


def _out_mlp_kernel(x_ref, on_ref, og_ref, gn_ref, wo_ref, g2_ref, wg_ref, wu_ref, wd_ref, gf_ref, out_ref,
                    act_sc):
    o_nsa = _rms(on_ref[...].astype(F32), gn_ref[...]).astype(BF16)
    mix = jnp.concatenate([o_nsa, og_ref[...]], axis=1)
    h = x_ref[...] + _dot(mix, wo_ref[...])
    hn = _rms(h, g2_ref[...]).astype(BF16)
    for c in range(D_FF // FF_CHUNK):
        cols = slice(c * FF_CHUNK, (c + 1) * FF_CHUNK)
        act_sc[:, cols] = (_silu(_dot(hn, wg_ref[:, cols])) * _dot(hn, wu_ref[:, cols])).astype(BF16)
    out_ref[...] = _rms(h + _dot(act_sc[...], wd_ref[...]), gf_ref[...])


def _out_mlp(x2, o_nsa, o_gdn, gn, wo, g2, wg, wu, wd, gf):
    n = x2.shape[0]
    row = lambda i: (i, 0)
    const = lambda i: (0, 0)
    resident = lambda shape: pl.BlockSpec(shape, const, pipeline_mode=pl.Buffered(1))
    return pl.pallas_call(
        _out_mlp_kernel,
        grid=(n // TM,),
        in_specs=[pl.BlockSpec((TM, D_MODEL), row), pl.BlockSpec((TM, NSA_WIDTH), row),
                  pl.BlockSpec((TM, GDN_WIDTH), row), pl.BlockSpec((1, NSA_WIDTH), const),
                  resident((D_MODEL, D_MODEL)), pl.BlockSpec((1, D_MODEL), const),
                  resident((D_MODEL, D_FF)), resident((D_MODEL, D_FF)), resident((D_FF, D_MODEL)),
                  pl.BlockSpec((1, D_MODEL), const)],
        out_specs=pl.BlockSpec((TM, D_MODEL), row),
        out_shape=jax.ShapeDtypeStruct((n, D_MODEL), F32),
        scratch_shapes=[pltpu.VMEM((TM, D_FF), BF16)],
        compiler_params=pltpu.CompilerParams(dimension_semantics=("arbitrary",),
                                             vmem_limit_bytes=VMEM_LIMIT),
        name="out_mlp",
    )(x2, o_nsa, o_gdn, gn, wo, g2, wg, wu, wd, gf)


def _nsa_head_perm():
    c = jnp.arange(NSA_WIDTH)
    return ((c // LANES) + 4 * ((c % LANES) // HEAD_DIM)) * HEAD_DIM + c % HEAD_DIM


def _layer(x2, pos_row, seq_len, norm1_g, w_in, cmp_pos, cmp_w1, cmp_w2, nsa_norm_g, gdn_conv_w, gdn_a_log,
           gdn_dt_bias, gdn_norm_g, w_out, norm2_g, w_gate, w_up, w_down, out_g):
    w_t, w_r, alog, dtb, alogt, dtbt = _prep_in_proj_weights(w_in, gdn_a_log, gdn_dt_bias)
    qt, cmp_k, ks, kw, vst, vwt, cmp_v, gq, gk, gv, z, small, small_t, gt = _in_proj(
        x2, pos_row, norm1_g[None, :].astype(F32), w_t, w_r, _rope_inv_freq(), gdn_conv_w.astype(F32),
        alog, dtb, alogt, dtbt, seq_len)
    kvc = _compress(cmp_k, cmp_v, *_prep_compress_weights(cmp_pos, cmp_w1, cmp_w2), seq_len)
    ovt, eblk = _nsa_constants(seq_len)
    o_nsa = _nsa(qt, ks, kw, vst, vwt, kvc, small_t, ovt, eblk, seq_len)
    o_gdn = _gdn(gq, gk, gv, z, small, gt, gdn_norm_g[None, :].astype(F32), seq_len)
    perm = _nsa_head_perm()
    wo = jnp.concatenate([w_out[:NSA_WIDTH][perm], w_out[NSA_WIDTH:]], axis=0).astype(BF16)
    return _out_mlp(x2, o_nsa, o_gdn, nsa_norm_g[perm][None, :].astype(F32), wo, norm2_g[None, :].astype(F32),
                    w_gate.astype(BF16), w_up.astype(BF16), w_down.astype(BF16), out_g[None, :].astype(F32))


def kernel(x, positions, norm1_g, w_in, cmp_pos, cmp_w1, cmp_w2, nsa_norm_g, gdn_conv_w, gdn_a_log,
           gdn_dt_bias, gdn_norm_g, w_out, norm2_g, w_gate, w_up, w_down, final_g):
    nb, seq_len, d = x.shape
    depth = w_in.shape[0]
    assert d == D_MODEL and seq_len % TM == 0 and seq_len // SEL_BLOCK == N_SEL and depth == 1
    x2 = x.reshape(nb * seq_len, d)
    pos_row = positions.reshape(1, nb * seq_len)
    out = _layer(x2, pos_row, seq_len, norm1_g[0], w_in[0], cmp_pos[0], cmp_w1[0], cmp_w2[0], nsa_norm_g[0],
                 gdn_conv_w[0], gdn_a_log[0], gdn_dt_bias[0], gdn_norm_g[0], w_out[0], norm2_g[0],
                 w_gate[0], w_up[0], w_down[0], final_g)
    return out.reshape(nb, seq_len, d)
```

```python
import functools
import math

import jax
import jax.numpy as jnp
from jax import lax
from jax.experimental import pallas as pl
from jax.experimental.pallas import tpu as pltpu

F32 = jnp.float32
BF16 = jnp.bfloat16
FP8 = jnp.float8_e4m3fn

LANES = 128
SUBLANES = 8

D_MODEL = 1024
N_HEADS = 8
N_GROUPS = 2
HEAD_DIM = 64
CMP_BLOCK = 32
CMP_STRIDE = 16
CMP_HIDDEN = 128
SEL_BLOCK = 64
SEL_TOP_N = 8
WINDOW = 512
ROPE_THETA = 500000.0
ROPE_DIM = 16
GDN_HEADS = 4
GDN_DIM = 128
GDN_CONV = 4
GDN_CHUNK = 64
NSA_WIDTH = 512
GDN_WIDTH = 512
D_FF = 2816
RMS_EPS = 1e-6
NEG_INF = -1e30
MASK_BIAS = -1e9
FORCE_SCORE = 1e9

OFF_KV = 512
OFF_GATE = OFF_KV + 6 * 128
OFF_GQKV = OFF_GATE + 24
OFF_Z = OFF_GQKV + 3 * GDN_WIDTH
OFF_B = OFF_Z + GDN_WIDTH
OFF_A = OFF_B + GDN_HEADS

TR_Q = 0
TR_K = 512
TR_V = 896
TR_SMALL = 1152
TR_ROWS = 1184
TM_VCMP = 0
TM_GDN = 128
TM_Z = 1664
TM_SMALL = 2176
TM_COLS = 2304
SMALL_B = 24
SMALL_A = 28
N_SMALL = 32

Q_SCALE = HEAD_DIM ** -0.5 * math.log2(math.e)
TM = 512
VMEM_LIMIT = 56 * 1024 * 1024


def _dot(a, b):
    return jnp.dot(a, b, preferred_element_type=F32)


def _dot_nt(a, b):
    return lax.dot_general(a, b, (((1,), (1,)), ((), ())), preferred_element_type=F32)


def _sigmoid(x):
    return 1.0 / (1.0 + jnp.exp(-x))


def _silu(x):
    return x * _sigmoid(x)


def _softplus(x):
    return jnp.maximum(x, 0.0) + jnp.log(1.0 + jnp.exp(-jnp.abs(x)))


def _split3(x):
    h1 = x.astype(BF16)
    r1 = x - h1.astype(F32)
    h2 = r1.astype(BF16)
    h3 = (r1 - h2.astype(F32)).astype(BF16)
    return h1, h2, h3


def _dot3(mat01, x, nt=False):
    f = (lambda a: _dot_nt(a, mat01)) if nt else (lambda a: _dot(mat01, a))
    h1, h2, h3 = _split3(x)
    return f(h1) + f(h2) + f(h3)


def _rope_t(blk, cosv, sinv):
    half = ROPE_DIM // 2
    parts = []
    for h in range(blk.shape[0] // HEAD_DIM):
        b = h * HEAD_DIM
        x0 = blk[b:b + half]
        x1 = blk[b + half:b + ROPE_DIM]
        parts += [x0 * cosv - x1 * sinv, x1 * cosv + x0 * sinv, blk[b + ROPE_DIM:b + HEAD_DIM]]
    return jnp.concatenate(parts, axis=0)


def _in_proj_kernel(tiles_per_seq, x_ref, pos_ref, g1_ref, wt_ref, w_ref, invf_ref, convw_ref,
                    alog_ref, dtb_ref, alogt_ref, dtbt_ref,
                    qt_ref, kc_ref, ks_ref, kw_ref, vst_ref, vwt_ref, cmpv_ref,
                    gq_ref, gk_ref, gv_ref, z_ref, small_ref, smallt_ref, gt_ref,
                    cbuf):
    tm = x_ref.shape[0]
    nlb = tm // LANES
    x = x_ref[...]
    hn = x * lax.rsqrt(jnp.mean(x * x, axis=-1, keepdims=True) + RMS_EPS) * g1_ref[...]
    hb = hn.astype(BF16)

    first = pl.program_id(0) % tiles_per_seq == 0

    @pl.when(first)
    def _():
        cbuf[0:SUBLANES, :] = jnp.zeros((SUBLANES, 3 * GDN_WIDTH), F32)

    proj = {}

    def gdn_piece(c):
        lo = TM_GDN + 2 * c * LANES
        cbuf[SUBLANES:SUBLANES + tm, 2 * c * LANES:2 * (c + 1) * LANES] = _dot(hb, w_ref[:, lo:lo + 2 * LANES])

    def t_piece(name, lo, hi):
        proj[name] = _dot_nt(wt_ref[lo:hi, :], hb)

    def tm_piece(name, lo, hi):
        proj[name] = _dot(hb, w_ref[:, lo:hi])

    later = [lambda: gdn_piece(2), lambda: t_piece("q01", TR_Q, TR_Q + 2 * LANES),
             lambda: gdn_piece(3), lambda: t_piece("q23", TR_Q + 2 * LANES, TR_K),
             lambda: gdn_piece(4), lambda: t_piece("k", TR_K, TR_V),
             lambda: gdn_piece(5), lambda: t_piece("v", TR_V, TR_ROWS),
             lambda: tm_piece("z0", TM_Z, TM_Z + 2 * LANES), lambda: tm_piece("z1", TM_Z + 2 * LANES, TM_SMALL),
             lambda: tm_piece("vcmp", TM_VCMP, TM_VCMP + LANES), lambda: tm_piece("small", TM_SMALL, TM_COLS)]
    gdn_piece(0)
    gdn_piece(1)

    ones_sq = jnp.ones((LANES, LANES), BF16)

    def conv_block(cb):
        cols = slice(cb * LANES, (cb + 1) * LANES)
        zx = cbuf[0:SUBLANES + tm, cols]
        y = zx * convw_ref[GDN_CONV - 1:GDN_CONV, cols]
        for s in range(1, GDN_CONV):
            y = y + pltpu.roll(zx, s, 0) * convw_ref[GDN_CONV - 1 - s:GDN_CONV - s, cols]
        return _silu(y[SUBLANES:])

    def finish_block(cb, y):
        cols = slice((cb % 4) * LANES, (cb % 4 + 1) * LANES)
        if cb < 8:
            y = y * lax.rsqrt(_dot((y * y).astype(BF16), ones_sq) + RMS_EPS)
        if cb < 4:
            gq_ref[:, cols] = (y * (GDN_DIM ** -0.5)).astype(BF16)
        elif cb < 8:
            gk_ref[:, cols] = y.astype(BF16)
        else:
            gv_ref[:, cols] = y.astype(BF16)

    pending = None
    for cb in range(12):
        later[cb]()
        y = conv_block(cb)
        if pending is not None:
            finish_block(*pending)
        pending = (cb, y)
    finish_block(*pending)
    cbuf[0:SUBLANES, :] = cbuf[tm:tm + SUBLANES, :]

    yt = jnp.concatenate([proj["q01"], proj["q23"], proj["k"], proj["v"]], axis=0)
    ang = invf_ref[...] * pos_ref[...].astype(F32)
    cosv = jnp.cos(ang)
    sinv = jnp.sin(ang)
    for m in range(4):
        blk = _rope_t(yt[TR_Q + m * LANES:TR_Q + (m + 1) * LANES], cosv, sinv)
        qt_ref[m * LANES:(m + 1) * LANES, :] = (blk * Q_SCALE).astype(BF16)
    for j, ref in enumerate((kc_ref, ks_ref, kw_ref)):
        blk = _rope_t(yt[TR_K + j * LANES:TR_K + (j + 1) * LANES], cosv, sinv)
        for c in range(nlb):
            ref[c * LANES:(c + 1) * LANES, :] = blk[:, c * LANES:(c + 1) * LANES].T.astype(ref.dtype)
    for j, ref in enumerate((vst_ref, vwt_ref)):
        blk = yt[TR_V + j * LANES:TR_V + (j + 1) * LANES].astype(BF16)
        for c in range(nlb):
            ref[c] = blk[:, c * LANES:(c + 1) * LANES]
    st = yt[TR_SMALL:TR_SMALL + N_SMALL]
    srow = lax.broadcasted_iota(jnp.int32, (N_SMALL, 1), 0)
    gdec_t = -jnp.exp(alogt_ref[...]) * _softplus(st + dtbt_ref[...])
    small_t = jnp.where(srow < SMALL_A, _sigmoid(st), gdec_t)
    smallt_ref[...] = small_t
    for c in range(nlb):
        gt_ref[c] = small_t[SMALL_B:N_SMALL, c * LANES:(c + 1) * LANES]

    cmpv_ref[...] = proj["vcmp"]
    z_ref[:, 0:2 * LANES] = proj["z0"].astype(BF16)
    z_ref[:, 2 * LANES:GDN_WIDTH] = proj["z1"].astype(BF16)

    sm = proj["small"]
    lane1 = lax.broadcasted_iota(jnp.int32, (1, LANES), 1)
    gdec = -jnp.exp(alog_ref[...]) * _softplus(sm + dtb_ref[...])
    small_ref[...] = jnp.where(lane1 < SMALL_A, _sigmoid(sm), gdec)


def _in_proj(x2, pos_row, g1, w_t, w_r, invf, convw, alog, dtb, alogt, dtbt, seq_len):
    n = x2.shape[0]
    row = lambda i: (i, 0)
    colb = lambda i: (0, i)
    lead = lambda i: (i, 0, 0)
    const = lambda i: (0, 0)
    nlb = TM // LANES
    out_shapes = (
        jax.ShapeDtypeStruct((NSA_WIDTH, n), BF16),
        jax.ShapeDtypeStruct((n, LANES), F32),
        jax.ShapeDtypeStruct((n, LANES), BF16),
        jax.ShapeDtypeStruct((n, LANES), BF16),
        jax.ShapeDtypeStruct((n // LANES, LANES, LANES), BF16),
        jax.ShapeDtypeStruct((n // LANES, LANES, LANES), BF16),
        jax.ShapeDtypeStruct((n, LANES), F32),
        jax.ShapeDtypeStruct((n, GDN_WIDTH), BF16),
        jax.ShapeDtypeStruct((n, GDN_WIDTH), BF16),
        jax.ShapeDtypeStruct((n, GDN_WIDTH), BF16),
        jax.ShapeDtypeStruct((n, GDN_WIDTH), BF16),
        jax.ShapeDtypeStruct((n, LANES), F32),
        jax.ShapeDtypeStruct((N_SMALL, n), F32),
        jax.ShapeDtypeStruct((n // LANES, SUBLANES, LANES), F32),
    )
    out_specs = (
        pl.BlockSpec((NSA_WIDTH, TM), colb),
        pl.BlockSpec((TM, LANES), row), pl.BlockSpec((TM, LANES), row), pl.BlockSpec((TM, LANES), row),
        pl.BlockSpec((nlb, LANES, LANES), lead), pl.BlockSpec((nlb, LANES, LANES), lead),
        pl.BlockSpec((TM, LANES), row),
        pl.BlockSpec((TM, GDN_WIDTH), row), pl.BlockSpec((TM, GDN_WIDTH), row),
        pl.BlockSpec((TM, GDN_WIDTH), row), pl.BlockSpec((TM, GDN_WIDTH), row),
        pl.BlockSpec((TM, LANES), row),
        pl.BlockSpec((N_SMALL, TM), colb),
        pl.BlockSpec((nlb, SUBLANES, LANES), lead),
    )
    in_specs = [
        pl.BlockSpec((TM, D_MODEL), row), pl.BlockSpec((1, TM), colb), pl.BlockSpec((1, D_MODEL), const),
        pl.BlockSpec((TR_ROWS, D_MODEL), const), pl.BlockSpec((D_MODEL, TM_COLS), const),
        pl.BlockSpec((ROPE_DIM // 2, 1), const), pl.BlockSpec((GDN_CONV, 3 * GDN_WIDTH), const),
        pl.BlockSpec((1, LANES), const), pl.BlockSpec((1, LANES), const),
        pl.BlockSpec((N_SMALL, 1), const), pl.BlockSpec((N_SMALL, 1), const),
    ]
    return pl.pallas_call(
        functools.partial(_in_proj_kernel, seq_len // TM),
        grid=(n // TM,), in_specs=in_specs, out_specs=out_specs, out_shape=out_shapes,
        scratch_shapes=[pltpu.VMEM((TM + 2 * SUBLANES, 3 * GDN_WIDTH), F32)],
        compiler_params=pltpu.CompilerParams(dimension_semantics=("arbitrary",),
                                             vmem_limit_bytes=VMEM_LIMIT),
        name="in_proj",
    )(x2, pos_row, g1, w_t, w_r, invf, convw, alog, dtb, alogt, dtbt)


def _prep_in_proj_weights(w_in, gdn_a_log, gdn_dt_bias):
    q = w_in[:, :NSA_WIDTH].reshape(D_MODEL, 2, 4, HEAD_DIM)
    q = jnp.transpose(q, (0, 2, 1, 3)).reshape(D_MODEL, NSA_WIDTH)
    kv = w_in[:, OFF_KV:OFF_GATE].reshape(D_MODEL, 6, LANES)
    gate = w_in[:, OFF_GATE:OFF_GQKV].reshape(D_MODEL, 2, 4, 3)
    gate = jnp.transpose(gate, (0, 3, 2, 1)).reshape(D_MODEL, 24)
    small = jnp.concatenate([gate, w_in[:, OFF_B:OFF_A], w_in[:, OFF_A:OFF_A + GDN_HEADS]], axis=1)
    w_t = jnp.concatenate([q, kv[:, 0], kv[:, 2], kv[:, 4], kv[:, 3], kv[:, 5], small], axis=1).T.astype(BF16)
    w_r = jnp.concatenate([kv[:, 1], w_in[:, OFF_GQKV:OFF_Z], w_in[:, OFF_Z:OFF_B], small,
                           jnp.zeros((D_MODEL, LANES - N_SMALL), w_in.dtype)], axis=1).astype(BF16)
    alog = jnp.zeros((LANES,), F32).at[SMALL_A:SMALL_A + GDN_HEADS].set(gdn_a_log.astype(F32))
    dtb = jnp.zeros((LANES,), F32).at[SMALL_A:SMALL_A + GDN_HEADS].set(gdn_dt_bias.astype(F32))
    return w_t, w_r, alog[None, :], dtb[None, :], alog[:N_SMALL, None], dtb[:N_SMALL, None]


def _rope_inv_freq():
    half = ROPE_DIM // 2
    return jnp.power(ROPE_THETA, -jnp.arange(half, dtype=F32) * (2.0 / ROPE_DIM))[:, None]


def _compress_kernel(xk_ref, xv_ref, pos_ref, w1_ref, w2_ref, out_ref):
    nblk = xk_ref.shape[0] // CMP_STRIDE
    acc_lo = jnp.zeros((nblk, 4 * CMP_HIDDEN), F32)
    acc_hi = jnp.zeros((nblk, 4 * CMP_HIDDEN), F32)
    for j in range(CMP_STRIDE):
        xj = jnp.concatenate([xk_ref[pl.ds(j, nblk, stride=CMP_STRIDE), :],
                              xv_ref[pl.ds(j, nblk, stride=CMP_STRIDE), :]], axis=1)
        acc_lo = acc_lo + _dot((xj + pos_ref[j:j + 1, :]).astype(BF16), w1_ref[j])
        acc_hi = acc_hi + _dot((xj + pos_ref[CMP_STRIDE + j:CMP_STRIDE + j + 1, :]).astype(BF16),
                               w1_ref[CMP_STRIDE + j])
    pre = acc_lo + pltpu.roll(acc_hi, nblk - 1, 0)
    kvc = _dot(_silu(pre).astype(BF16), w2_ref[...])
    out_ref[0:nblk, :] = kvc[:, 0:LANES]
    out_ref[nblk:2 * nblk, :] = kvc[:, LANES:2 * LANES].T


def _compress(cmp_k, cmp_v, pos_rows, w1_bd, w2_bd, seq_len):
    n = cmp_k.shape[0]
    nb = n // seq_len
    nblk = seq_len // CMP_STRIDE
    assert nblk == LANES
    return pl.pallas_call(
        _compress_kernel,
        grid=(nb,),
        in_specs=[pl.BlockSpec((seq_len, LANES), lambda b: (b, 0)),
                  pl.BlockSpec((seq_len, LANES), lambda b: (b, 0)),
                  pl.BlockSpec((CMP_BLOCK, 256), lambda b: (0, 0)),
                  pl.BlockSpec((CMP_BLOCK, 256, 4 * CMP_HIDDEN), lambda b: (0, 0, 0)),
                  pl.BlockSpec((4 * CMP_HIDDEN, 256), lambda b: (0, 0))],
        out_specs=pl.BlockSpec((2 * nblk, LANES), lambda b: (b, 0)),
        out_shape=jax.ShapeDtypeStruct((nb * 2 * nblk, LANES), F32),
        compiler_params=pltpu.CompilerParams(dimension_semantics=("arbitrary",),
                                             vmem_limit_bytes=VMEM_LIMIT),
        name="nsa_compress",
    )(cmp_k, cmp_v, pos_rows, w1_bd, w2_bd)


def _prep_compress_weights(cmp_pos, cmp_w1, cmp_w2):
    slot_src = jnp.array([0, 0, 1, 1])
    eye = jnp.eye(4, dtype=BF16)
    w1 = cmp_w1.reshape(2, CMP_BLOCK, HEAD_DIM, CMP_HIDDEN).astype(BF16)[slot_src]
    w1 = jnp.transpose(w1, (1, 0, 2, 3)).reshape(CMP_BLOCK, 4 * HEAD_DIM, CMP_HIDDEN)
    diag = jnp.repeat(jnp.repeat(eye, HEAD_DIM, axis=0), CMP_HIDDEN, axis=1)
    w1_bd = jnp.tile(w1, (1, 1, 4)) * diag[None]
    w2_bd = cmp_w2.astype(BF16)[slot_src][:, :, None, :] * eye[:, None, :, None]
    w2_bd = w2_bd.reshape(4 * CMP_HIDDEN, 256)
    pos_rows = jnp.concatenate([cmp_pos[0], cmp_pos[0], cmp_pos[1], cmp_pos[1]], axis=-1).astype(F32)
    return pos_rows, w1_bd, w2_bd


TQ = 256
KC = 256
N_SEL = 32
ROWS = N_HEADS * TQ
WCHUNKS = WINDOW // KC + 1
ONES_ROWS = 16


def _nsa_kernel(qt_ref, ks_ref, kw_ref, vst_ref, vwt_ref, kvc_ref, gt_ref, ovt_ref, eblk_ref, o_ref,
                qaug, sbuf, mxbuf, ms_sc, mw_sc, accs_sc, accw_sc, out_sc):
    assert WCHUNKS == 3
    tile = pl.program_id(1)
    t0 = tile * TQ
    tcol = t0 + lax.broadcasted_iota(jnp.int32, (1, TQ), 1)
    krow = lax.broadcasted_iota(jnp.int32, (KC, 1), 0)
    ones = jnp.ones((ONES_ROWS, KC), BF16)
    groups = [slice(r * TQ, (r + 1) * TQ) for r in range(N_HEADS)]
    blocks_per_chunk = KC // LANES

    zhalf = jnp.zeros((HEAD_DIM, TQ), BF16)
    for m in range(4):
        blk = qt_ref[m * LANES:(m + 1) * LANES, :]
        qaug[0:LANES, groups[2 * m]] = jnp.concatenate([blk[0:HEAD_DIM], zhalf], axis=0)
        qaug[0:LANES, groups[2 * m + 1]] = jnp.concatenate([zhalf, blk[HEAD_DIM:LANES]], axis=0)

    def v_chunk(vt_ref, k0):
        b0 = k0 // LANES
        vt = jnp.concatenate([vt_ref[b0 + j] for j in range(blocks_per_chunk)], axis=1)
        return [jnp.concatenate([vt[g * HEAD_DIM:(g + 1) * HEAD_DIM], ones], axis=0).astype(FP8)
                for g in range(N_GROUPS)]

    buf_a, buf_b = sbuf.at[0], sbuf.at[1]
    mx_a, mx_b = mxbuf.at[0], mxbuf.at[1]

    def produce(buf, mx, kmat, bias):
        qrows = kmat.shape[1]
        for cols in groups:
            s = _dot(kmat, qaug[0:qrows, cols])
            if bias is not None:
                s = s + bias
            buf[:, cols] = s
            mx[:, cols] = jnp.max(s, axis=0, keepdims=True)

    def consume(buf, mx, vt, m_ref, acc_ref):
        for r, cols in enumerate(groups):
            m_old = m_ref[:, cols]
            m_new = jnp.maximum(m_old, mx[:, cols])
            p = jnp.exp2((buf[:, cols] - m_new).astype(BF16)).astype(FP8)
            m_ref[:, cols] = m_new
            acc_ref[:, cols] = acc_ref[:, cols] * jnp.exp2(m_old - m_new) + _dot(vt[r % N_GROUPS], p)

    def produce_consume(pbuf, pmx, kmat, bias, cbuf, cmx, vt, m_ref, acc_ref):
        qrows = kmat.shape[1]
        for r, cols in enumerate(groups):
            s = _dot(kmat, qaug[0:qrows, cols])
            if bias is not None:
                s = s + bias
            pbuf[:, cols] = s
            pmx[:, cols] = jnp.max(s, axis=0, keepdims=True)
            m_old = m_ref[:, cols]
            m_new = jnp.maximum(m_old, cmx[:, cols])
            p = jnp.exp2((cbuf[:, cols] - m_new).astype(BF16)).astype(FP8)
            m_ref[:, cols] = m_new
            acc_ref[:, cols] = acc_ref[:, cols] * jnp.exp2(m_old - m_new) + _dot(vt[r % N_GROUPS], p)

    mw_sc[...] = jnp.full(mw_sc.shape, NEG_INF, F32)
    accw_sc[...] = jnp.zeros(accw_sc.shape, F32)

    def window_chunk(j):
        start = t0 - WINDOW + j * KC
        k0 = pl.multiple_of(jnp.maximum(start, 0), KC)
        kpos = start + krow
        diff = tcol - kpos
        bias = jnp.where((kpos >= 0) & (diff >= 0) & (diff < WINDOW), 0.0, NEG_INF)
        return k0, kw_ref[pl.ds(k0, KC), :], bias

    w_k0 = [None] * WCHUNKS
    w_k0[2], kwin, wbias = window_chunk(2)
    produce(buf_a, mx_a, kwin, wbias)
    w_k0[1], kwin, wbias = window_chunk(1)
    produce_consume(buf_b, mx_b, kwin, wbias, buf_a, mx_a, v_chunk(vwt_ref, w_k0[2]), mw_sc, accw_sc)

    nblk = kvc_ref.shape[0] // 2
    kc = kvc_ref[0:nblk, :].astype(BF16)
    vct = kvc_ref[nblk:2 * nblk, :].astype(BF16)
    nrow = lax.broadcasted_iota(jnp.int32, (nblk, 1), 0)
    vbias = jnp.where(nrow * CMP_STRIDE + (CMP_BLOCK - 1) <= tcol, 0.0, NEG_INF)
    has_any = tcol >= CMP_BLOCK - 1
    s_c = [_dot(kc, qaug[0:LANES, cols]) + vbias for cols in groups]
    e_c = [jnp.exp2(y - jnp.max(y, axis=0, keepdims=True)) for y in s_c]
    p_c = [jnp.where(has_any, y * (1.0 / jnp.sum(y, axis=0, keepdims=True)), 0.0) for y in e_c]
    gates = gt_ref[...]
    for r, cols in enumerate(groups):
        g = r % N_GROUPS
        out_sc[:, cols] = gates[r:r + 1] * _dot(vct[g * HEAD_DIM:(g + 1) * HEAD_DIM], p_c[r].astype(BF16))

    w_k0[0], kwin, wbias = window_chunk(0)
    produce_consume(buf_a, mx_a, kwin, wbias, buf_b, mx_b, v_chunk(vwt_ref, w_k0[1]), mw_sc, accw_sc)

    jrow = lax.broadcasted_iota(jnp.int32, (N_SEL, 1), 0)
    cur = lax.shift_right_logical(tcol, 6)
    forced = (jrow == 0) | (jrow == cur) | (jrow == cur - 1)
    causal = jrow <= cur
    for g in range(N_GROUPS):
        psum = (p_c[g] + p_c[2 + g]) + (p_c[4 + g] + p_c[6 + g])
        imp_t = _dot3(ovt_ref[...], psum)
        score = jnp.where(forced, FORCE_SCORE, jnp.where(causal, imp_t[0:N_SEL, :], NEG_INF))
        cnt = jnp.zeros((N_SEL, TQ), jnp.int32)
        for jp in range(N_SEL):
            rowv = score[jp:jp + 1, :]
            beats = (rowv > score) | ((rowv == score) & (jrow > jp))
            cnt = cnt + jnp.where(beats, 1, 0)
        sel = (cnt < SEL_TOP_N) & causal
        bias = jnp.concatenate([jnp.where(sel, 0.0, MASK_BIAS),
                                jnp.zeros((LANES - N_SEL, TQ), F32)], axis=0).astype(BF16)
        for m in range(4):
            qaug[LANES:2 * LANES, groups[2 * m + g]] = bias

    ms_sc[...] = jnp.full(ms_sc.shape, NEG_INF, F32)
    accs_sc[...] = jnp.zeros(accs_sc.shape, F32)

    def sel_keys(c):
        k0 = pl.multiple_of(c * KC, KC)
        return jnp.concatenate([ks_ref[pl.ds(k0, KC), :], eblk_ref[pl.ds(k0, KC), :]], axis=1)

    def sel_consume(buf, mx, c):
        consume(buf, mx, v_chunk(vst_ref, c * KC), ms_sc, accs_sc)

    produce_consume(buf_b, mx_b, sel_keys(tile), jnp.where(t0 + krow <= tcol, 0.0, NEG_INF),
                    buf_a, mx_a, v_chunk(vwt_ref, w_k0[0]), mw_sc, accw_sc)
    n_full = tile
    n_pairs = n_full // 2
    odd = n_full % 2 == 1

    def sel_produce_consume(pbuf, pmx, c_new, cbuf, cmx, c_old):
        produce_consume(pbuf, pmx, sel_keys(c_new), None, cbuf, cmx, v_chunk(vst_ref, c_old * KC), ms_sc, accs_sc)

    def body(j, carry):
        sel_produce_consume(buf_a, mx_a, 2 * j, buf_b, mx_b, jnp.where(j == 0, tile, 2 * j - 1))
        sel_produce_consume(buf_b, mx_b, 2 * j + 1, buf_a, mx_a, 2 * j)
        return carry

    lax.fori_loop(0, n_pairs, body, 0)
    last_b = jnp.where(n_pairs == 0, tile, 2 * n_pairs - 1)

    @pl.when(odd)
    def _():
        sel_produce_consume(buf_a, mx_a, 2 * n_pairs, buf_b, mx_b, last_b)
        sel_consume(buf_a, mx_a, 2 * n_pairs)

    @pl.when(jnp.logical_not(odd))
    def _():
        sel_consume(buf_b, mx_b, last_b)

    for m in range(4):
        halves = []
        for g in range(N_GROUPS):
            r = 2 * m + g
            acc_s = accs_sc[:, groups[r]]
            acc_w = accw_sc[:, groups[r]]
            halves.append(out_sc[:, groups[r]]
                          + (gates[8 + r:9 + r] * (1.0 / acc_s[HEAD_DIM:HEAD_DIM + 1])) * acc_s[0:HEAD_DIM]
                          + (gates[16 + r:17 + r] * (1.0 / acc_w[HEAD_DIM:HEAD_DIM + 1])) * acc_w[0:HEAD_DIM])
        o_ref[:, m * LANES:(m + 1) * LANES] = jnp.concatenate(halves, axis=0).T.astype(BF16)


def _nsa(qt, ks, kw, vst, vwt, kvc, small_t, ovt, eblk, seq_len):
    n = ks.shape[0]
    nb = n // seq_len
    nq = seq_len // TQ
    nkb = seq_len // LANES
    seq = lambda b, i: (b, 0)
    return pl.pallas_call(
        _nsa_kernel,
        grid=(nb, nq),
        in_specs=[pl.BlockSpec((NSA_WIDTH, TQ), lambda b, i: (0, b * nq + i)),
                  pl.BlockSpec((seq_len, LANES), seq), pl.BlockSpec((seq_len, LANES), seq),
                  pl.BlockSpec((nkb, LANES, LANES), lambda b, i: (b, 0, 0)),
                  pl.BlockSpec((nkb, LANES, LANES), lambda b, i: (b, 0, 0)),
                  pl.BlockSpec((2 * LANES, LANES), seq),
                  pl.BlockSpec((N_SMALL, TQ), lambda b, i: (0, b * nq + i)),
                  pl.BlockSpec((LANES, LANES), lambda b, i: (0, 0)),
                  pl.BlockSpec((seq_len, LANES), lambda b, i: (0, 0))],
        out_specs=pl.BlockSpec((TQ, NSA_WIDTH), lambda b, i: (b * nq + i, 0)),
        out_shape=jax.ShapeDtypeStruct((n, NSA_WIDTH), BF16),
        scratch_shapes=[pltpu.VMEM((2 * LANES, ROWS), BF16),
                        pltpu.VMEM((2, KC, ROWS), F32),
                        pltpu.VMEM((2, 1, ROWS), F32),
                        pltpu.VMEM((1, ROWS), F32), pltpu.VMEM((1, ROWS), F32),
                        pltpu.VMEM((HEAD_DIM + ONES_ROWS, ROWS), F32),
                        pltpu.VMEM((HEAD_DIM + ONES_ROWS, ROWS), F32),
                        pltpu.VMEM((HEAD_DIM, ROWS), F32)],
        compiler_params=pltpu.CompilerParams(dimension_semantics=("arbitrary", "arbitrary"),
                                             vmem_limit_bytes=VMEM_LIMIT),
        name="nsa_attention",
    )(qt, ks, kw, vst, vwt, kvc, small_t, ovt, eblk)


def _nsa_constants(seq_len):
    n_cmp = (seq_len - CMP_BLOCK) // CMP_STRIDE + 1
    s = jnp.arange(LANES)[:, None]
    nn = jnp.arange(LANES)[None, :]
    cs = nn * CMP_STRIDE
    ss = s * SEL_BLOCK
    ovt = (cs < ss + SEL_BLOCK) & (cs + CMP_BLOCK > ss) & (s < seq_len // SEL_BLOCK) & (nn < n_cmp)
    k = jnp.arange(seq_len)[:, None]
    eblk = (k // SEL_BLOCK) == jnp.arange(LANES)[None, :]
    return ovt.astype(BF16), eblk.astype(BF16)


PAIR = 2 * GDN_CHUNK
N_DOUBLINGS = 5
GT_G = 4


def _gdn_kernel(q_ref, k_ref, v_ref, z_ref, small_ref, gt_ref, ng_ref, o_ref, s_sc):
    nseq = q_ref.shape[0]
    units = [(s, h) for s in range(nseq) for h in range(GDN_HEADS)]
    us = range(len(units))
    cols = [slice(h * GDN_DIM, (h + 1) * GDN_DIM) for _, h in units]
    ri = lax.broadcasted_iota(jnp.int32, (PAIR, PAIR), 0)
    ci = lax.broadcasted_iota(jnp.int32, (PAIR, PAIR), 1)
    same = lax.shift_right_logical(ri, 6) == lax.shift_right_logical(ci, 6)
    incl = same & (ri >= ci)
    strict = same & (ri > ci)
    first_cols = ci < GDN_CHUNK
    first_row = ci[0:1, :] < GDN_CHUNK
    ltri = jnp.where(incl, 1.0, 0.0).astype(BF16)
    tot = [jnp.where(first_cols, 1.0, 0.0).astype(BF16),
           jnp.where(first_cols, 0.0, 1.0).astype(BF16)]

    @pl.when(pl.program_id(1) == 0)
    def _():
        s_sc[...] = jnp.zeros(s_sc.shape, F32)

    sm_s = [small_ref[s] for s in range(nseq)]
    cs_s = [_dot3(ltri, y) for y in sm_s]
    gt_s = [gt_ref[s, 0] for s in range(nseq)]
    csr_s = [_dot3(ltri, y, nt=True) for y in gt_s]
    glast_s = [[_dot3(tot[c], y, nt=True) for c in range(2)] for y in gt_s]
    seq = [s for s, _ in units]
    q = [q_ref[seq[u], :, cols[u]] for u in us]
    k = [k_ref[seq[u], :, cols[u]] for u in us]
    kf = [y.astype(F32) for y in k]
    beta = [sm_s[s][:, SMALL_B + h:SMALL_B + h + 1] for s, h in units]
    gcc = [cs_s[s][:, SMALL_A + h:SMALL_A + h + 1] for s, h in units]
    gcr = [csr_s[s][GT_G + h:GT_G + h + 1, :] for s, h in units]
    glast = [[glast_s[s][c][GT_G + h:GT_G + h + 1, :] for c in range(2)] for s, h in units]
    decay = [jnp.exp(jnp.where(incl, gcc[u] - gcr[u], NEG_INF)) for u in us]
    egc = [jnp.exp(g) for g in gcc]
    kb = [kf[u] * beta[u] for u in us]
    kk = [_dot_nt(kb[u].astype(BF16), k[u]) for u in us]
    p = [jnp.where(strict, -(kk[u] * decay[u]), 0.0) for u in us]
    x = [jnp.concatenate([v_ref[seq[u], :, cols[u]].astype(F32) * beta[u], kb[u] * egc[u]], axis=1) for u in us]
    for i in range(N_DOUBLINGS + 1):
        pb = [y.astype(BF16) for y in p]
        x = [x[u] + _dot(pb[u], x[u].astype(BF16)) for u in us]
        if i < N_DOUBLINGS:
            p = [_dot(y, y) for y in pb]
    uw = [y.astype(BF16) for y in x]
    qk = [(_dot_nt(q[u], k[u]) * decay[u]).astype(BF16) for u in us]
    ke_t = [kf[u].T * jnp.exp(jnp.where(first_row, glast[u][0], glast[u][1]) - gcr[u]) for u in us]
    kw = [[_dot(jnp.where(first_cols, y, 0.0).astype(BF16), uw[u]) for u, y in enumerate(ke_t)],
          [_dot(jnp.where(first_cols, 0.0, y).astype(BF16), uw[u]) for u, y in enumerate(ke_t)]]
    qw = [_dot(qk[u], uw[u]) for u in us]
    qm = [(q[u].astype(F32) * egc[u] - qw[u][:, GDN_DIM:2 * GDN_DIM]).astype(BF16) for u in us]
    s = [s_sc[u] for u in us]
    os = []
    for c in range(2):
        half = slice(c * GDN_CHUNK, (c + 1) * GDN_CHUNK)
        sb = [y.astype(BF16) for y in s]
        os.append([_dot(qm[u][half], sb[u]) for u in us])
        ks = [_dot(kw[c][u][:, GDN_DIM:2 * GDN_DIM].astype(BF16), sb[u]) for u in us]
        s = [s[u] * jnp.exp(glast[u][c]) - ks[u] + kw[c][u][:, 0:GDN_DIM] for u in us]
    for u in us:
        s_sc[u] = s[u]
        o = jnp.concatenate([os[0][u], os[1][u]], axis=0) + qw[u][:, 0:GDN_DIM]
        o = o * lax.rsqrt(jnp.mean(o * o, axis=-1, keepdims=True) + RMS_EPS) * ng_ref[...]
        o_ref[seq[u], :, cols[u]] = (o * _silu(z_ref[seq[u], :, cols[u]].astype(F32))).astype(BF16)


GDN_SEQS = 4


def _gdn(gq, gk, gv, z, small, gt, norm_g, seq_len):
    n = gq.shape[0]
    nb = n // seq_len
    n_pairs = seq_len // PAIR
    nseq = max(d for d in range(1, GDN_SEQS + 1) if nb % d == 0)
    by_seq = lambda a: a.reshape(nb, seq_len, a.shape[-1])
    blk = lambda w: pl.BlockSpec((nseq, PAIR, w), lambda b, i: (b, i, 0))
    out = pl.pallas_call(
        _gdn_kernel,
        grid=(nb // nseq, n_pairs),
        in_specs=[blk(GDN_WIDTH), blk(GDN_WIDTH), blk(GDN_WIDTH), blk(GDN_WIDTH), blk(LANES),
                  pl.BlockSpec((nseq, 1, SUBLANES, PAIR), lambda b, i: (b, i, 0, 0)),
                  pl.BlockSpec((1, GDN_DIM), lambda b, i: (0, 0))],
        out_specs=blk(GDN_WIDTH),
        out_shape=jax.ShapeDtypeStruct((nb, seq_len, GDN_WIDTH), BF16),
        scratch_shapes=[pltpu.VMEM((nseq * GDN_HEADS, GDN_DIM, GDN_DIM), F32)],
        compiler_params=pltpu.CompilerParams(dimension_semantics=("arbitrary", "arbitrary"),
                                             vmem_limit_bytes=VMEM_LIMIT),
        name="gated_delta_rule",
    )(by_seq(gq), by_seq(gk), by_seq(gv), by_seq(z), by_seq(small),
      gt.reshape(nb, n_pairs, SUBLANES, PAIR), norm_g)
    return out.reshape(n, GDN_WIDTH)


FF_CHUNK = 256


def _rms(x, g):
    return x * lax.rsqrt(jnp.mean(x * x, axis=-1, keepdims=True) + RMS_EPS) * g


def _out_mlp_kernel(x_ref, on_ref, og_ref, gn_ref, wo_ref, g2_ref, wg_ref, wu_ref, wd_ref, gf_ref, out_ref,
                    act_sc):
    o_nsa = _rms(on_ref[...].astype(F32), gn_ref[...]).astype(BF16)
    mix = jnp.concatenate([o_nsa, og_ref[...]], axis=1)
    h = x_ref[...] + _dot(mix, wo_ref[...])
    hn = _rms(h, g2_ref[...]).astype(BF16)
    for c in range(D_FF // FF_CHUNK):
        cols = slice(c * FF_CHUNK, (c + 1) * FF_CHUNK)
        act_sc[:, cols] = (_silu(_dot(hn, wg_ref[:, cols])) * _dot(hn, wu_ref[:, cols])).astype(BF16)
    out_ref[...] = _rms(h + _dot(act_sc[...], wd_ref[...]), gf_ref[...])


def _out_mlp(x2, o_nsa, o_gdn, gn, wo, g2, wg, wu, wd, gf):
    n = x2.shape[0]
    row = lambda i: (i, 0)
    const = lambda i: (0, 0)
    resident = lambda shape: pl.BlockSpec(shape, const, pipeline_mode=pl.Buffered(1))
    return pl.pallas_call(
        _out_mlp_kernel,
        grid=(n // TM,),
        in_specs=[pl.BlockSpec((TM, D_MODEL), row), pl.BlockSpec((TM, NSA_WIDTH), row),
                  pl.BlockSpec((TM, GDN_WIDTH), row), pl.BlockSpec((1, NSA_WIDTH), const),
                  resident((D_MODEL, D_MODEL)), pl.BlockSpec((1, D_MODEL), const),
                  resident((D_MODEL, D_FF)), resident((D_MODEL, D_FF)), resident((D_FF, D_MODEL)),
                  pl.BlockSpec((1, D_MODEL), const)],
        out_specs=pl.BlockSpec((TM, D_MODEL), row),
        out_shape=jax.ShapeDtypeStruct((n, D_MODEL), F32),
        scratch_shapes=[pltpu.VMEM((TM, D_FF), BF16)],
        compiler_params=pltpu.CompilerParams(dimension_semantics=("arbitrary",),
                                             vmem_limit_bytes=VMEM_LIMIT),
        name="out_mlp",
    )(x2, o_nsa, o_gdn, gn, wo, g2, wg, wu, wd, gf)


def _nsa_head_perm():
    c = jnp.arange(NSA_WIDTH)
    return ((c // LANES) + 4 * ((c % LANES) // HEAD_DIM)) * HEAD_DIM + c % HEAD_DIM


def _layer(x2, pos_row, seq_len, norm1_g, w_in, cmp_pos, cmp_w1, cmp_w2, nsa_norm_g, gdn_conv_w, gdn_a_log,
           gdn_dt_bias, gdn_norm_g, w_out, norm2_g, w_gate, w_up, w_down, out_g):
    w_t, w_r, alog, dtb, alogt, dtbt = _prep_in_proj_weights(w_in, gdn_a_log, gdn_dt_bias)
    qt, cmp_k, ks, kw, vst, vwt, cmp_v, gq, gk, gv, z, small, small_t, gt = _in_proj(
        x2, pos_row, norm1_g[None, :].astype(F32), w_t, w_r, _rope_inv_freq(), gdn_conv_w.astype(F32),
        alog, dtb, alogt, dtbt, seq_len)
    kvc = _compress(cmp_k, cmp_v, *_prep_compress_weights(cmp_pos, cmp_w1, cmp_w2), seq_len)
    ovt, eblk = _nsa_constants(seq_len)
    o_nsa = _nsa(qt, ks, kw, vst, vwt, kvc, small_t, ovt, eblk, seq_len)
    o_gdn = _gdn(gq, gk, gv, z, small, gt, gdn_norm_g[None, :].astype(F32), seq_len)
    perm = _nsa_head_perm()
    wo = jnp.concatenate([w_out[:NSA_WIDTH][perm], w_out[NSA_WIDTH:]], axis=0).astype(BF16)
    return _out_mlp(x2, o_nsa, o_gdn, nsa_norm_g[perm][None, :].astype(F32), wo, norm2_g[None, :].astype(F32),
                    w_gate.astype(BF16), w_up.astype(BF16), w_down.astype(BF16), out_g[None, :].astype(F32))


def kernel(x, positions, norm1_g, w_in, cmp_pos, cmp_w1, cmp_w2, nsa_norm_g, gdn_conv_w, gdn_a_log,
           gdn_dt_bias, gdn_norm_g, w_out, norm2_g, w_gate, w_up, w_down, final_g):
    nb, seq_len, d = x.shape
    depth = w_in.shape[0]
    assert d == D_MODEL and seq_len % TM == 0 and seq_len // SEL_BLOCK == N_SEL and depth == 1
    x2 = x.reshape(nb * seq_len, d)
    pos_row = positions.reshape(1, nb * seq_len)
    out = _layer(x2, pos_row, seq_len, norm1_g[0], w_in[0], cmp_pos[0], cmp_w1[0], cmp_w2[0], nsa_norm_g[0],
                 gdn_conv_w[0], gdn_a_log[0], gdn_dt_bias[0], gdn_norm_g[0], w_out[0], norm2_g[0],
                 w_gate[0], w_up[0], w_down[0], final_g)
    return out.reshape(nb, seq_len, d)
```

```python
import functools
import math

import jax
import jax.numpy as jnp
from jax import lax
from jax.experimental import pallas as pl
from jax.experimental.pallas import tpu as pltpu

F32 = jnp.float32
BF16 = jnp.bfloat16
FP8 = jnp.float8_e4m3fn

LANES = 128
SUBLANES = 8

D_MODEL = 1024
N_HEADS = 8
N_GROUPS = 2
HEAD_DIM = 64
CMP_BLOCK = 32
CMP_STRIDE = 16
CMP_HIDDEN = 128
SEL_BLOCK = 64
SEL_TOP_N = 8
WINDOW = 512
ROPE_THETA = 500000.0
ROPE_DIM = 16
GDN_HEADS = 4
GDN_DIM = 128
GDN_CONV = 4
GDN_CHUNK = 64
NSA_WIDTH = 512
GDN_WIDTH = 512
D_FF = 2816
RMS_EPS = 1e-6
NEG_INF = -1e30
MASK_BIAS = -1e9
FORCE_SCORE = 1e9

OFF_KV = 512
OFF_GATE = OFF_KV + 6 * 128
OFF_GQKV = OFF_GATE + 24
OFF_Z = OFF_GQKV + 3 * GDN_WIDTH
OFF_B = OFF_Z + GDN_WIDTH
OFF_A = OFF_B + GDN_HEADS

TR_Q = 0
TR_K = 512
TR_V = 896
TR_SMALL = 1152
TR_ROWS = 1184
TM_VCMP = 0
TM_GDN = 128
TM_Z = 1664
TM_SMALL = 2176
TM_COLS = 2304
SMALL_B = 24
SMALL_A = 28
N_SMALL = 32

Q_SCALE = HEAD_DIM ** -0.5 * math.log2(math.e)
TM = 512
VMEM_LIMIT = 56 * 1024 * 1024


def _dot(a, b):
    return jnp.dot(a, b, preferred_element_type=F32)


def _dot_nt(a, b):
    return lax.dot_general(a, b, (((1,), (1,)), ((), ())), preferred_element_type=F32)


def _sigmoid(x):
    return 1.0 / (1.0 + jnp.exp(-x))


def _silu(x):
    return x * _sigmoid(x)


def _softplus(x):
    return jnp.maximum(x, 0.0) + jnp.log(1.0 + jnp.exp(-jnp.abs(x)))


def _split3(x):
    h1 = x.astype(BF16)
    r1 = x - h1.astype(F32)
    h2 = r1.astype(BF16)
    h3 = (r1 - h2.astype(F32)).astype(BF16)
    return h1, h2, h3


def _dot3(mat01, x, nt=False):
    f = (lambda a: _dot_nt(a, mat01)) if nt else (lambda a: _dot(mat01, a))
    h1, h2, h3 = _split3(x)
    return f(h1) + f(h2) + f(h3)


def _rope_t(blk, cosv, sinv):
    half = ROPE_DIM // 2
    parts = []
    for h in range(blk.shape[0] // HEAD_DIM):
        b = h * HEAD_DIM
        x0 = blk[b:b + half]
        x1 = blk[b + half:b + ROPE_DIM]
        parts += [x0 * cosv - x1 * sinv, x1 * cosv + x0 * sinv, blk[b + ROPE_DIM:b + HEAD_DIM]]
    return jnp.concatenate(parts, axis=0)


def _in_proj_kernel(tiles_per_seq, x_ref, pos_ref, g1_ref, wt_ref, w_ref, invf_ref, convw_ref,
                    alog_ref, dtb_ref, alogt_ref, dtbt_ref,
                    qt_ref, kc_ref, ks_ref, kw_ref, vst_ref, vwt_ref, cmpv_ref,
                    gq_ref, gk_ref, gv_ref, z_ref, small_ref, smallt_ref, gt_ref,
                    cbuf):
    tm = x_ref.shape[0]
    nlb = tm // LANES
    x = x_ref[...]
    hn = x * lax.rsqrt(jnp.mean(x * x, axis=-1, keepdims=True) + RMS_EPS) * g1_ref[...]
    hb = hn.astype(BF16)

    first = pl.program_id(0) % tiles_per_seq == 0

    @pl.when(first)
    def _():
        cbuf[0:SUBLANES, :] = jnp.zeros((SUBLANES, 3 * GDN_WIDTH), F32)

    proj = {}

    def gdn_piece(c):
        lo = TM_GDN + 2 * c * LANES
        cbuf[SUBLANES:SUBLANES + tm, 2 * c * LANES:2 * (c + 1) * LANES] = _dot(hb, w_ref[:, lo:lo + 2 * LANES])

    def t_piece(name, lo, hi):
        proj[name] = _dot_nt(wt_ref[lo:hi, :], hb)

    def tm_piece(name, lo, hi):
        proj[name] = _dot(hb, w_ref[:, lo:hi])

    later = [lambda: gdn_piece(2), lambda: t_piece("q01", TR_Q, TR_Q + 2 * LANES),
             lambda: gdn_piece(3), lambda: t_piece("q23", TR_Q + 2 * LANES, TR_K),
             lambda: gdn_piece(4), lambda: t_piece("k", TR_K, TR_V),
             lambda: gdn_piece(5), lambda: t_piece("v", TR_V, TR_ROWS),
             lambda: tm_piece("z0", TM_Z, TM_Z + 2 * LANES), lambda: tm_piece("z1", TM_Z + 2 * LANES, TM_SMALL),
             lambda: tm_piece("vcmp", TM_VCMP, TM_VCMP + LANES), lambda: tm_piece("small", TM_SMALL, TM_COLS)]
    gdn_piece(0)
    gdn_piece(1)

    ones_sq = jnp.ones((LANES, LANES), BF16)

    def conv_block(cb):
        cols = slice(cb * LANES, (cb + 1) * LANES)
        zx = cbuf[0:SUBLANES + tm, cols]
        y = zx * convw_ref[GDN_CONV - 1:GDN_CONV, cols]
        for s in range(1, GDN_CONV):
            y = y + pltpu.roll(zx, s, 0) * convw_ref[GDN_CONV - 1 - s:GDN_CONV - s, cols]
        return _silu(y[SUBLANES:])

    def finish_block(cb, y):
        cols = slice((cb % 4) * LANES, (cb % 4 + 1) * LANES)
        if cb < 8:
            y = y * lax.rsqrt(_dot((y * y).astype(BF16), ones_sq) + RMS_EPS)
        if cb < 4:
            gq_ref[:, cols] = (y * (GDN_DIM ** -0.5)).astype(BF16)
        elif cb < 8:
            gk_ref[:, cols] = y.astype(BF16)
        else:
            gv_ref[:, cols] = y.astype(BF16)

    pending = None
    for cb in range(12):
        later[cb]()
        y = conv_block(cb)
        if pending is not None:
            finish_block(*pending)
        pending = (cb, y)
    finish_block(*pending)
    cbuf[0:SUBLANES, :] = cbuf[tm:tm + SUBLANES, :]

    yt = jnp.concatenate([proj["q01"], proj["q23"], proj["k"], proj["v"]], axis=0)
    ang = invf_ref[...] * pos_ref[...].astype(F32)
    cosv = jnp.cos(ang)
    sinv = jnp.sin(ang)
    for m in range(4):
        blk = _rope_t(yt[TR_Q + m * LANES:TR_Q + (m + 1) * LANES], cosv, sinv)
        qt_ref[m * LANES:(m + 1) * LANES, :] = (blk * Q_SCALE).astype(BF16)
    for j, ref in enumerate((kc_ref, ks_ref, kw_ref)):
        blk = _rope_t(yt[TR_K + j * LANES:TR_K + (j + 1) * LANES], cosv, sinv)
        for c in range(nlb):
            ref[c * LANES:(c + 1) * LANES, :] = blk[:, c * LANES:(c + 1) * LANES].T.astype(ref.dtype)
    for j, ref in enumerate((vst_ref, vwt_ref)):
        blk = yt[TR_V + j * LANES:TR_V + (j + 1) * LANES].astype(BF16)
        for c in range(nlb):
            ref[c] = blk[:, c * LANES:(c + 1) * LANES]
    st = yt[TR_SMALL:TR_SMALL + N_SMALL]
    srow = lax.broadcasted_iota(jnp.int32, (N_SMALL, 1), 0)
    gdec_t = -jnp.exp(alogt_ref[...]) * _softplus(st + dtbt_ref[...])
    small_t = jnp.where(srow < SMALL_A, _sigmoid(st), gdec_t)
    smallt_ref[...] = small_t
    for c in range(nlb):
        gt_ref[c] = small_t[SMALL_B:N_SMALL, c * LANES:(c + 1) * LANES]

    cmpv_ref[...] = proj["vcmp"]
    z_ref[:, 0:2 * LANES] = proj["z0"].astype(BF16)
    z_ref[:, 2 * LANES:GDN_WIDTH] = proj["z1"].astype(BF16)

    sm = proj["small"]
    lane1 = lax.broadcasted_iota(jnp.int32, (1, LANES), 1)
    gdec = -jnp.exp(alog_ref[...]) * _softplus(sm + dtb_ref[...])
    small_ref[...] = jnp.where(lane1 < SMALL_A, _sigmoid(sm), gdec)


def _in_proj(x2, pos_row, g1, w_t, w_r, invf, convw, alog, dtb, alogt, dtbt, seq_len):
    n = x2.shape[0]
    row = lambda i: (i, 0)
    colb = lambda i: (0, i)
    lead = lambda i: (i, 0, 0)
    const = lambda i: (0, 0)
    nlb = TM // LANES
    out_shapes = (
        jax.ShapeDtypeStruct((NSA_WIDTH, n), BF16),
        jax.ShapeDtypeStruct((n, LANES), F32),
        jax.ShapeDtypeStruct((n, LANES), BF16),
        jax.ShapeDtypeStruct((n, LANES), BF16),
        jax.ShapeDtypeStruct((n // LANES, LANES, LANES), BF16),
        jax.ShapeDtypeStruct((n // LANES, LANES, LANES), BF16),
        jax.ShapeDtypeStruct((n, LANES), F32),
        jax.ShapeDtypeStruct((n, GDN_WIDTH), BF16),
        jax.ShapeDtypeStruct((n, GDN_WIDTH), BF16),
        jax.ShapeDtypeStruct((n, GDN_WIDTH), BF16),
        jax.ShapeDtypeStruct((n, GDN_WIDTH), BF16),
        jax.ShapeDtypeStruct((n, LANES), F32),
        jax.ShapeDtypeStruct((N_SMALL, n), F32),
        jax.ShapeDtypeStruct((n // LANES, SUBLANES, LANES), F32),
    )
    out_specs = (
        pl.BlockSpec((NSA_WIDTH, TM), colb),
        pl.BlockSpec((TM, LANES), row), pl.BlockSpec((TM, LANES), row), pl.BlockSpec((TM, LANES), row),
        pl.BlockSpec((nlb, LANES, LANES), lead), pl.BlockSpec((nlb, LANES, LANES), lead),
        pl.BlockSpec((TM, LANES), row),
        pl.BlockSpec((TM, GDN_WIDTH), row), pl.BlockSpec((TM, GDN_WIDTH), row),
        pl.BlockSpec((TM, GDN_WIDTH), row), pl.BlockSpec((TM, GDN_WIDTH), row),
        pl.BlockSpec((TM, LANES), row),
        pl.BlockSpec((N_SMALL, TM), colb),
        pl.BlockSpec((nlb, SUBLANES, LANES), lead),
    )
    in_specs = [
        pl.BlockSpec((TM, D_MODEL), row), pl.BlockSpec((1, TM), colb), pl.BlockSpec((1, D_MODEL), const),
        pl.BlockSpec((TR_ROWS, D_MODEL), const), pl.BlockSpec((D_MODEL, TM_COLS), const),
        pl.BlockSpec((ROPE_DIM // 2, 1), const), pl.BlockSpec((GDN_CONV, 3 * GDN_WIDTH), const),
        pl.BlockSpec((1, LANES), const), pl.BlockSpec((1, LANES), const),
        pl.BlockSpec((N_SMALL, 1), const), pl.BlockSpec((N_SMALL, 1), const),
    ]
    return pl.pallas_call(
        functools.partial(_in_proj_kernel, seq_len // TM),
        grid=(n // TM,), in_specs=in_specs, out_specs=out_specs, out_shape=out_shapes,
        scratch_shapes=[pltpu.VMEM((TM + 2 * SUBLANES, 3 * GDN_WIDTH), F32)],
        compiler_params=pltpu.CompilerParams(dimension_semantics=("arbitrary",),
                                             vmem_limit_bytes=VMEM_LIMIT),
        name="in_proj",
    )(x2, pos_row, g1, w_t, w_r, invf, convw, alog, dtb, alogt, dtbt)


def _prep_in_proj_weights(w_in, gdn_a_log, gdn_dt_bias):
    q = w_in[:, :NSA_WIDTH].reshape(D_MODEL, 2, 4, HEAD_DIM)
    q = jnp.transpose(q, (0, 2, 1, 3)).reshape(D_MODEL, NSA_WIDTH)
    kv = w_in[:, OFF_KV:OFF_GATE].reshape(D_MODEL, 6, LANES)
    gate = w_in[:, OFF_GATE:OFF_GQKV].reshape(D_MODEL, 2, 4, 3)
    gate = jnp.transpose(gate, (0, 3, 2, 1)).reshape(D_MODEL, 24)
    small = jnp.concatenate([gate, w_in[:, OFF_B:OFF_A], w_in[:, OFF_A:OFF_A + GDN_HEADS]], axis=1)
    w_t = jnp.concatenate([q, kv[:, 0], kv[:, 2], kv[:, 4], kv[:, 3], kv[:, 5], small], axis=1).T.astype(BF16)
    w_r = jnp.concatenate([kv[:, 1], w_in[:, OFF_GQKV:OFF_Z], w_in[:, OFF_Z:OFF_B], small,
                           jnp.zeros((D_MODEL, LANES - N_SMALL), w_in.dtype)], axis=1).astype(BF16)
    alog = jnp.zeros((LANES,), F32).at[SMALL_A:SMALL_A + GDN_HEADS].set(gdn_a_log.astype(F32))
    dtb = jnp.zeros((LANES,), F32).at[SMALL_A:SMALL_A + GDN_HEADS].set(gdn_dt_bias.astype(F32))
    return w_t, w_r, alog[None, :], dtb[None, :], alog[:N_SMALL, None], dtb[:N_SMALL, None]


def _rope_inv_freq():
    half = ROPE_DIM // 2
    return jnp.power(ROPE_THETA, -jnp.arange(half, dtype=F32) * (2.0 / ROPE_DIM))[:, None]


def _compress_kernel(xk_ref, xv_ref, pos_ref, w1_ref, w2_ref, out_ref):
    nblk = xk_ref.shape[0] // CMP_STRIDE
    acc_lo = jnp.zeros((nblk, 4 * CMP_HIDDEN), F32)
    acc_hi = jnp.zeros((nblk, 4 * CMP_HIDDEN), F32)
    for j in range(CMP_STRIDE):
        xj = jnp.concatenate([xk_ref[pl.ds(j, nblk, stride=CMP_STRIDE), :],
                              xv_ref[pl.ds(j, nblk, stride=CMP_STRIDE), :]], axis=1)
        acc_lo = acc_lo + _dot((xj + pos_ref[j:j + 1, :]).astype(BF16), w1_ref[j])
        acc_hi = acc_hi + _dot((xj + pos_ref[CMP_STRIDE + j:CMP_STRIDE + j + 1, :]).astype(BF16),
                               w1_ref[CMP_STRIDE + j])
    pre = acc_lo + pltpu.roll(acc_hi, nblk - 1, 0)
    kvc = _dot(_silu(pre).astype(BF16), w2_ref[...])
    out_ref[0:nblk, :] = kvc[:, 0:LANES]
    out_ref[nblk:2 * nblk, :] = kvc[:, LANES:2 * LANES].T


def _compress(cmp_k, cmp_v, pos_rows, w1_bd, w2_bd, seq_len):
    n = cmp_k.shape[0]
    nb = n // seq_len
    nblk = seq_len // CMP_STRIDE
    assert nblk == LANES
    return pl.pallas_call(
        _compress_kernel,
        grid=(nb,),
        in_specs=[pl.BlockSpec((seq_len, LANES), lambda b: (b, 0)),
                  pl.BlockSpec((seq_len, LANES), lambda b: (b, 0)),
                  pl.BlockSpec((CMP_BLOCK, 256), lambda b: (0, 0)),
                  pl.BlockSpec((CMP_BLOCK, 256, 4 * CMP_HIDDEN), lambda b: (0, 0, 0)),
                  pl.BlockSpec((4 * CMP_HIDDEN, 256), lambda b: (0, 0))],
        out_specs=pl.BlockSpec((2 * nblk, LANES), lambda b: (b, 0)),
        out_shape=jax.ShapeDtypeStruct((nb * 2 * nblk, LANES), F32),
        compiler_params=pltpu.CompilerParams(dimension_semantics=("arbitrary",),
                                             vmem_limit_bytes=VMEM_LIMIT),
        name="nsa_compress",
    )(cmp_k, cmp_v, pos_rows, w1_bd, w2_bd)


def _prep_compress_weights(cmp_pos, cmp_w1, cmp_w2):
    slot_src = jnp.array([0, 0, 1, 1])
    eye = jnp.eye(4, dtype=BF16)
    w1 = cmp_w1.reshape(2, CMP_BLOCK, HEAD_DIM, CMP_HIDDEN).astype(BF16)[slot_src]
    w1 = jnp.transpose(w1, (1, 0, 2, 3)).reshape(CMP_BLOCK, 4 * HEAD_DIM, CMP_HIDDEN)
    diag = jnp.repeat(jnp.repeat(eye, HEAD_DIM, axis=0), CMP_HIDDEN, axis=1)
    w1_bd = jnp.tile(w1, (1, 1, 4)) * diag[None]
    w2_bd = cmp_w2.astype(BF16)[slot_src][:, :, None, :] * eye[:, None, :, None]
    w2_bd = w2_bd.reshape(4 * CMP_HIDDEN, 256)
    pos_rows = jnp.concatenate([cmp_pos[0], cmp_pos[0], cmp_pos[1], cmp_pos[1]], axis=-1).astype(F32)
    return pos_rows, w1_bd, w2_bd


TQ = 256
KC = 256
N_SEL = 32
ROWS = N_HEADS * TQ
WCHUNKS = WINDOW // KC + 1
ONES_ROWS = 16


def _nsa_kernel(qt_ref, ks_ref, kw_ref, vst_ref, vwt_ref, kvc_ref, gt_ref, ovt_ref, eblk_ref, o_ref,
                qaug, sbuf, mxbuf, ms_sc, mw_sc, accs_sc, accw_sc, out_sc):
    assert WCHUNKS == 3
    tile = pl.program_id(1)
    t0 = tile * TQ
    tcol = t0 + lax.broadcasted_iota(jnp.int32, (1, TQ), 1)
    krow = lax.broadcasted_iota(jnp.int32, (KC, 1), 0)
    ones = jnp.ones((ONES_ROWS, KC), BF16)
    groups = [slice(r * TQ, (r + 1) * TQ) for r in range(N_HEADS)]
    blocks_per_chunk = KC // LANES

    zhalf = jnp.zeros((HEAD_DIM, TQ), BF16)
    for m in range(4):
        blk = qt_ref[m * LANES:(m + 1) * LANES, :]
        qaug[0:LANES, groups[2 * m]] = jnp.concatenate([blk[0:HEAD_DIM], zhalf], axis=0)
        qaug[0:LANES, groups[2 * m + 1]] = jnp.concatenate([zhalf, blk[HEAD_DIM:LANES]], axis=0)

    def v_chunk(vt_ref, k0):
        b0 = k0 // LANES
        vt = jnp.concatenate([vt_ref[b0 + j] for j in range(blocks_per_chunk)], axis=1)
        return [jnp.concatenate([vt[g * HEAD_DIM:(g + 1) * HEAD_DIM], ones], axis=0).astype(FP8)
                for g in range(N_GROUPS)]

    buf_a, buf_b = sbuf.at[0], sbuf.at[1]
    mx_a, mx_b = mxbuf.at[0], mxbuf.at[1]

    def produce(buf, mx, kmat, bias):
        qrows = kmat.shape[1]
        for cols in groups:
            s = _dot(kmat, qaug[0:qrows, cols])
            if bias is not None:
                s = s + bias
            buf[:, cols] = s
            mx[:, cols] = jnp.max(s, axis=0, keepdims=True)

    def consume(buf, mx, vt, m_ref, acc_ref):
        for r, cols in enumerate(groups):
            m_old = m_ref[:, cols]
            m_new = jnp.maximum(m_old, mx[:, cols])
            p = jnp.exp2((buf[:, cols] - m_new).astype(BF16)).astype(FP8)
            m_ref[:, cols] = m_new
            acc_ref[:, cols] = acc_ref[:, cols] * jnp.exp2(m_old - m_new) + _dot(vt[r % N_GROUPS], p)

    def produce_consume(pbuf, pmx, kmat, bias, cbuf, cmx, vt, m_ref, acc_ref):
        qrows = kmat.shape[1]
        for r, cols in enumerate(groups):
            s = _dot(kmat, qaug[0:qrows, cols])
            if bias is not None:
                s = s + bias
            pbuf[:, cols] = s
            pmx[:, cols] = jnp.max(s, axis=0, keepdims=True)
            m_old = m_ref[:, cols]
            m_new = jnp.maximum(m_old, cmx[:, cols])
            p = jnp.exp2((cbuf[:, cols] - m_new).astype(BF16)).astype(FP8)
            m_ref[:, cols] = m_new
            acc_ref[:, cols] = acc_ref[:, cols] * jnp.exp2(m_old - m_new) + _dot(vt[r % N_GROUPS], p)

    mw_sc[...] = jnp.full(mw_sc.shape, NEG_INF, F32)
    accw_sc[...] = jnp.zeros(accw_sc.shape, F32)

    def window_chunk(j):
        start = t0 - WINDOW + j * KC
        k0 = pl.multiple_of(jnp.maximum(start, 0), KC)
        kpos = start + krow
        diff = tcol - kpos
        bias = jnp.where((kpos >= 0) & (diff >= 0) & (diff < WINDOW), 0.0, NEG_INF)
        return k0, kw_ref[pl.ds(k0, KC), :], bias

    w_k0 = [None] * WCHUNKS
    w_k0[2], kwin, wbias = window_chunk(2)
    produce(buf_a, mx_a, kwin, wbias)
    w_k0[1], kwin, wbias = window_chunk(1)
    produce_consume(buf_b, mx_b, kwin, wbias, buf_a, mx_a, v_chunk(vwt_ref, w_k0[2]), mw_sc, accw_sc)

    nblk = kvc_ref.shape[0] // 2
    kc = kvc_ref[0:nblk, :].astype(BF16)
    vct = kvc_ref[nblk:2 * nblk, :].astype(BF16)
    nrow = lax.broadcasted_iota(jnp.int32, (nblk, 1), 0)
    vbias = jnp.where(nrow * CMP_STRIDE + (CMP_BLOCK - 1) <= tcol, 0.0, NEG_INF)
    has_any = tcol >= CMP_BLOCK - 1
    s_c = [_dot(kc, qaug[0:LANES, cols]) + vbias for cols in groups]
    e_c = [jnp.exp2(y - jnp.max(y, axis=0, keepdims=True)) for y in s_c]
    p_c = [jnp.where(has_any, y * (1.0 / jnp.sum(y, axis=0, keepdims=True)), 0.0) for y in e_c]
    gates = gt_ref[...]
    for r, cols in enumerate(groups):
        g = r % N_GROUPS
        out_sc[:, cols] = gates[r:r + 1] * _dot(vct[g * HEAD_DIM:(g + 1) * HEAD_DIM], p_c[r].astype(BF16))

    w_k0[0], kwin, wbias = window_chunk(0)
    produce_consume(buf_a, mx_a, kwin, wbias, buf_b, mx_b, v_chunk(vwt_ref, w_k0[1]), mw_sc, accw_sc)

    jrow = lax.broadcasted_iota(jnp.int32, (N_SEL, 1), 0)
    cur = lax.shift_right_logical(tcol, 6)
    forced = (jrow == 0) | (jrow == cur) | (jrow == cur - 1)
    causal = jrow <= cur
    for g in range(N_GROUPS):
        psum = (p_c[g] + p_c[2 + g]) + (p_c[4 + g] + p_c[6 + g])
        imp_t = _dot3(ovt_ref[...], psum)
        score = jnp.where(forced, FORCE_SCORE, jnp.where(causal, imp_t[0:N_SEL, :], NEG_INF))
        cnt = jnp.zeros((N_SEL, TQ), jnp.int32)
        for jp in range(N_SEL):
            rowv = score[jp:jp + 1, :]
            beats = (rowv > score) | ((rowv == score) & (jrow > jp))
            cnt = cnt + jnp.where(beats, 1, 0)
        sel = (cnt < SEL_TOP_N) & causal
        bias = jnp.concatenate([jnp.where(sel, 0.0, MASK_BIAS),
                                jnp.zeros((LANES - N_SEL, TQ), F32)], axis=0).astype(BF16)
        for m in range(4):
            qaug[LANES:2 * LANES, groups[2 * m + g]] = bias

    ms_sc[...] = jnp.full(ms_sc.shape, NEG_INF, F32)
    accs_sc[...] = jnp.zeros(accs_sc.shape, F32)

    def sel_keys(c):
        k0 = pl.multiple_of(c * KC, KC)
        return jnp.concatenate([ks_ref[pl.ds(k0, KC), :], eblk_ref[pl.ds(k0, KC), :]], axis=1)

    def sel_consume(buf, mx, c):
        consume(buf, mx, v_chunk(vst_ref, c * KC), ms_sc, accs_sc)

    produce_consume(buf_b, mx_b, sel_keys(tile), jnp.where(t0 + krow <= tcol, 0.0, NEG_INF),
                    buf_a, mx_a, v_chunk(vwt_ref, w_k0[0]), mw_sc, accw_sc)
    n_full = tile
    n_pairs = n_full // 2
    odd = n_full % 2 == 1

    def sel_produce_consume(pbuf, pmx, c_new, cbuf, cmx, c_old):
        produce_consume(pbuf, pmx, sel_keys(c_new), None, cbuf, cmx, v_chunk(vst_ref, c_old * KC), ms_sc, accs_sc)

    def body(j, carry):
        sel_produce_consume(buf_a, mx_a, 2 * j, buf_b, mx_b, jnp.where(j == 0, tile, 2 * j - 1))
        sel_produce_consume(buf_b, mx_b, 2 * j + 1, buf_a, mx_a, 2 * j)
        return carry

    lax.fori_loop(0, n_pairs, body, 0)
    last_b = jnp.where(n_pairs == 0, tile, 2 * n_pairs - 1)

    @pl.when(odd)
    def _():
        sel_produce_consume(buf_a, mx_a, 2 * n_pairs, buf_b, mx_b, last_b)
        sel_consume(buf_a, mx_a, 2 * n_pairs)

    @pl.when(jnp.logical_not(odd))
    def _():
        sel_consume(buf_b, mx_b, last_b)

    for m in range(4):
        halves = []
        for g in range(N_GROUPS):
            r = 2 * m + g
            acc_s = accs_sc[:, groups[r]]
            acc_w = accw_sc[:, groups[r]]
            halves.append(out_sc[:, groups[r]]
                          + (gates[8 + r:9 + r] * (1.0 / acc_s[HEAD_DIM:HEAD_DIM + 1])) * acc_s[0:HEAD_DIM]
                          + (gates[16 + r:17 + r] * (1.0 / acc_w[HEAD_DIM:HEAD_DIM + 1])) * acc_w[0:HEAD_DIM])
        o_ref[:, m * LANES:(m + 1) * LANES] = jnp.concatenate(halves, axis=0).T.astype(BF16)


def _nsa(qt, ks, kw, vst, vwt, kvc, small_t, ovt, eblk, seq_len):
    n = ks.shape[0]
    nb = n // seq_len
    nq = seq_len // TQ
    nkb = seq_len // LANES
    seq = lambda b, i: (b, 0)
    return pl.pallas_call(
        _nsa_kernel,
        grid=(nb, nq),
        in_specs=[pl.BlockSpec((NSA_WIDTH, TQ), lambda b, i: (0, b * nq + i)),
                  pl.BlockSpec((seq_len, LANES), seq), pl.BlockSpec((seq_len, LANES), seq),
                  pl.BlockSpec((nkb, LANES, LANES), lambda b, i: (b, 0, 0)),
                  pl.BlockSpec((nkb, LANES, LANES), lambda b, i: (b, 0, 0)),
                  pl.BlockSpec((2 * LANES, LANES), seq),
                  pl.BlockSpec((N_SMALL, TQ), lambda b, i: (0, b * nq + i)),
                  pl.BlockSpec((LANES, LANES), lambda b, i: (0, 0)),
                  pl.BlockSpec((seq_len, LANES), lambda b, i: (0, 0))],
        out_specs=pl.BlockSpec((TQ, NSA_WIDTH), lambda b, i: (b * nq + i, 0)),
        out_shape=jax.ShapeDtypeStruct((n, NSA_WIDTH), BF16),
        scratch_shapes=[pltpu.VMEM((2 * LANES, ROWS), BF16),
                        pltpu.VMEM((2, KC, ROWS), F32),
                        pltpu.VMEM((2, 1, ROWS), F32),
                        pltpu.VMEM((1, ROWS), F32), pltpu.VMEM((1, ROWS), F32),
                        pltpu.VMEM((HEAD_DIM + ONES_ROWS, ROWS), F32),
                        pltpu.VMEM((HEAD_DIM + ONES_ROWS, ROWS), F32),
                        pltpu.VMEM((HEAD_DIM, ROWS), F32)],
        compiler_params=pltpu.CompilerParams(dimension_semantics=("arbitrary", "arbitrary"),
                                             vmem_limit_bytes=VMEM_LIMIT),
        name="nsa_attention",
    )(qt, ks, kw, vst, vwt, kvc, small_t, ovt, eblk)


def _nsa_constants(seq_len):
    n_cmp = (seq_len - CMP_BLOCK) // CMP_STRIDE + 1
    s = jnp.arange(LANES)[:, None]
    nn = jnp.arange(LANES)[None, :]
    cs = nn * CMP_STRIDE
    ss = s * SEL_BLOCK
    ovt = (cs < ss + SEL_BLOCK) & (cs + CMP_BLOCK > ss) & (s < seq_len // SEL_BLOCK) & (nn < n_cmp)
    k = jnp.arange(seq_len)[:, None]
    eblk = (k // SEL_BLOCK) == jnp.arange(LANES)[None, :]
    return ovt.astype(BF16), eblk.astype(BF16)


PAIR = 2 * GDN_CHUNK
N_DOUBLINGS = 5
GT_G = 4


def _gdn_kernel(q_ref, k_ref, v_ref, z_ref, small_ref, gt_ref, ng_ref, o_ref, s_sc):
    nseq = q_ref.shape[0]
    units = [(s, h) for s in range(nseq) for h in range(GDN_HEADS)]
    us = range(len(units))
    cols = [slice(h * GDN_DIM, (h + 1) * GDN_DIM) for _, h in units]
    ri = lax.broadcasted_iota(jnp.int32, (PAIR, PAIR), 0)
    ci = lax.broadcasted_iota(jnp.int32, (PAIR, PAIR), 1)
    same = lax.shift_right_logical(ri, 6) == lax.shift_right_logical(ci, 6)
    incl = same & (ri >= ci)
    strict = same & (ri > ci)
    first_cols = ci < GDN_CHUNK
    first_row = ci[0:1, :] < GDN_CHUNK
    ltri = jnp.where(incl, 1.0, 0.0).astype(BF16)
    tot = [jnp.where(first_cols, 1.0, 0.0).astype(BF16),
           jnp.where(first_cols, 0.0, 1.0).astype(BF16)]

    @pl.when(pl.program_id(1) == 0)
    def _():
        s_sc[...] = jnp.zeros(s_sc.shape, F32)

    sm_s = [small_ref[s] for s in range(nseq)]
    cs_s = [_dot3(ltri, y) for y in sm_s]
    gt_s = [gt_ref[s, 0] for s in range(nseq)]
    csr_s = [_dot3(ltri, y, nt=True) for y in gt_s]
    glast_s = [[_dot3(tot[c], y, nt=True) for c in range(2)] for y in gt_s]
    seq = [s for s, _ in units]
    q = [q_ref[seq[u], :, cols[u]] for u in us]
    k = [k_ref[seq[u], :, cols[u]] for u in us]
    kf = [y.astype(F32) for y in k]
    beta = [sm_s[s][:, SMALL_B + h:SMALL_B + h + 1] for s, h in units]
    gcc = [cs_s[s][:, SMALL_A + h:SMALL_A + h + 1] for s, h in units]
    gcr = [csr_s[s][GT_G + h:GT_G + h + 1, :] for s, h in units]
    glast = [[glast_s[s][c][GT_G + h:GT_G + h + 1, :] for c in range(2)] for s, h in units]
    decay = [jnp.exp(jnp.where(incl, gcc[u] - gcr[u], NEG_INF)) for u in us]
    egc = [jnp.exp(g) for g in gcc]
    kb = [kf[u] * beta[u] for u in us]
    kk = [_dot_nt(kb[u].astype(BF16), k[u]) for u in us]
    p = [jnp.where(strict, -(kk[u] * decay[u]), 0.0) for u in us]
    x = [jnp.concatenate([v_ref[seq[u], :, cols[u]].astype(F32) * beta[u], kb[u] * egc[u]], axis=1) for u in us]
    for i in range(N_DOUBLINGS + 1):
        pb = [y.astype(BF16) for y in p]
        new_x, new_p = [], []
        for u in us:
            new_x.append(x[u] + _dot(pb[u], x[u].astype(BF16)))
            if i < N_DOUBLINGS:
                new_p.append(_dot(pb[u], pb[u]))
        x, p = new_x, new_p
    uw = [y.astype(BF16) for y in x]
    qk = [(_dot_nt(q[u], k[u]) * decay[u]).astype(BF16) for u in us]
    ke_t = [kf[u].T * jnp.exp(jnp.where(first_row, glast[u][0], glast[u][1]) - gcr[u]) for u in us]
    kw = [[_dot(jnp.where(first_cols, y, 0.0).astype(BF16), uw[u]) for u, y in enumerate(ke_t)],
          [_dot(jnp.where(first_cols, 0.0, y).astype(BF16), uw[u]) for u, y in enumerate(ke_t)]]
    qw = [_dot(qk[u], uw[u]) for u in us]
    qm = [(q[u].astype(F32) * egc[u] - qw[u][:, GDN_DIM:2 * GDN_DIM]).astype(BF16) for u in us]
    s = [s_sc[u] for u in us]
    os = []
    for c in range(2):
        half = slice(c * GDN_CHUNK, (c + 1) * GDN_CHUNK)
        sb = [y.astype(BF16) for y in s]
        os.append([_dot(qm[u][half], sb[u]) for u in us])
        ks = [_dot(kw[c][u][:, GDN_DIM:2 * GDN_DIM].astype(BF16), sb[u]) for u in us]
        s = [s[u] * jnp.exp(glast[u][c]) - ks[u] + kw[c][u][:, 0:GDN_DIM] for u in us]
    for u in us:
        s_sc[u] = s[u]
        o = jnp.concatenate([os[0][u], os[1][u]], axis=0) + qw[u][:, 0:GDN_DIM]
        o = o * lax.rsqrt(jnp.mean(o * o, axis=-1, keepdims=True) + RMS_EPS) * ng_ref[...]
        o_ref[seq[u], :, cols[u]] = (o * _silu(z_ref[seq[u], :, cols[u]].astype(F32))).astype(BF16)


GDN_SEQS = 4


def _gdn(gq, gk, gv, z, small, gt, norm_g, seq_len):
    n = gq.shape[0]
    nb = n // seq_len
    n_pairs = seq_len // PAIR
    nseq = max(d for d in range(1, GDN_SEQS + 1) if nb % d == 0)
    by_seq = lambda a: a.reshape(nb, seq_len, a.shape[-1])
    blk = lambda w: pl.BlockSpec((nseq, PAIR, w), lambda b, i: (b, i, 0))
    out = pl.pallas_call(
        _gdn_kernel,
        grid=(nb // nseq, n_pairs),
        in_specs=[blk(GDN_WIDTH), blk(GDN_WIDTH), blk(GDN_WIDTH), blk(GDN_WIDTH), blk(LANES),
                  pl.BlockSpec((nseq, 1, SUBLANES, PAIR), lambda b, i: (b, i, 0, 0)),
                  pl.BlockSpec((1, GDN_DIM), lambda b, i: (0, 0))],
        out_specs=blk(GDN_WIDTH),
        out_shape=jax.ShapeDtypeStruct((nb, seq_len, GDN_WIDTH), BF16),
        scratch_shapes=[pltpu.VMEM((nseq * GDN_HEADS, GDN_DIM, GDN_DIM), F32)],
        compiler_params=pltpu.CompilerParams(dimension_semantics=("arbitrary", "arbitrary"),
                                             vmem_limit_bytes=VMEM_LIMIT),
        name="gated_delta_rule",
    )(by_seq(gq), by_seq(gk), by_seq(gv), by_seq(z), by_seq(small),
      gt.reshape(nb, n_pairs, SUBLANES, PAIR), norm_g)
    return out.reshape(n, GDN_WIDTH)


FF_CHUNK = 256


def _rms(x, g):
    return x * lax.rsqrt(jnp.mean(x * x, axis=-1, keepdims=True) + RMS_EPS) * g


def _out_mlp_kernel(x_ref, on_ref, og_ref, gn_ref, wo_ref, g2_ref, wg_ref, wu_ref, wd_ref, gf_ref, out_ref,
                    act_sc):
    o_nsa = _rms(on_ref[...].astype(F32), gn_ref[...]).astype(BF16)
    mix = jnp.concatenate([o_nsa, og_ref[...]], axis=1)
    h = x_ref[...] + _dot(mix, wo_ref[...])
    hn = _rms(h, g2_ref[...]).astype(BF16)
    for c in range(D_FF // FF_CHUNK):
        cols = slice(c * FF_CHUNK, (c + 1) * FF_CHUNK)
        act_sc[:, cols] = (_silu(_dot(hn, wg_ref[:, cols])) * _dot(hn, wu_ref[:, cols])).astype(BF16)
    out_ref[...] = _rms(h + _dot(act_sc[...], wd_ref[...]), gf_ref[...])


def _out_mlp(x2, o_nsa, o_gdn, gn, wo, g2, wg, wu, wd, gf):
    n = x2.shape[0]
    row = lambda i: (i, 0)
    const = lambda i: (0, 0)
    resident = lambda shape: pl.BlockSpec(shape, const, pipeline_mode=pl.Buffered(1))
    return pl.pallas_call(
        _out_mlp_kernel,
        grid=(n // TM,),
        in_specs=[pl.BlockSpec((TM, D_MODEL), row), pl.BlockSpec((TM, NSA_WIDTH), row),
                  pl.BlockSpec((TM, GDN_WIDTH), row), pl.BlockSpec((1, NSA_WIDTH), const),
                  resident((D_MODEL, D_MODEL)), pl.BlockSpec((1, D_MODEL), const),
                  resident((D_MODEL, D_FF)), resident((D_MODEL, D_FF)), resident((D_FF, D_MODEL)),
                  pl.BlockSpec((1, D_MODEL), const)],
        out_specs=pl.BlockSpec((TM, D_MODEL), row),
        out_shape=jax.ShapeDtypeStruct((n, D_MODEL), F32),
        scratch_shapes=[pltpu.VMEM((TM, D_FF), BF16)],
        compiler_params=pltpu.CompilerParams(dimension_semantics=("arbitrary",),
                                             vmem_limit_bytes=VMEM_LIMIT),
        name="out_mlp",
    )(x2, o_nsa, o_gdn, gn, wo, g2, wg, wu, wd, gf)


def _nsa_head_perm():
    c = jnp.arange(NSA_WIDTH)
    return ((c // LANES) + 4 * ((c % LANES) // HEAD_DIM)) * HEAD_DIM + c % HEAD_DIM


def _layer(x2, pos_row, seq_len, norm1_g, w_in, cmp_pos, cmp_w1, cmp_w2, nsa_norm_g, gdn_conv_w, gdn_a_log,
           gdn_dt_bias, gdn_norm_g, w_out, norm2_g, w_gate, w_up, w_down, out_g):
    w_t, w_r, alog, dtb, alogt, dtbt = _prep_in_proj_weights(w_in, gdn_a_log, gdn_dt_bias)
    qt, cmp_k, ks, kw, vst, vwt, cmp_v, gq, gk, gv, z, small, small_t, gt = _in_proj(
        x2, pos_row, norm1_g[None, :].astype(F32), w_t, w_r, _rope_inv_freq(), gdn_conv_w.astype(F32),
        alog, dtb, alogt, dtbt, seq_len)
    kvc = _compress(cmp_k, cmp_v, *_prep_compress_weights(cmp_pos, cmp_w1, cmp_w2), seq_len)
    ovt, eblk = _nsa_constants(seq_len)
    o_nsa = _nsa(qt, ks, kw, vst, vwt, kvc, small_t, ovt, eblk, seq_len)
    o_gdn = _gdn(gq, gk, gv, z, small, gt, gdn_norm_g[None, :].astype(F32), seq_len)
    perm = _nsa_head_perm()
    wo = jnp.concatenate([w_out[:NSA_WIDTH][perm], w_out[NSA_WIDTH:]], axis=0).astype(BF16)
    return _out_mlp(x2, o_nsa, o_gdn, nsa_norm_g[perm][None, :].astype(F32), wo, norm2_g[None, :].astype(F32),
                    w_gate.astype(BF16), w_up.astype(BF16), w_down.astype(BF16), out_g[None, :].astype(F32))


def kernel(x, positions, norm1_g, w_in, cmp_pos, cmp_w1, cmp_w2, nsa_norm_g, gdn_conv_w, gdn_a_log,
           gdn_dt_bias, gdn_norm_g, w_out, norm2_g, w_gate, w_up, w_down, final_g):
    nb, seq_len, d = x.shape
    depth = w_in.shape[0]
    assert d == D_MODEL and seq_len % TM == 0 and seq_len // SEL_BLOCK == N_SEL and depth == 1
    x2 = x.reshape(nb * seq_len, d)
    pos_row = positions.reshape(1, nb * seq_len)
    out = _layer(x2, pos_row, seq_len, norm1_g[0], w_in[0], cmp_pos[0], cmp_w1[0], cmp_w2[0], nsa_norm_g[0],
                 gdn_conv_w[0], gdn_a_log[0], gdn_dt_bias[0], gdn_norm_g[0], w_out[0], norm2_g[0],
                 w_gate[0], w_up[0], w_down[0], final_g)
    return out.reshape(nb, seq_len, d)
```
